```python
import math
import jax, jax.numpy as jnp
from jax import lax
import numpy as np

D_MODEL = 2048
BATCH = 4
SEQ = 4096
DEPTH = 2
DEC_BATCH = 4
DEC_SEQ = 2048
PAST_LEN = 128

F32 = jnp.float32
EPS = 1e-6
HEAD_DIM = 128
MIX_WIDTH = D_MODEL
A_Q_HEADS = 6
A_KV_HEADS = 2
A_GROUP = A_Q_HEADS // A_KV_HEADS
WINDOW = 128
BLOCK = 128
B_HEADS = 6
GRID_W = 64
NB_KH_MAX = 8
NB_KW = 16
C_HEADS = 4
C_QK_DIM = 64
C_V_DIM = 128
A_WIDTH = A_Q_HEADS * HEAD_DIM
B_WIDTH = B_HEADS * HEAD_DIM
C_WIDTH = C_HEADS * C_V_DIM
A_KV = A_KV_HEADS * HEAD_DIM
C_QK = C_HEADS * 2 * C_QK_DIM
IN_SIZES = (A_WIDTH, A_KV, A_KV, B_WIDTH, B_WIDTH, B_WIDTH, C_QK, C_QK, C_WIDTH)
IN_WIDTH = sum(IN_SIZES)
REL_BUCKETS = 32
REL_MAX_DIST = 128
REL_COLS = A_Q_HEADS + 2 * C_HEADS
PEER_HEADS = 8
PEER_KEYS = 128
PEER_EXPERTS = PEER_KEYS * PEER_KEYS
PEER_TOPK = 16
PEER_QDIM = 256
PEER_HALF = PEER_QDIM // 2
PEER_CHUNK = 128

kernel_name = 'hymba_peer_hybrid_encoder'


def rmsnorm(x, g):
    xf = x.astype(F32)
    y = xf * lax.rsqrt(jnp.mean(xf * xf, axis=-1, keepdims=True) + EPS)
    return (y * g.astype(F32)).astype(x.dtype)


def t5_bucket(rel):
    nb = REL_BUCKETS // 2
    max_exact = nb // 2
    ret = jnp.where(rel > 0, nb, 0)
    n = jnp.abs(rel)
    nf = jnp.maximum(n, 1).astype(F32)
    large = max_exact + (jnp.log(nf / max_exact) / math.log(REL_MAX_DIST / max_exact) * (nb - max_exact)).astype(jnp.int32)
    large = jnp.minimum(large, nb - 1)
    return ret + jnp.where(n < max_exact, n, large)


def window_gqa(q, k, v, sink, rel_bias_a):
    B_, S = q.shape[0], q.shape[1]
    nb = S // BLOCK
    scale = HEAD_DIM ** -0.5
    qb = q.reshape(B_, nb, BLOCK, A_KV_HEADS, A_GROUP, HEAD_DIM)
    pad = ((0, 0), (BLOCK, BLOCK), (0, 0), (0, 0))

    def band(t):
        tb = jnp.pad(t, pad).reshape(B_, nb + 2, BLOCK, A_KV_HEADS, HEAD_DIM)
        return jnp.concatenate([tb[:, :-2], tb[:, 1:-1], tb[:, 2:]], axis=2)

    kw, vw = band(k), band(v)
    s = jnp.einsum('bnqhgd,bnjhd->bnhgqj', qb, kw).astype(F32) * scale
    rel = jnp.arange(3 * BLOCK)[None, :] - BLOCK - jnp.arange(BLOCK)[:, None]
    bias = rel_bias_a[t5_bucket(rel)].astype(F32)
    bias = bias.transpose(2, 0, 1).reshape(A_KV_HEADS, A_GROUP, BLOCK, 3 * BLOCK)
    kpos = jnp.arange(nb)[:, None] * BLOCK - BLOCK + jnp.arange(3 * BLOCK)[None, :]
    valid = (jnp.abs(rel) <= WINDOW)[None] & ((kpos >= 0) & (kpos < S))[:, None, :]
    s = jnp.where(valid[None, :, None, None], s + bias, -jnp.inf)
    sk = sink.astype(F32).reshape(A_KV_HEADS, A_GROUP)[:, :, None, None]
    m = jnp.maximum(jnp.max(s, axis=-1, keepdims=True), sk)
    p = jnp.exp(s - m)
    denom = jnp.sum(p, axis=-1, keepdims=True) + jnp.exp(sk - m)
    p = (p / denom).astype(v.dtype)
    o = jnp.einsum('bnhgqj,bnjhd->bnqhgd', p, vw)
    return o.reshape(B_, S, A_WIDTH)


def neighborhood_attn(q, k, v, rpb):
    B_, S, H, dh = q.shape
    rows = S // GRID_W
    kh = min(NB_KH_MAX, rows)
    scale = dh ** -0.5
    qg = q.reshape(B_, rows, GRID_W, H, dh)
    kg = k.reshape(B_, rows, GRID_W, H, dh)
    vg = v.reshape(B_, rows, GRID_W, H, dh)
    cols = jnp.arange(GRID_W)
    col_start = jnp.clip(cols - NB_KW // 2, 0, GRID_W - NB_KW)
    col_idx = col_start[:, None] + jnp.arange(NB_KW)[None, :]
    dc = col_idx - cols[:, None]

    def row_fn(r):
        r0 = jnp.clip(r - kh // 2, 0, rows - kh)
        k_nb = lax.dynamic_slice_in_dim(kg, r0, kh, axis=1)[:, :, col_idx]
        v_nb = lax.dynamic_slice_in_dim(vg, r0, kh, axis=1)[:, :, col_idx]
        q_r = lax.dynamic_index_in_dim(qg, r, axis=1, keepdims=False)
        s = jnp.einsum('bqhd,bkqwhd->bhqkw', q_r, k_nb).astype(F32) * scale
        dr = r0 + jnp.arange(kh) - r
        bias = rpb[:, dr[:, None, None] + NB_KH_MAX - 1, dc[None, :, :] + NB_KW - 1]
        s = s + bias.transpose(0, 2, 1, 3).astype(F32)[None]
        p = jax.nn.softmax(s.reshape(B_, H, GRID_W, kh * NB_KW), axis=-1)
        p = p.reshape(B_, H, GRID_W, kh, NB_KW).astype(v.dtype)
        return jnp.einsum('bhqkw,bkqwhd->bqhd', p, v_nb)

    o = lax.map(row_fn, jnp.arange(rows))
    return o.transpose(1, 0, 2, 3, 4).reshape(B_, S, H * dh)


def diff_attn(q, k, v, lam, lam_init, g_subln, rel_bias_c):
    B_, S = q.shape[0], q.shape[1]
    nb = S // BLOCK
    scale = C_QK_DIM ** -0.5
    qb = q.reshape(B_, nb, BLOCK, C_HEADS, 2, C_QK_DIM).transpose(1, 0, 2, 3, 4, 5)
    kpos = jnp.arange(S)

    def blk(args):
        i, qi = args
        s = jnp.einsum('bqhcd,bkhcd->bhcqk', qi, k).astype(F32) * scale
        rel = kpos[None, :] - (i * BLOCK + jnp.arange(BLOCK))[:, None]
        bias = rel_bias_c[t5_bucket(rel)].astype(F32)
        bias = bias.transpose(2, 0, 1).reshape(C_HEADS, 2, BLOCK, S)
        p = jax.nn.softmax(s + bias, axis=-1)
        a = p[:, :, 0] - lam * p[:, :, 1]
        return jnp.einsum('bhqk,bkhd->bqhd', a.astype(v.dtype), v)

    o = lax.map(blk, (jnp.arange(nb), qb))
    o = o.transpose(1, 0, 2, 3, 4).reshape(B_, S, C_HEADS, C_V_DIM)
    o = rmsnorm(o, g_subln) * (1.0 - lam_init)
    return o.reshape(B_, S, C_WIDTH)


def peer_ffn(h, w_q, keys, u, v):
    B_, S, D = h.shape
    t = h.reshape(-1, D)
    n = t.shape[0]
    q = (t @ w_q).reshape(n, PEER_HEADS, 2, PEER_HALF)
    sc = jnp.einsum('nhcd,hckd->nhck', q, keys).astype(F32)
    s1, i1 = lax.top_k(sc[:, :, 0], PEER_TOPK)
    s2, i2 = lax.top_k(sc[:, :, 1], PEER_TOPK)
    cand = (s1[..., :, None] + s2[..., None, :]).reshape(n, PEER_HEADS, PEER_TOPK * PEER_TOPK)
    cidx = (i1[..., :, None] * PEER_KEYS + i2[..., None, :]).reshape(n, PEER_HEADS, PEER_TOPK * PEER_TOPK)
    top_s, pos = lax.top_k(cand, PEER_TOPK)
    eidx = jnp.take_along_axis(cidx, pos, axis=-1)
    g = jax.nn.softmax(top_s, axis=-1)
    nc = n // PEER_CHUNK

    def chunk(args):
        tc, ec, gc = args
        a = jax.nn.gelu(jnp.einsum('cd,chkd->chk', tc, u[ec]).astype(F32), approximate=False)
        w = (gc * a).astype(tc.dtype)
        return jnp.einsum('chk,chkd->cd', w, v[ec])

    out = lax.map(chunk, (t.reshape(nc, PEER_CHUNK, D),
                          eidx.reshape(nc, PEER_CHUNK, PEER_HEADS, PEER_TOPK),
                          g.reshape(nc, PEER_CHUNK, PEER_HEADS, PEER_TOPK)))
    return out.reshape(B_, S, D)


def encoder_layer(x, layer_idx, rel_bias, g_attn, w_in, a_sink, g_grp_a, nat_rpb, g_grp_b,
                  lam_q1, lam_k1, lam_q2, lam_k2, g_subln, w_out, g_ffn,
                  peer_wq, peer_keys, peer_u, peer_v):
    B_, S, _ = x.shape
    h = rmsnorm(x, g_attn)
    proj = h @ w_in
    split_points = [int(c) for c in np.cumsum(IN_SIZES)[:-1]]
    aq, ak, av, bq, bk, bv, cq, ck, cv = jnp.split(proj, split_points, axis=-1)
    ya = window_gqa(aq.reshape(B_, S, A_Q_HEADS, HEAD_DIM),
                    ak.reshape(B_, S, A_KV_HEADS, HEAD_DIM),
                    av.reshape(B_, S, A_KV_HEADS, HEAD_DIM),
                    a_sink, rel_bias[:, :A_Q_HEADS])
    ya = rmsnorm(ya, g_grp_a)
    yb = neighborhood_attn(bq.reshape(B_, S, B_HEADS, HEAD_DIM),
                           bk.reshape(B_, S, B_HEADS, HEAD_DIM),
                           bv.reshape(B_, S, B_HEADS, HEAD_DIM), nat_rpb)
    yb = rmsnorm(yb, g_grp_b)
    lam_init = 0.8 - 0.6 * math.exp(-0.3 * layer_idx)
    lam = (jnp.exp(jnp.sum(lam_q1.astype(F32) * lam_k1.astype(F32)))
           - jnp.exp(jnp.sum(lam_q2.astype(F32) * lam_k2.astype(F32))) + lam_init)
    yc = diff_attn(cq.reshape(B_, S, C_HEADS, 2, C_QK_DIM),
                   ck.reshape(B_, S, C_HEADS, 2, C_QK_DIM),
                   cv.reshape(B_, S, C_HEADS, C_V_DIM),
                   lam, lam_init, g_subln, rel_bias[:, A_Q_HEADS:])
    x = x + jnp.concatenate([ya, yb, yc], axis=-1) @ w_out
    x = x + peer_ffn(rmsnorm(x, g_ffn), peer_wq, peer_keys, peer_u, peer_v)
    return x


def setup_inputs(seed: int = 0) -> dict:
    key = jax.random.key(seed)
    ks = jax.random.split(key, 24)
    nrm = jax.random.normal
    return {
        'x_prompt': nrm(ks[0], (BATCH, SEQ, D_MODEL), F32),
        'x_sample': nrm(ks[1], (DEC_BATCH, DEC_SEQ, D_MODEL), F32),
        'rel_bias': 0.5 * nrm(ks[2], (REL_BUCKETS, REL_COLS), F32),
        'g_attn': 1.0 + 0.02 * nrm(ks[3], (DEPTH, D_MODEL), F32),
        'w_in': nrm(ks[4], (DEPTH, D_MODEL, IN_WIDTH), F32) * D_MODEL ** -0.5,
        'a_sink': 0.5 * nrm(ks[5], (DEPTH, A_Q_HEADS), F32),
        'g_grp_a': 1.0 + 0.02 * nrm(ks[6], (DEPTH, A_WIDTH), F32),
        'nat_rpb': 0.5 * nrm(ks[7], (DEPTH, B_HEADS, 2 * NB_KH_MAX - 1, 2 * NB_KW - 1), F32),
        'g_grp_b': 1.0 + 0.02 * nrm(ks[8], (DEPTH, B_WIDTH), F32),
        'lam_q1': 0.1 * nrm(ks[9], (DEPTH, C_QK_DIM), F32),
        'lam_k1': 0.1 * nrm(ks[10], (DEPTH, C_QK_DIM), F32),
        'lam_q2': 0.1 * nrm(ks[11], (DEPTH, C_QK_DIM), F32),
        'lam_k2': 0.1 * nrm(ks[12], (DEPTH, C_QK_DIM), F32),
        'g_subln': 1.0 + 0.02 * nrm(ks[13], (DEPTH, C_V_DIM), F32),
        'w_out': nrm(ks[14], (DEPTH, MIX_WIDTH, D_MODEL), F32) * MIX_WIDTH ** -0.5,
        'g_ffn': 1.0 + 0.02 * nrm(ks[15], (DEPTH, D_MODEL), F32),
        'peer_wq': nrm(ks[16], (DEPTH, D_MODEL, PEER_HEADS * PEER_QDIM), F32) * D_MODEL ** -0.5,
        'peer_keys': nrm(ks[17], (DEPTH, PEER_HEADS, 2, PEER_KEYS, PEER_HALF), F32) * PEER_HALF ** -0.5,
        'peer_u': nrm(ks[18], (DEPTH, PEER_EXPERTS, D_MODEL), F32) * D_MODEL ** -0.5,
        'peer_v': nrm(ks[19], (DEPTH, PEER_EXPERTS, D_MODEL), F32) * PEER_HEADS ** -0.5,
        'g_final': 1.0 + 0.02 * nrm(ks[20], (D_MODEL,), F32),
    }


def reference(x_prompt, x_sample, rel_bias, g_attn, w_in, a_sink, g_grp_a, nat_rpb, g_grp_b,
              lam_q1, lam_k1, lam_q2, lam_k2, g_subln, w_out, g_ffn,
              peer_wq, peer_keys, peer_u, peer_v, g_final):
    def trunk(x):
        for l in range(DEPTH):
            x = encoder_layer(x, l, rel_bias, g_attn[l], w_in[l], a_sink[l], g_grp_a[l], nat_rpb[l], g_grp_b[l],
                              lam_q1[l], lam_k1[l], lam_q2[l], lam_k2[l], g_subln[l], w_out[l], g_ffn[l],
                              peer_wq[l], peer_keys[l], peer_u[l], peer_v[l])
        return rmsnorm(x, g_final)

    y_prompt = trunk(x_prompt)
    y_sample = trunk(x_sample)
    return (y_prompt, y_sample)
```

```python
import functools
import math

import numpy as np
import jax
import jax.numpy as jnp
from jax import lax
from jax.experimental import pallas as pl
from jax.experimental.pallas import tpu as pltpu

F32 = jnp.float32
BF16 = jnp.bfloat16
I32 = jnp.int32
EPS = 1e-6
NEG = -1e30

D_MODEL = 2048
DEPTH = 2
HEAD_DIM = 128
A_Q_HEADS = 6
A_KV_HEADS = 2
A_GROUP = A_Q_HEADS // A_KV_HEADS
WINDOW = 128
BLOCK = 128
B_HEADS = 6
GRID_W = 64
NB_KH = 8
NB_KW = 16
C_HEADS = 4
C_QK_DIM = 64
C_V_DIM = 128
A_WIDTH = A_Q_HEADS * HEAD_DIM
B_WIDTH = B_HEADS * HEAD_DIM
C_WIDTH = C_HEADS * C_V_DIM
A_KV = A_KV_HEADS * HEAD_DIM
C_QK = C_HEADS * 2 * C_QK_DIM
IN_SIZES = (A_WIDTH, A_KV, A_KV, B_WIDTH, B_WIDTH, B_WIDTH, C_QK, C_QK, C_WIDTH)
IN_WIDTH = sum(IN_SIZES)
REL_BUCKETS = 32
REL_MAX_DIST = 128
PEER_HEADS = 8
PEER_KEYS = 128
PEER_TOPK = 16
PEER_QDIM = 256
PEER_HALF = PEER_QDIM // 2
PEER_PICKS = PEER_HEADS * PEER_TOPK

_PROJ_ORDER = (0, 3, 4, 5, 1, 2, 6, 7, 8)
_COL_AQ, _COL_BQ, _COL_BK, _COL_BV = 0, 1, 2, 3
_COL_AK, _COL_AV = 12, 13
_COL_CQ, _COL_CK, _COL_CV = 7, 8, 9

_VMEM_LIMIT = 56 * 1024 * 1024


def _cparams(sem):
    return pltpu.CompilerParams(dimension_semantics=sem, vmem_limit_bytes=_VMEM_LIMIT)


def _t5_bucket(rel):
    nb = REL_BUCKETS // 2
    max_exact = nb // 2
    ret = jnp.where(rel > 0, nb, 0)
    n = jnp.abs(rel)
    nf = jnp.maximum(n, 1).astype(F32)
    large = max_exact + (jnp.log(nf / max_exact) / math.log(REL_MAX_DIST / max_exact) * (nb - max_exact)).astype(I32)
    large = jnp.minimum(large, nb - 1)
    return ret + jnp.where(n < max_exact, n, large)


def _bias_a_table(rel_bias_a):
    rel = np.arange(3 * BLOCK)[None, :] - BLOCK - np.arange(BLOCK)[:, None]
    bias = rel_bias_a[_t5_bucket(jnp.asarray(rel))].astype(F32)
    bias = jnp.where(jnp.asarray(np.abs(rel) <= WINDOW)[:, :, None], bias, NEG)
    return bias.transpose(2, 0, 1)


def _bias_b_table(rpb):
    classes = (((0, 1), (0, 0)), ((2, 3), (0, 0)), ((4, 5), (0, 1)), ((5, 6), (1, 1)), ((7, 8), (1, 1)))
    qc = np.arange(GRID_W)
    c0 = np.clip(qc - NB_KW // 2, 0, GRID_W - NB_KW)
    kr = np.arange(9)[:, None].repeat(GRID_W, 1).reshape(-1)
    kc = np.arange(GRID_W)[None, :].repeat(9, 0).reshape(-1)
    tabs = []
    for (rq, r0) in classes:
        rows = []
        for a in range(2):
            dr = kr[None, :] - rq[a]
            dc = kc[None, :] - qc[:, None]
            valid = ((kr[None, :] >= r0[a]) & (kr[None, :] < r0[a] + NB_KH)
                     & (kc[None, :] >= c0[:, None]) & (kc[None, :] < c0[:, None] + NB_KW))
            ri = np.clip(dr + NB_KH - 1, 0, 2 * NB_KH - 2) + 0 * dc
            ci = np.clip(dc + NB_KW - 1, 0, 2 * NB_KW - 2)
            b = rpb[:, jnp.asarray(ri), jnp.asarray(ci)].astype(F32)
            rows.append(jnp.where(jnp.asarray(valid)[None], b, NEG))
        tabs.append(jnp.concatenate(rows, axis=1))
    return jnp.stack(tabs)


def _bias_c_tables(rel_bias_c):
    q = np.arange(BLOCK)[:, None]
    k = np.arange(BLOCK)[None, :]
    far_neg = rel_bias_c[_t5_bucket(jnp.asarray(-2 * BLOCK))].astype(F32)
    far_pos = rel_bias_c[_t5_bucket(jnp.asarray(2 * BLOCK))].astype(F32)
    tiles = []
    for d in (-1, 0, 1):
        rel = k + d * BLOCK - q
        t = rel_bias_c[_t5_bucket(jnp.asarray(rel))].astype(F32).transpose(2, 0, 1)
        base = far_neg if d < 0 else far_pos
        tiles.append(t - base[:, None, None])
    return jnp.stack(tiles, axis=1), jnp.stack([far_neg, far_pos], axis=1)


def _norm_proj_kernel(x_ref, g_ref, w_ref, o_ref, h_ref):
    @pl.when(pl.program_id(1) == 0)
    def _():
        x = x_ref[...]
        ms = jnp.mean(x * x, axis=-1, keepdims=True)
        h_ref[...] = (x * lax.rsqrt(ms + EPS) * g_ref[...]).astype(BF16)

    o_ref[...] = jnp.dot(h_ref[...], w_ref[...], preferred_element_type=F32).astype(o_ref.dtype)


def _norm_proj(x, g, w, *, tm=512, tn=1280):
    n, d = x.shape
    width = w.shape[1]
    return pl.pallas_call(
        _norm_proj_kernel,
        grid=(n // tm, width // tn),
        in_specs=[
            pl.BlockSpec((tm, d), lambda i, j: (i, 0)),
            pl.BlockSpec((1, d), lambda i, j: (0, 0)),
            pl.BlockSpec((d, tn), lambda i, j: (0, j)),
        ],
        out_specs=pl.BlockSpec((tm, tn), lambda i, j: (i, j)),
        out_shape=jax.ShapeDtypeStruct((n, width), BF16),
        scratch_shapes=[pltpu.VMEM((tm, d), BF16)],
        compiler_params=_cparams(("parallel", "arbitrary")),
        name="norm_proj",
    )(x, g.reshape(1, d), w)


def _group_norm_store(o_ref, outs, g_ref):
    o = jnp.concatenate(outs, axis=-1)
    ms = jnp.mean(o * o, axis=-1, keepdims=True)
    o_ref[...] = (o * lax.rsqrt(ms + EPS) * g_ref[...]).astype(o_ref.dtype)


def _mixer_a_kernel(q_ref, kp_ref, kc_ref, kn_ref, vp_ref, vc_ref, vn_ref, bias_ref, sink_ref, g_ref, o_ref):
    i = pl.program_id(1)
    nb = pl.num_programs(1)
    scale = HEAD_DIM ** -0.5
    col = lax.broadcasted_iota(I32, (BLOCK, 3 * BLOCK), 1)
    lo = jnp.where(i == 0, BLOCK, 0)
    hi = jnp.where(i == nb - 1, 2 * BLOCK, 3 * BLOCK)
    in_seq = (col >= lo) & (col < hi)
    outs = []
    for kv in range(A_KV_HEADS):
        sl = slice(kv * HEAD_DIM, (kv + 1) * HEAD_DIM)
        k = jnp.concatenate([kp_ref[:, sl], kc_ref[:, sl], kn_ref[:, sl]], axis=0)
        v = jnp.concatenate([vp_ref[:, sl], vc_ref[:, sl], vn_ref[:, sl]], axis=0)
        for gq in range(A_GROUP):
            h = kv * A_GROUP + gq
            q = q_ref[:, h * HEAD_DIM:(h + 1) * HEAD_DIM]
            s = lax.dot_general(q, k, (((1,), (1,)), ((), ())), preferred_element_type=F32) * scale
            s = jnp.where(in_seq, s + bias_ref[h], NEG)
            sk = sink_ref[h]
            m = jnp.maximum(jnp.max(s, axis=-1, keepdims=True), sk)
            p = jnp.exp(s - m)
            denom = jnp.sum(p, axis=-1, keepdims=True) + jnp.exp(sk - m)
            p = (p / denom).astype(BF16)
            outs.append(jnp.dot(p, v, preferred_element_type=F32))
    _group_norm_store(o_ref, outs, g_ref)


def _mixer_a(proj, bias, sink, g, *, row0, n_seq, seq):
    nb = seq // BLOCK
    b0 = row0 // BLOCK

    def blk(shift, colblk):
        return lambda b, i: (b0 + b * nb + jnp.clip(i + shift, 0, nb - 1), colblk)

    kv_spec = lambda shift, colblk: pl.BlockSpec((BLOCK, A_KV), blk(shift, colblk))
    return pl.pallas_call(
        _mixer_a_kernel,
        grid=(n_seq, nb),
        in_specs=[
            pl.BlockSpec((BLOCK, A_WIDTH), blk(0, _COL_AQ)),
            kv_spec(-1, _COL_AK), kv_spec(0, _COL_AK), kv_spec(1, _COL_AK),
            kv_spec(-1, _COL_AV), kv_spec(0, _COL_AV), kv_spec(1, _COL_AV),
            pl.BlockSpec((A_Q_HEADS, BLOCK, 3 * BLOCK), lambda b, i: (0, 0, 0)),
            pl.BlockSpec(memory_space=pltpu.SMEM),
            pl.BlockSpec((1, A_WIDTH), lambda b, i: (0, 0)),
        ],
        out_specs=pl.BlockSpec((BLOCK, A_WIDTH), lambda b, i: (b * nb + i, 0)),
        out_shape=jax.ShapeDtypeStruct((n_seq * seq, A_WIDTH), BF16),
        compiler_params=_cparams(("parallel", "parallel")),
        name="mixer_a",
    )(proj, proj, proj, proj, proj, proj, proj, bias, sink, g.reshape(1, A_WIDTH))


_B_QROWS = 2
_B_KROWS = NB_KH + _B_QROWS - 1


def _mixer_b_kernel(q_ref, k_ref, v_ref, bias_ref, g_ref, o_ref, *, rows):
    i = pl.program_id(1)
    scale = HEAD_DIM ** -0.5
    kstart = jnp.clip(_B_QROWS * i - NB_KH // 2, 0, rows - _B_KROWS)
    ks = pl.multiple_of(kstart * GRID_W, GRID_W)
    nk = _B_KROWS * GRID_W
    outs = []
    for h in range(B_HEADS):
        sl = slice(h * HEAD_DIM, (h + 1) * HEAD_DIM)
        q = q_ref[:, sl]
        k = k_ref[pl.ds(ks, nk), sl]
        v = v_ref[pl.ds(ks, nk), sl]
        s = lax.dot_general(q, k, (((1,), (1,)), ((), ())), preferred_element_type=F32) * scale + bias_ref[0, h]
        m = jnp.max(s, axis=-1, keepdims=True)
        p = jnp.exp(s - m)
        p = (p / jnp.sum(p, axis=-1, keepdims=True)).astype(BF16)
        outs.append(jnp.dot(p, v, preferred_element_type=F32))
    _group_norm_store(o_ref, outs, g_ref)


def _mixer_b(proj, bias, g, *, row0, n_seq, seq):
    rows = seq // GRID_W
    nrp = rows // _B_QROWS
    tq = _B_QROWS * GRID_W
    b0 = row0 // tq
    s0 = row0 // seq

    def cls(i):
        return jnp.where(i == 0, 0, jnp.where(i == 1, 1, jnp.where(i == nrp - 2, 3, jnp.where(i == nrp - 1, 4, 2))))

    return pl.pallas_call(
        functools.partial(_mixer_b_kernel, rows=rows),
        grid=(n_seq, nrp),
        in_specs=[
            pl.BlockSpec((tq, B_WIDTH), lambda b, i: (b0 + b * nrp + i, _COL_BQ)),
            pl.BlockSpec((seq, B_WIDTH), lambda b, i: (s0 + b, _COL_BK)),
            pl.BlockSpec((seq, B_WIDTH), lambda b, i: (s0 + b, _COL_BV)),
            pl.BlockSpec((1, B_HEADS, tq, _B_KROWS * GRID_W), lambda b, i: (cls(i), 0, 0, 0)),
            pl.BlockSpec((1, B_WIDTH), lambda b, i: (0, 0)),
        ],
        out_specs=pl.BlockSpec((tq, B_WIDTH), lambda b, i: (b * nrp + i, 0)),
        out_shape=jax.ShapeDtypeStruct((n_seq * seq, B_WIDTH), BF16),
        compiler_params=_cparams(("parallel", "arbitrary")),
        name="mixer_b",
    )(proj, proj, proj, bias, g.reshape(1, B_WIDTH))


def _mixer_c_kernel(q_ref, k_ref, v_ref, tiles_ref, far_ref, lq1_ref, lk1_ref, lq2_ref, lk2_ref, g_ref, o_ref,
                    *, seq, lam_init):
    i = pl.program_id(1)
    scale = C_QK_DIM ** -0.5
    lam = (jnp.exp(jnp.sum(lq1_ref[...] * lk1_ref[...], axis=-1, keepdims=True))
           - jnp.exp(jnp.sum(lq2_ref[...] * lk2_ref[...], axis=-1, keepdims=True)) + lam_init)
    kblk = jnp.right_shift(lax.broadcasted_iota(I32, (BLOCK, seq), 1), int(math.log2(BLOCK)))
    lane = lax.broadcasted_iota(I32, (BLOCK, HEAD_DIM), 1)
    reps = seq // BLOCK
    outs = []
    for h in range(C_HEADS):
        sl = slice(h * HEAD_DIM, (h + 1) * HEAD_DIM)
        qh = q_ref[:, sl]
        kh = k_ref[:, sl]
        probs = []
        for c in range(2):
            hc = 2 * h + c
            qc = jnp.where((lane >= c * C_QK_DIM) & (lane < (c + 1) * C_QK_DIM), qh, jnp.zeros_like(qh))
            s = lax.dot_general(qc, kh, (((1,), (1,)), ((), ())), preferred_element_type=F32) * scale
            bias = jnp.where(kblk < i, far_ref[hc, 0], far_ref[hc, 1])
            for d in range(3):
                bias = bias + jnp.where(kblk == i + (d - 1), jnp.tile(tiles_ref[hc, d], (1, reps)), 0.0)
            s = s + bias
            m = jnp.max(s, axis=-1, keepdims=True)
            p = jnp.exp(s - m)
            probs.append(p / jnp.sum(p, axis=-1, keepdims=True))
        a = (probs[0] - lam * probs[1]).astype(BF16)
        o = jnp.dot(a, v_ref[:, sl], preferred_element_type=F32)
        ms = jnp.mean(o * o, axis=-1, keepdims=True)
        outs.append(o * lax.rsqrt(ms + EPS) * g_ref[...] * (1.0 - lam_init))
    o_ref[...] = jnp.concatenate(outs, axis=-1).astype(o_ref.dtype)


def _mixer_c(proj, tiles, far, lq1, lk1, lq2, lk2, g, *, row0, n_seq, seq, lam_init):
    nb = seq // BLOCK
    b0 = row0 // BLOCK
    s0 = row0 // seq
    vec = lambda a: a.reshape(1, C_QK_DIM)
    vspec = pl.BlockSpec((1, C_QK_DIM), lambda b, i: (0, 0))
    return pl.pallas_call(
        functools.partial(_mixer_c_kernel, seq=seq, lam_init=lam_init),
        grid=(n_seq, nb),
        in_specs=[
            pl.BlockSpec((BLOCK, C_QK), lambda b, i: (b0 + b * nb + i, _COL_CQ)),
            pl.BlockSpec((seq, C_QK), lambda b, i: (s0 + b, _COL_CK)),
            pl.BlockSpec((seq, C_WIDTH), lambda b, i: (s0 + b, _COL_CV)),
            pl.BlockSpec((2 * C_HEADS, 3, BLOCK, BLOCK), lambda b, i: (0, 0, 0, 0)),
            pl.BlockSpec(memory_space=pltpu.SMEM),
            vspec, vspec, vspec, vspec,
            pl.BlockSpec((1, C_V_DIM), lambda b, i: (0, 0)),
        ],
        out_specs=pl.BlockSpec((BLOCK, C_WIDTH), lambda b, i: (b * nb + i, 0)),
        out_shape=jax.ShapeDtypeStruct((n_seq * seq, C_WIDTH), BF16),
        compiler_params=_cparams(("parallel", "arbitrary")),
        name="mixer_c",
    )(proj, proj, proj, tiles, far, vec(lq1), vec(lk1), vec(lq2), vec(lk2), g.reshape(1, C_V_DIM))


def _out_proj_kernel(x_ref, ya_ref, yb_ref, yc_ref, w_ref, g_ref, x1_ref, h_ref):
    acc = jnp.dot(ya_ref[...], w_ref[0:A_WIDTH, :], preferred_element_type=F32)
    acc += jnp.dot(yb_ref[...], w_ref[A_WIDTH:A_WIDTH + B_WIDTH, :], preferred_element_type=F32)
    acc += jnp.dot(yc_ref[...], w_ref[A_WIDTH + B_WIDTH:, :], preferred_element_type=F32)
    x1 = x_ref[...] + acc
    x1_ref[...] = x1
    ms = jnp.mean(x1 * x1, axis=-1, keepdims=True)
    h_ref[...] = x1 * lax.rsqrt(ms + EPS) * g_ref[...]


def _out_proj(x, ya, yb, yc, w, g, *, tm=256):
    n, d = x.shape
    row = lambda width: pl.BlockSpec((tm, width), lambda i: (i, 0))
    return pl.pallas_call(
        _out_proj_kernel,
        grid=(n // tm,),
        in_specs=[row(d), row(A_WIDTH), row(B_WIDTH), row(C_WIDTH),
                  pl.BlockSpec((d, d), lambda i: (0, 0)),
                  pl.BlockSpec((1, d), lambda i: (0, 0))],
        out_specs=[row(d), row(d)],
        out_shape=[jax.ShapeDtypeStruct((n, d), F32), jax.ShapeDtypeStruct((n, d), F32)],
        compiler_params=_cparams(("parallel",)),
        name="out_proj",
    )(x, ya, yb, yc, w, g.reshape(1, d))


def _top16(x):
    r = x.shape[0]
    iota = lax.broadcasted_iota(I32, x.shape, 0)
    vals, idxs = [], []
    for _ in range(PEER_TOPK):
        m = jnp.max(x, axis=0, keepdims=True)
        idx = jnp.min(jnp.where(x == m, iota, r), axis=0, keepdims=True)
        vals.append(m)
        idxs.append(idx)
        x = jnp.where(iota == idx, -jnp.inf, x)
    return vals, idxs


def _peer_route_kernel(h_ref, wq_ref, keys_ref, eidx_ref, gate_ref):
    q = jnp.dot(h_ref[...].astype(BF16), wq_ref[...], preferred_element_type=F32).astype(BF16)
    tm = q.shape[0]
    for hp in range(PEER_HEADS):
        tops = []
        for c in range(2):
            hc = 2 * hp + c
            qc = q[:, hc * PEER_HALF:(hc + 1) * PEER_HALF]
            sc = lax.dot_general(keys_ref[hc], qc, (((1,), (1,)), ((), ())), preferred_element_type=F32)
            tops.append(_top16(sc))
        (s1, i1), (s2, i2) = tops
        s2a = jnp.concatenate(s2, axis=0)
        i2a = jnp.concatenate(i2, axis=0)
        cand = jnp.concatenate([s1[a] + s2a for a in range(PEER_TOPK)], axis=0)
        cidx = jnp.concatenate([i1[a] * PEER_KEYS + i2a for a in range(PEER_TOPK)], axis=0)
        top_s, pos = _top16(cand)
        iota = lax.broadcasted_iota(I32, cand.shape, 0)
        eidx = [jnp.sum(jnp.where(iota == p, cidx, 0), axis=0, keepdims=True) for p in pos]
        ts = jnp.concatenate(top_s, axis=0)
        e = jnp.exp(ts - top_s[0])
        gate_ref[hp] = e / jnp.sum(e, axis=0, keepdims=True)
        eidx_ref[hp] = jnp.concatenate(eidx, axis=0)


def _peer_route(h, wq, keys, *, tm=256):
    n, d = h.shape
    out_spec = pl.BlockSpec((PEER_HEADS, PEER_TOPK, tm), lambda i: (0, 0, i))
    return pl.pallas_call(
        _peer_route_kernel,
        grid=(n // tm,),
        in_specs=[pl.BlockSpec((tm, d), lambda i: (i, 0)),
                  pl.BlockSpec((d, PEER_HEADS * PEER_QDIM), lambda i: (0, 0)),
                  pl.BlockSpec((2 * PEER_HEADS, PEER_KEYS, PEER_HALF), lambda i: (0, 0, 0))],
        out_specs=[out_spec, out_spec],
        out_shape=[jax.ShapeDtypeStruct((PEER_HEADS, PEER_TOPK, n), I32),
                   jax.ShapeDtypeStruct((PEER_HEADS, PEER_TOPK, n), F32)],
        compiler_params=_cparams(("parallel",)),
        name="peer_route",
    )(h, wq, keys)


def _gelu_exact(x):
    return 0.5 * x * (1.0 + lax.erf(x * (2.0 ** -0.5)))


_PEER_T = 8
_PEER_ROWS = _PEER_T * PEER_PICKS


def _peer_expert_kernel(idx_hbm, gate_ref, h_ref, x_ref, gf_ref, u_hbm, v_hbm, o_ref,
                        idx_smem, ubuf, vbuf, idx_sem, row_sem, *, final_norm):
    i = pl.program_id(0)
    n = pl.num_programs(0)

    def idx_copy(step, slot):
        return pltpu.make_async_copy(idx_hbm.at[step], idx_smem.at[slot], idx_sem.at[slot])

    def issue_rows(slot):
        def body(t, carry):
            for k in range(PEER_PICKS):
                e = idx_smem[slot, t * PEER_PICKS + k]
                r = t * PEER_PICKS + k
                pltpu.make_async_copy(u_hbm.at[e], ubuf.at[slot, r], row_sem.at[0, slot]).start()
                pltpu.make_async_copy(v_hbm.at[e], vbuf.at[slot, r], row_sem.at[1, slot]).start()
            return carry
        lax.fori_loop(0, _PEER_T, body, 0)

    def wait_rows(slot):
        pltpu.make_async_copy(u_hbm.at[pl.ds(0, _PEER_ROWS)], ubuf.at[slot], row_sem.at[0, slot]).wait()
        pltpu.make_async_copy(v_hbm.at[pl.ds(0, _PEER_ROWS)], vbuf.at[slot], row_sem.at[1, slot]).wait()

    slot = i % 2

    @pl.when(i == 0)
    def _():
        idx_copy(0, 0).start()
        idx_copy(0, 0).wait()
        issue_rows(0)

    @pl.when(i + 1 < n)
    def _():
        idx_copy(i + 1, 1 - slot).start()
        idx_copy(i + 1, 1 - slot).wait()
        issue_rows(1 - slot)

    wait_rows(slot)

    outs = []
    for t in range(_PEER_T):
        rows = pl.ds(t * PEER_PICKS, PEER_PICKS)
        ht = h_ref[t:t + 1, :].astype(BF16)
        a = lax.dot_general(ht, ubuf[slot, rows, :].astype(BF16), (((1,), (1,)), ((), ())),
                            preferred_element_type=F32)
        w = (gate_ref[t:t + 1, :] * _gelu_exact(a)).astype(BF16)
        outs.append(jnp.dot(w, vbuf[slot, rows, :].astype(BF16), preferred_element_type=F32))
    y = x_ref[...] + jnp.concatenate(outs, axis=0)
    if final_norm:
        ms = jnp.mean(y * y, axis=-1, keepdims=True)
        y = y * lax.rsqrt(ms + EPS) * gf_ref[...]
    o_ref[...] = y


def _peer_experts(eidx, gate, h, x, u, v, g_final, *, final_norm):
    n, d = x.shape
    steps = n // _PEER_T
    row = lambda width: pl.BlockSpec((_PEER_T, width), lambda i: (i, 0))
    return pl.pallas_call(
        functools.partial(_peer_expert_kernel, final_norm=final_norm),
        grid=(steps,),
        in_specs=[pl.BlockSpec(memory_space=pl.ANY), row(PEER_PICKS), row(d), row(d),
                  pl.BlockSpec((1, d), lambda i: (0, 0)),
                  pl.BlockSpec(memory_space=pl.ANY), pl.BlockSpec(memory_space=pl.ANY)],
        out_specs=row(d),
        out_shape=jax.ShapeDtypeStruct((n, d), F32),
        scratch_shapes=[pltpu.SMEM((2, _PEER_ROWS), I32),
                        pltpu.VMEM((2, _PEER_ROWS, d), F32),
                        pltpu.VMEM((2, _PEER_ROWS, d), F32),
                        pltpu.SemaphoreType.DMA((2,)),
                        pltpu.SemaphoreType.DMA((2, 2))],
        compiler_params=_cparams(("arbitrary",)),
        name="peer_experts",
    )(eidx.reshape(steps, _PEER_ROWS), gate, h, x, g_final.reshape(1, d), u, v)


def _encoder(x, groups, rel_bias, g_attn, w_in, a_sink, g_grp_a, nat_rpb, g_grp_b, lam_q1, lam_k1, lam_q2, lam_k2,
             g_subln, w_out, g_ffn, peer_wq, peer_keys, peer_u, peer_v, g_final):
    depth = w_in.shape[0]
    splits = np.cumsum((0,) + IN_SIZES)
    perm = np.concatenate([np.arange(splits[s], splits[s + 1]) for s in _PROJ_ORDER])
    bias_a = _bias_a_table(rel_bias[:, :A_Q_HEADS])
    tiles_c, far_c = _bias_c_tables(rel_bias[:, A_Q_HEADS:])
    n = x.shape[0]
    for l in range(depth):
        lam_init = 0.8 - 0.6 * math.exp(-0.3 * l)
        w_l = w_in[l][:, perm].astype(BF16)
        proj = _norm_proj(x, g_attn[l], w_l)
        bias_b = _bias_b_table(nat_rpb[l])
        ya, yb, yc = [], [], []
        for (row0, n_seq, seq) in groups:
            kw = dict(row0=row0, n_seq=n_seq, seq=seq)
            ya.append(_mixer_a(proj, bias_a, a_sink[l], g_grp_a[l], **kw))
            yb.append(_mixer_b(proj, bias_b, g_grp_b[l], **kw))
            yc.append(_mixer_c(proj, tiles_c, far_c, lam_q1[l], lam_k1[l], lam_q2[l], lam_k2[l], g_subln[l],
                               lam_init=lam_init, **kw))
        cat = lambda parts: parts[0] if len(parts) == 1 else jnp.concatenate(parts, axis=0)
        x1, h2 = _out_proj(x, cat(ya), cat(yb), cat(yc), w_out[l].astype(BF16), g_ffn[l])
        keys = peer_keys[l].reshape(2 * PEER_HEADS, PEER_KEYS, PEER_HALF).astype(BF16)
        eidx, gate = _peer_route(h2, peer_wq[l].astype(BF16), keys)
        eidx = eidx.reshape(PEER_PICKS, n).T
        gate = gate.reshape(PEER_PICKS, n).T
        x = _peer_experts(eidx, gate, h2, x1, peer_u[l], peer_v[l], g_final, final_norm=(l == depth - 1))
    return x


def kernel(x_prompt, x_sample, rel_bias, g_attn, w_in, a_sink, g_grp_a, nat_rpb, g_grp_b, lam_q1, lam_k1, lam_q2, lam_k2, g_subln, w_out, g_ffn, peer_wq, peer_keys, peer_u, peer_v, g_final):
    bp, sp, d = x_prompt.shape
    bs, ss, _ = x_sample.shape
    x = jnp.concatenate([x_prompt.reshape(bp * sp, d), x_sample.reshape(bs * ss, d)], axis=0)
    groups = ((0, bp, sp), (bp * sp, bs, ss))
    y = _encoder(x, groups, rel_bias, g_attn, w_in, a_sink, g_grp_a, nat_rpb, g_grp_b, lam_q1, lam_k1, lam_q2, lam_k2,
                 g_subln, w_out, g_ffn, peer_wq, peer_keys, peer_u, peer_v, g_final)
    return (y[:bp * sp].reshape(bp, sp, d), y[bp * sp:].reshape(bs, ss, d))
```

```python
import functools
import math

import numpy as np
import jax
import jax.numpy as jnp
from jax import lax
from jax.experimental import pallas as pl
from jax.experimental.pallas import tpu as pltpu

F32 = jnp.float32
BF16 = jnp.bfloat16
I32 = jnp.int32
EPS = 1e-6
NEG = -1e30

D_MODEL = 2048
DEPTH = 2
HEAD_DIM = 128
A_Q_HEADS = 6
A_KV_HEADS = 2
A_GROUP = A_Q_HEADS // A_KV_HEADS
WINDOW = 128
BLOCK = 128
B_HEADS = 6
GRID_W = 64
NB_KH = 8
NB_KW = 16
C_HEADS = 4
C_QK_DIM = 64
C_V_DIM = 128
A_WIDTH = A_Q_HEADS * HEAD_DIM
B_WIDTH = B_HEADS * HEAD_DIM
C_WIDTH = C_HEADS * C_V_DIM
A_KV = A_KV_HEADS * HEAD_DIM
C_QK = C_HEADS * 2 * C_QK_DIM
IN_SIZES = (A_WIDTH, A_KV, A_KV, B_WIDTH, B_WIDTH, B_WIDTH, C_QK, C_QK, C_WIDTH)
IN_WIDTH = sum(IN_SIZES)
REL_BUCKETS = 32
REL_MAX_DIST = 128
PEER_HEADS = 8
PEER_KEYS = 128
PEER_TOPK = 16
PEER_QDIM = 256
PEER_HALF = PEER_QDIM // 2
PEER_PICKS = PEER_HEADS * PEER_TOPK

_PROJ_ORDER = (0, 3, 4, 5, 1, 2, 6, 7, 8)
_COL_AQ, _COL_BQ, _COL_BK, _COL_BV = 0, 1, 2, 3
_COL_AK, _COL_AV = 12, 13
_COL_CQ, _COL_CK, _COL_CV = 7, 8, 9

_VMEM_LIMIT = 56 * 1024 * 1024


def _cparams(sem):
    return pltpu.CompilerParams(dimension_semantics=sem, vmem_limit_bytes=_VMEM_LIMIT)


def _t5_bucket(rel):
    nb = REL_BUCKETS // 2
    max_exact = nb // 2
    ret = jnp.where(rel > 0, nb, 0)
    n = jnp.abs(rel)
    nf = jnp.maximum(n, 1).astype(F32)
    large = max_exact + (jnp.log(nf / max_exact) / math.log(REL_MAX_DIST / max_exact) * (nb - max_exact)).astype(I32)
    large = jnp.minimum(large, nb - 1)
    return ret + jnp.where(n < max_exact, n, large)


def _lookup(table, idx):
    onehot = jax.nn.one_hot(idx, table.shape[0], dtype=F32)
    return jnp.einsum('...b,bc->...c', onehot, table.astype(F32), precision=lax.Precision.HIGHEST)


def _bias_a_table(rel_bias_a):
    rel = np.arange(3 * BLOCK)[None, :] - BLOCK - np.arange(BLOCK)[:, None]
    bias = _lookup(rel_bias_a, _t5_bucket(jnp.asarray(rel)))
    bias = jnp.where(jnp.asarray(np.abs(rel) <= WINDOW)[:, :, None], bias, NEG)
    return bias.transpose(2, 0, 1)


def _bias_b_table(rpb):
    classes = (((0, 1), (0, 0)), ((2, 3), (0, 0)), ((4, 5), (0, 1)), ((5, 6), (1, 1)), ((7, 8), (1, 1)))
    qc = np.arange(GRID_W)[:, None]
    kc = np.arange(GRID_W)[None, :]
    c0 = np.clip(qc - NB_KW // 2, 0, GRID_W - NB_KW)
    in_cols = (kc >= c0) & (kc < c0 + NB_KW)
    n_dc = 2 * NB_KW - 1
    dc_onehot = ((kc - qc + NB_KW - 1)[:, :, None] == np.arange(n_dc)).astype(np.float32)
    blocks = jnp.einsum('qkc,hdc->hdqk', jnp.asarray(dc_onehot), rpb.astype(F32), precision=lax.Precision.HIGHEST)
    blocks = jnp.where(jnp.asarray(in_cols), blocks, NEG)
    neg = jnp.full((rpb.shape[0], GRID_W, GRID_W), NEG, F32)
    tabs = []
    for (rq, r0) in classes:
        rows = []
        for a in range(_B_QROWS):
            parts = [blocks[:, kr - rq[a] + NB_KH - 1] if r0[a] <= kr < r0[a] + NB_KH else neg
                     for kr in range(_B_KROWS)]
            rows.append(jnp.concatenate(parts, axis=-1))
        tabs.append(jnp.concatenate(rows, axis=1))
    return jnp.stack(tabs)


def _bias_c_tables(rel_bias_c):
    q = np.arange(BLOCK)[:, None]
    k = np.arange(BLOCK)[None, :]
    far = _lookup(rel_bias_c, _t5_bucket(jnp.asarray([-2 * BLOCK, 2 * BLOCK])))
    tiles = []
    for d in (-1, 0, 1):
        rel = k + d * BLOCK - q
        t = _lookup(rel_bias_c, _t5_bucket(jnp.asarray(rel))).transpose(2, 0, 1)
        base = far[0] if d < 0 else far[1]
        tiles.append(t - base[:, None, None])
    return jnp.stack(tiles, axis=1), far.T


def _norm_proj_kernel(x_ref, g_ref, w_ref, o_ref, h_ref):
    @pl.when(pl.program_id(1) == 0)
    def _():
        x = x_ref[...]
        ms = jnp.mean(x * x, axis=-1, keepdims=True)
        h_ref[...] = (x * lax.rsqrt(ms + EPS) * g_ref[...]).astype(BF16)

    o_ref[...] = jnp.dot(h_ref[...], w_ref[...], preferred_element_type=F32).astype(o_ref.dtype)


def _norm_proj(x, g, w, *, tm=512, tn=1280):
    n, d = x.shape
    width = w.shape[1]
    return pl.pallas_call(
        _norm_proj_kernel,
        grid=(n // tm, width // tn),
        in_specs=[
            pl.BlockSpec((tm, d), lambda i, j: (i, 0)),
            pl.BlockSpec((1, d), lambda i, j: (0, 0)),
            pl.BlockSpec((d, tn), lambda i, j: (0, j)),
        ],
        out_specs=pl.BlockSpec((tm, tn), lambda i, j: (i, j)),
        out_shape=jax.ShapeDtypeStruct((n, width), BF16),
        scratch_shapes=[pltpu.VMEM((tm, d), BF16)],
        compiler_params=_cparams(("parallel", "arbitrary")),
        name="norm_proj",
    )(x, g.reshape(1, d), w)


def _group_norm_store(o_ref, outs, g_ref):
    o = jnp.concatenate(outs, axis=-1)
    ms = jnp.mean(o * o, axis=-1, keepdims=True)
    o_ref[...] = (o * lax.rsqrt(ms + EPS) * g_ref[...]).astype(o_ref.dtype)


def _mixer_a_kernel(q_ref, kp_ref, kc_ref, kn_ref, vp_ref, vc_ref, vn_ref, bias_ref, sink_ref, g_ref, o_ref):
    i = pl.program_id(1)
    nb = pl.num_programs(1)
    scale = HEAD_DIM ** -0.5
    col = lax.broadcasted_iota(I32, (BLOCK, 3 * BLOCK), 1)
    lo = jnp.where(i == 0, BLOCK, 0)
    hi = jnp.where(i == nb - 1, 2 * BLOCK, 3 * BLOCK)
    in_seq = (col >= lo) & (col < hi)
    outs = []
    for kv in range(A_KV_HEADS):
        sl = slice(kv * HEAD_DIM, (kv + 1) * HEAD_DIM)
        k = jnp.concatenate([kp_ref[:, sl], kc_ref[:, sl], kn_ref[:, sl]], axis=0)
        v = jnp.concatenate([vp_ref[:, sl], vc_ref[:, sl], vn_ref[:, sl]], axis=0)
        for gq in range(A_GROUP):
            h = kv * A_GROUP + gq
            q = q_ref[:, h * HEAD_DIM:(h + 1) * HEAD_DIM]
            s = lax.dot_general(q, k, (((1,), (1,)), ((), ())), preferred_element_type=F32) * scale
            s = jnp.where(in_seq, s + bias_ref[h], NEG)
            sk = sink_ref[h]
            m = jnp.maximum(jnp.max(s, axis=-1, keepdims=True), sk)
            p = jnp.exp(s - m)
            denom = jnp.sum(p, axis=-1, keepdims=True) + jnp.exp(sk - m)
            p = (p / denom).astype(BF16)
            outs.append(jnp.dot(p, v, preferred_element_type=F32))
    _group_norm_store(o_ref, outs, g_ref)


def _mixer_a(proj, bias, sink, g, *, row0, n_seq, seq):
    nb = seq // BLOCK
    b0 = row0 // BLOCK

    def blk(shift, colblk):
        return lambda b, i: (b0 + b * nb + jnp.clip(i + shift, 0, nb - 1), colblk)

    kv_spec = lambda shift, colblk: pl.BlockSpec((BLOCK, A_KV), blk(shift, colblk))
    return pl.pallas_call(
        _mixer_a_kernel,
        grid=(n_seq, nb),
        in_specs=[
            pl.BlockSpec((BLOCK, A_WIDTH), blk(0, _COL_AQ)),
            kv_spec(-1, _COL_AK), kv_spec(0, _COL_AK), kv_spec(1, _COL_AK),
            kv_spec(-1, _COL_AV), kv_spec(0, _COL_AV), kv_spec(1, _COL_AV),
            pl.BlockSpec((A_Q_HEADS, BLOCK, 3 * BLOCK), lambda b, i: (0, 0, 0)),
            pl.BlockSpec(memory_space=pltpu.SMEM),
            pl.BlockSpec((1, A_WIDTH), lambda b, i: (0, 0)),
        ],
        out_specs=pl.BlockSpec((BLOCK, A_WIDTH), lambda b, i: (b * nb + i, 0)),
        out_shape=jax.ShapeDtypeStruct((n_seq * seq, A_WIDTH), BF16),
        compiler_params=_cparams(("parallel", "parallel")),
        name="mixer_a",
    )(proj, proj, proj, proj, proj, proj, proj, bias, sink, g.reshape(1, A_WIDTH))


_B_QROWS = 2
_B_KROWS = NB_KH + _B_QROWS - 1


def _mixer_b_kernel(q_ref, k_ref, v_ref, bias_ref, g_ref, o_ref, *, rows):
    i = pl.program_id(1)
    scale = HEAD_DIM ** -0.5
    kstart = jnp.clip(_B_QROWS * i - NB_KH // 2, 0, rows - _B_KROWS)
    ks = pl.multiple_of(kstart * GRID_W, GRID_W)
    nk = _B_KROWS * GRID_W
    outs = []
    for h in range(B_HEADS):
        sl = slice(h * HEAD_DIM, (h + 1) * HEAD_DIM)
        q = q_ref[:, sl]
        k = k_ref[pl.ds(ks, nk), sl]
        v = v_ref[pl.ds(ks, nk), sl]
        s = lax.dot_general(q, k, (((1,), (1,)), ((), ())), preferred_element_type=F32) * scale + bias_ref[0, h]
        m = jnp.max(s, axis=-1, keepdims=True)
        p = jnp.exp(s - m)
        p = (p / jnp.sum(p, axis=-1, keepdims=True)).astype(BF16)
        outs.append(jnp.dot(p, v, preferred_element_type=F32))
    _group_norm_store(o_ref, outs, g_ref)


def _mixer_b(proj, bias, g, *, row0, n_seq, seq):
    rows = seq // GRID_W
    nrp = rows // _B_QROWS
    tq = _B_QROWS * GRID_W
    b0 = row0 // tq
    s0 = row0 // seq

    def cls(i):
        return jnp.where(i == 0, 0, jnp.where(i == 1, 1, jnp.where(i == nrp - 2, 3, jnp.where(i == nrp - 1, 4, 2))))

    return pl.pallas_call(
        functools.partial(_mixer_b_kernel, rows=rows),
        grid=(n_seq, nrp),
        in_specs=[
            pl.BlockSpec((tq, B_WIDTH), lambda b, i: (b0 + b * nrp + i, _COL_BQ)),
            pl.BlockSpec((seq, B_WIDTH), lambda b, i: (s0 + b, _COL_BK)),
            pl.BlockSpec((seq, B_WIDTH), lambda b, i: (s0 + b, _COL_BV)),
            pl.BlockSpec((1, B_HEADS, tq, _B_KROWS * GRID_W), lambda b, i: (cls(i), 0, 0, 0)),
            pl.BlockSpec((1, B_WIDTH), lambda b, i: (0, 0)),
        ],
        out_specs=pl.BlockSpec((tq, B_WIDTH), lambda b, i: (b * nrp + i, 0)),
        out_shape=jax.ShapeDtypeStruct((n_seq * seq, B_WIDTH), BF16),
        compiler_params=_cparams(("parallel", "arbitrary")),
        name="mixer_b",
    )(proj, proj, proj, bias, g.reshape(1, B_WIDTH))


def _mixer_c_kernel(q_ref, k_ref, v_ref, tiles_ref, far_ref, lq1_ref, lk1_ref, lq2_ref, lk2_ref, g_ref, o_ref,
                    *, seq, lam_init):
    i = pl.program_id(1)
    scale = C_QK_DIM ** -0.5
    lam = (jnp.exp(jnp.sum(lq1_ref[...] * lk1_ref[...], axis=-1, keepdims=True))
           - jnp.exp(jnp.sum(lq2_ref[...] * lk2_ref[...], axis=-1, keepdims=True)) + lam_init)
    kblk = jnp.right_shift(lax.broadcasted_iota(I32, (BLOCK, seq), 1), int(math.log2(BLOCK)))
    lane = lax.broadcasted_iota(I32, (BLOCK, HEAD_DIM), 1)
    reps = seq // BLOCK
    outs = []
    for h in range(C_HEADS):
        sl = slice(h * HEAD_DIM, (h + 1) * HEAD_DIM)
        qh = q_ref[:, sl]
        kh = k_ref[:, sl]
        probs = []
        for c in range(2):
            hc = 2 * h + c
            qc = jnp.where((lane >= c * C_QK_DIM) & (lane < (c + 1) * C_QK_DIM), qh, jnp.zeros_like(qh))
            s = lax.dot_general(qc, kh, (((1,), (1,)), ((), ())), preferred_element_type=F32) * scale
            bias = jnp.where(kblk < i, far_ref[hc, 0], far_ref[hc, 1])
            for d in range(3):
                bias = bias + jnp.where(kblk == i + (d - 1), jnp.tile(tiles_ref[hc, d], (1, reps)), 0.0)
            s = s + bias
            m = jnp.max(s, axis=-1, keepdims=True)
            p = jnp.exp(s - m)
            probs.append(p / jnp.sum(p, axis=-1, keepdims=True))
        a = (probs[0] - lam * probs[1]).astype(BF16)
        o = jnp.dot(a, v_ref[:, sl], preferred_element_type=F32)
        ms = jnp.mean(o * o, axis=-1, keepdims=True)
        outs.append(o * lax.rsqrt(ms + EPS) * g_ref[...] * (1.0 - lam_init))
    o_ref[...] = jnp.concatenate(outs, axis=-1).astype(o_ref.dtype)


def _mixer_c(proj, tiles, far, lq1, lk1, lq2, lk2, g, *, row0, n_seq, seq, lam_init):
    nb = seq // BLOCK
    b0 = row0 // BLOCK
    s0 = row0 // seq
    vec = lambda a: a.reshape(1, C_QK_DIM)
    vspec = pl.BlockSpec((1, C_QK_DIM), lambda b, i: (0, 0))
    return pl.pallas_call(
        functools.partial(_mixer_c_kernel, seq=seq, lam_init=lam_init),
        grid=(n_seq, nb),
        in_specs=[
            pl.BlockSpec((BLOCK, C_QK), lambda b, i: (b0 + b * nb + i, _COL_CQ)),
            pl.BlockSpec((seq, C_QK), lambda b, i: (s0 + b, _COL_CK)),
            pl.BlockSpec((seq, C_WIDTH), lambda b, i: (s0 + b, _COL_CV)),
            pl.BlockSpec((2 * C_HEADS, 3, BLOCK, BLOCK), lambda b, i: (0, 0, 0, 0)),
            pl.BlockSpec(memory_space=pltpu.SMEM),
            vspec, vspec, vspec, vspec,
            pl.BlockSpec((1, C_V_DIM), lambda b, i: (0, 0)),
        ],
        out_specs=pl.BlockSpec((BLOCK, C_WIDTH), lambda b, i: (b * nb + i, 0)),
        out_shape=jax.ShapeDtypeStruct((n_seq * seq, C_WIDTH), BF16),
        compiler_params=_cparams(("parallel", "arbitrary")),
        name="mixer_c",
    )(proj, proj, proj, tiles, far, vec(lq1), vec(lk1), vec(lq2), vec(lk2), g.reshape(1, C_V_DIM))


def _out_proj_kernel(x_ref, ya_ref, yb_ref, yc_ref, w_ref, g_ref, x1_ref, h_ref):
    acc = jnp.dot(ya_ref[...], w_ref[0:A_WIDTH, :], preferred_element_type=F32)
    acc += jnp.dot(yb_ref[...], w_ref[A_WIDTH:A_WIDTH + B_WIDTH, :], preferred_element_type=F32)
    acc += jnp.dot(yc_ref[...], w_ref[A_WIDTH + B_WIDTH:, :], preferred_element_type=F32)
    x1 = x_ref[...] + acc
    x1_ref[...] = x1
    ms = jnp.mean(x1 * x1, axis=-1, keepdims=True)
    h_ref[...] = x1 * lax.rsqrt(ms + EPS) * g_ref[...]


def _out_proj(x, ya, yb, yc, w, g, *, tm=256):
    n, d = x.shape
    row = lambda width: pl.BlockSpec((tm, width), lambda i: (i, 0))
    return pl.pallas_call(
        _out_proj_kernel,
        grid=(n // tm,),
        in_specs=[row(d), row(A_WIDTH), row(B_WIDTH), row(C_WIDTH),
                  pl.BlockSpec((d, d), lambda i: (0, 0)),
                  pl.BlockSpec((1, d), lambda i: (0, 0))],
        out_specs=[row(d), row(d)],
        out_shape=[jax.ShapeDtypeStruct((n, d), F32), jax.ShapeDtypeStruct((n, d), F32)],
        compiler_params=_cparams(("parallel",)),
        name="out_proj",
    )(x, ya, yb, yc, w, g.reshape(1, d))


def _top16(x):
    r = x.shape[0]
    iota = lax.broadcasted_iota(I32, x.shape, 0)
    vals, idxs = [], []
    for _ in range(PEER_TOPK):
        m = jnp.max(x, axis=0, keepdims=True)
        idx = jnp.min(jnp.where(x == m, iota, r), axis=0, keepdims=True)
        vals.append(m)
        idxs.append(idx)
        x = jnp.where(iota == idx, -jnp.inf, x)
    return vals, idxs


def _peer_route_kernel(h_ref, wq_ref, keys_ref, eidx_ref, gate_ref):
    q = jnp.dot(h_ref[...].astype(BF16), wq_ref[...], preferred_element_type=F32).astype(BF16)
    tm = q.shape[0]
    for hp in range(PEER_HEADS):
        tops = []
        for c in range(2):
            hc = 2 * hp + c
            qc = q[:, hc * PEER_HALF:(hc + 1) * PEER_HALF]
            sc = lax.dot_general(keys_ref[hc], qc, (((1,), (1,)), ((), ())), preferred_element_type=F32)
            tops.append(_top16(sc))
        (s1, i1), (s2, i2) = tops
        s2a = jnp.concatenate(s2, axis=0)
        i2a = jnp.concatenate(i2, axis=0)
        cand = jnp.concatenate([s1[a] + s2a for a in range(PEER_TOPK)], axis=0)
        cidx = jnp.concatenate([i1[a] * PEER_KEYS + i2a for a in range(PEER_TOPK)], axis=0)
        top_s, pos = _top16(cand)
        iota = lax.broadcasted_iota(I32, cand.shape, 0)
        eidx = [jnp.sum(jnp.where(iota == p, cidx, 0), axis=0, keepdims=True) for p in pos]
        ts = jnp.concatenate(top_s, axis=0)
        e = jnp.exp(ts - top_s[0])
        gate_ref[hp] = e / jnp.sum(e, axis=0, keepdims=True)
        eidx_ref[hp] = jnp.concatenate(eidx, axis=0)


def _peer_route(h, wq, keys, *, tm=256):
    n, d = h.shape
    out_spec = pl.BlockSpec((PEER_HEADS, PEER_TOPK, tm), lambda i: (0, 0, i))
    return pl.pallas_call(
        _peer_route_kernel,
        grid=(n // tm,),
        in_specs=[pl.BlockSpec((tm, d), lambda i: (i, 0)),
                  pl.BlockSpec((d, PEER_HEADS * PEER_QDIM), lambda i: (0, 0)),
                  pl.BlockSpec((2 * PEER_HEADS, PEER_KEYS, PEER_HALF), lambda i: (0, 0, 0))],
        out_specs=[out_spec, out_spec],
        out_shape=[jax.ShapeDtypeStruct((PEER_HEADS, PEER_TOPK, n), I32),
                   jax.ShapeDtypeStruct((PEER_HEADS, PEER_TOPK, n), F32)],
        compiler_params=_cparams(("parallel",)),
        name="peer_route",
    )(h, wq, keys)


def _gelu_exact(x):
    return 0.5 * x * (1.0 + lax.erf(x * (2.0 ** -0.5)))


_PEER_T = 8
_PEER_ROWS = _PEER_T * PEER_PICKS


def _pack_expert_tables(u, v):
    half = u.shape[1] // 2
    bits = lambda a: lax.bitcast_convert_type(a.astype(BF16), jnp.uint16).astype(jnp.uint32)
    lo = jnp.concatenate([bits(u[:, :half]), bits(v[:, :half])], axis=1)
    hi = jnp.concatenate([bits(u[:, half:]), bits(v[:, half:])], axis=1)
    return lax.bitcast_convert_type(lo | (hi << 16), I32)


def _peer_expert_kernel(idx_hbm, gate_ref, h_ref, x_ref, gf_ref, uv_hbm, o_ref,
                        idx_smem, buf, idx_sem, row_sem, *, final_norm):
    i = pl.program_id(0)
    n = pl.num_programs(0)

    def idx_copy(step, slot):
        dst = idx_smem.at[pl.ds(pl.multiple_of(slot * _PEER_ROWS, _PEER_ROWS), _PEER_ROWS)]
        return pltpu.make_async_copy(idx_hbm.at[step], dst, idx_sem.at[slot])

    def issue_rows(slot):
        def body(t, carry):
            base = slot * _PEER_ROWS + t * PEER_PICKS
            for k in range(PEER_PICKS):
                off = pl.multiple_of(idx_smem[base + k], D_MODEL)
                pltpu.make_async_copy(uv_hbm.at[pl.ds(off, D_MODEL)], buf.at[slot, t, k], row_sem.at[slot]).start()
            return carry
        lax.fori_loop(0, _PEER_T, body, 0)

    def wait_rows(slot):
        pltpu.make_async_copy(buf.at[slot], buf.at[slot], row_sem.at[slot]).wait()

    slot = i % 2

    @pl.when(i == 0)
    def _():
        idx_copy(0, 0).start()
        idx_copy(0, 0).wait()
        issue_rows(0)

    @pl.when(i + 1 < n)
    def _():
        idx_copy(i + 1, 1 - slot).start()
        idx_copy(i + 1, 1 - slot).wait()
        issue_rows(1 - slot)

    wait_rows(slot)

    half = h_ref.shape[1] // 2
    h = h_ref[...]
    hh = jnp.concatenate([h[:, :half], h[:, half:]], axis=0).astype(BF16)
    tok = lax.broadcasted_iota(I32, (_PEER_T, 2 * PEER_PICKS), 0)
    a = jnp.zeros((_PEER_T, 2 * PEER_PICKS), F32)
    for t in range(_PEER_T):
        urows = pltpu.bitcast(buf[slot, t, :, :half], BF16)
        r = lax.dot_general(hh, urows, (((1,), (1,)), ((), ())), preferred_element_type=F32)
        r = r[:_PEER_T] + pltpu.roll(r[_PEER_T:], 2 * PEER_PICKS - 1, axis=1)
        a = jnp.where(tok == t, r, a)
    w_even = gate_ref[...] * _gelu_exact(a)
    w = jnp.concatenate([w_even, pltpu.roll(w_even, 1, axis=1)], axis=0).astype(BF16)
    y = x_ref[...]
    tok = lax.broadcasted_iota(I32, (_PEER_T, half), 0)
    lo = jnp.zeros((_PEER_T, half), F32)
    hi = jnp.zeros((_PEER_T, half), F32)
    for t in range(_PEER_T):
        vrows = pltpu.bitcast(buf[slot, t, :, half:], BF16)
        o = jnp.dot(w, vrows, preferred_element_type=F32)
        lo = jnp.where(tok == t, o[:_PEER_T], lo)
        hi = jnp.where(tok == t, o[_PEER_T:], hi)
    y = y + jnp.concatenate([lo, hi], axis=1)
    if final_norm:
        ms = jnp.mean(y * y, axis=-1, keepdims=True)
        y = y * lax.rsqrt(ms + EPS) * gf_ref[...]
    o_ref[...] = y


def _peer_experts(eidx, gate, h, x, uv, g_final, *, final_norm):
    n, d = x.shape
    steps = n // _PEER_T
    row = lambda width: pl.BlockSpec((_PEER_T, width), lambda i: (i, 0))
    gate2 = jnp.stack([gate, jnp.zeros_like(gate)], axis=-1).reshape(n, 2 * PEER_PICKS)
    return pl.pallas_call(
        functools.partial(_peer_expert_kernel, final_norm=final_norm),
        grid=(steps,),
        in_specs=[pl.BlockSpec(memory_space=pl.ANY), row(2 * PEER_PICKS), row(d), row(d),
                  pl.BlockSpec((1, d), lambda i: (0, 0)),
                  pl.BlockSpec(memory_space=pl.ANY)],
        out_specs=row(d),
        out_shape=jax.ShapeDtypeStruct((n, d), F32),
        scratch_shapes=[pltpu.SMEM((2 * _PEER_ROWS,), I32),
                        pltpu.VMEM((2, _PEER_T, PEER_PICKS, d), I32),
                        pltpu.SemaphoreType.DMA((2,)),
                        pltpu.SemaphoreType.DMA((2,))],
        compiler_params=_cparams(("arbitrary",)),
        name="peer_experts",
    )((eidx * d).reshape(steps, _PEER_ROWS), gate2, h, x, g_final.reshape(1, d), uv.reshape(-1))


def _encoder(x, groups, rel_bias, g_attn, w_in, a_sink, g_grp_a, nat_rpb, g_grp_b, lam_q1, lam_k1, lam_q2, lam_k2,
             g_subln, w_out, g_ffn, peer_wq, peer_keys, peer_u, peer_v, g_final):
    depth = w_in.shape[0]
    splits = np.cumsum((0,) + IN_SIZES)
    perm = np.concatenate([np.arange(splits[s], splits[s + 1]) for s in _PROJ_ORDER])
    bias_a = _bias_a_table(rel_bias[:, :A_Q_HEADS])
    tiles_c, far_c = _bias_c_tables(rel_bias[:, A_Q_HEADS:])
    n = x.shape[0]
    for l in range(depth):
        lam_init = 0.8 - 0.6 * math.exp(-0.3 * l)
        w_l = w_in[l][:, perm].astype(BF16)
        proj = _norm_proj(x, g_attn[l], w_l)
        bias_b = _bias_b_table(nat_rpb[l])
        ya, yb, yc = [], [], []
        for (row0, n_seq, seq) in groups:
            kw = dict(row0=row0, n_seq=n_seq, seq=seq)
            ya.append(_mixer_a(proj, bias_a, a_sink[l], g_grp_a[l], **kw))
            yb.append(_mixer_b(proj, bias_b, g_grp_b[l], **kw))
            yc.append(_mixer_c(proj, tiles_c, far_c, lam_q1[l], lam_k1[l], lam_q2[l], lam_k2[l], g_subln[l],
                               lam_init=lam_init, **kw))
        cat = lambda parts: parts[0] if len(parts) == 1 else jnp.concatenate(parts, axis=0)
        x1, h2 = _out_proj(x, cat(ya), cat(yb), cat(yc), w_out[l].astype(BF16), g_ffn[l])
        keys = peer_keys[l].reshape(2 * PEER_HEADS, PEER_KEYS, PEER_HALF).astype(BF16)
        eidx, gate = _peer_route(h2, peer_wq[l].astype(BF16), keys)
        eidx = eidx.reshape(PEER_PICKS, n).T
        gate = gate.reshape(PEER_PICKS, n).T
        uv = _pack_expert_tables(peer_u[l], peer_v[l])
        x = _peer_experts(eidx, gate, h2, x1, uv, g_final, final_norm=(l == depth - 1))
    return x


def kernel(x_prompt, x_sample, rel_bias, g_attn, w_in, a_sink, g_grp_a, nat_rpb, g_grp_b, lam_q1, lam_k1, lam_q2, lam_k2, g_subln, w_out, g_ffn, peer_wq, peer_keys, peer_u, peer_v, g_final):
    bp, sp, d = x_prompt.shape
    bs, ss, _ = x_sample.shape
    x = jnp.concatenate([x_prompt.reshape(bp * sp, d), x_sample.reshape(bs * ss, d)], axis=0)
    groups = ((0, bp, sp), (bp * sp, bs, ss))
    y = _encoder(x, groups, rel_bias, g_attn, w_in, a_sink, g_grp_a, nat_rpb, g_grp_b, lam_q1, lam_k1, lam_q2, lam_k2,
                 g_subln, w_out, g_ffn, peer_wq, peer_keys, peer_u, peer_v, g_final)
    return (y[:bp * sp].reshape(bp, sp, d), y[bp * sp:].reshape(bs, ss, d))
```

```python
import functools
import math

import numpy as np
import jax
import jax.numpy as jnp
from jax import lax
from jax.experimental import pallas as pl
from jax.experimental.pallas import tpu as pltpu

F32 = jnp.float32
BF16 = jnp.bfloat16
I32 = jnp.int32
EPS = 1e-6
NEG = -1e30

D_MODEL = 2048
DEPTH = 2
HEAD_DIM = 128
A_Q_HEADS = 6
A_KV_HEADS = 2
A_GROUP = A_Q_HEADS // A_KV_HEADS
WINDOW = 128
BLOCK = 128
B_HEADS = 6
GRID_W = 64
NB_KH = 8
NB_KW = 16
C_HEADS = 4
C_QK_DIM = 64
C_V_DIM = 128
A_WIDTH = A_Q_HEADS * HEAD_DIM
B_WIDTH = B_HEADS * HEAD_DIM
C_WIDTH = C_HEADS * C_V_DIM
A_KV = A_KV_HEADS * HEAD_DIM
C_QK = C_HEADS * 2 * C_QK_DIM
IN_SIZES = (A_WIDTH, A_KV, A_KV, B_WIDTH, B_WIDTH, B_WIDTH, C_QK, C_QK, C_WIDTH)
IN_WIDTH = sum(IN_SIZES)
REL_BUCKETS = 32
REL_MAX_DIST = 128
PEER_HEADS = 8
PEER_KEYS = 128
PEER_TOPK = 16
PEER_QDIM = 256
PEER_HALF = PEER_QDIM // 2
PEER_PICKS = PEER_HEADS * PEER_TOPK

_PROJ_ORDER = (0, 3, 4, 5, 1, 2, 6, 7, 8)
_COL_AQ, _COL_BQ, _COL_BK, _COL_BV = 0, 1, 2, 3
_COL_AK, _COL_AV = 12, 13
_COL_CQ, _COL_CK, _COL_CV = 7, 8, 9

_VMEM_LIMIT = 56 * 1024 * 1024


def _cparams(sem):
    return pltpu.CompilerParams(dimension_semantics=sem, vmem_limit_bytes=_VMEM_LIMIT)


def _t5_bucket(rel):
    nb = REL_BUCKETS // 2
    max_exact = nb // 2
    ret = jnp.where(rel > 0, nb, 0)
    n = jnp.abs(rel)
    nf = jnp.maximum(n, 1).astype(F32)
    large = max_exact + (jnp.log(nf / max_exact) / math.log(REL_MAX_DIST / max_exact) * (nb - max_exact)).astype(I32)
    large = jnp.minimum(large, nb - 1)
    return ret + jnp.where(n < max_exact, n, large)


def _lookup(table, idx):
    onehot = jax.nn.one_hot(idx, table.shape[0], dtype=F32)
    return jnp.einsum('...b,bc->...c', onehot, table.astype(F32), precision=lax.Precision.HIGHEST)


def _bias_a_table(rel_bias_a):
    rel = np.arange(3 * BLOCK)[None, :] - BLOCK - np.arange(BLOCK)[:, None]
    bias = _lookup(rel_bias_a, _t5_bucket(jnp.asarray(rel)))
    bias = jnp.where(jnp.asarray(np.abs(rel) <= WINDOW)[:, :, None], bias, NEG)
    return bias.transpose(2, 0, 1)


def _bias_b_table(rpb):
    classes = (((0, 1), (0, 0)), ((2, 3), (0, 0)), ((4, 5), (0, 1)), ((5, 6), (1, 1)), ((7, 8), (1, 1)))
    qc = np.arange(GRID_W)[:, None]
    kc = np.arange(GRID_W)[None, :]
    c0 = np.clip(qc - NB_KW // 2, 0, GRID_W - NB_KW)
    in_cols = (kc >= c0) & (kc < c0 + NB_KW)
    n_dc = 2 * NB_KW - 1
    dc_onehot = ((kc - qc + NB_KW - 1)[:, :, None] == np.arange(n_dc)).astype(np.float32)
    blocks = jnp.einsum('qkc,hdc->hdqk', jnp.asarray(dc_onehot), rpb.astype(F32), precision=lax.Precision.HIGHEST)
    blocks = jnp.where(jnp.asarray(in_cols), blocks, NEG)
    neg = jnp.full((rpb.shape[0], GRID_W, GRID_W), NEG, F32)
    tabs = []
    for (rq, r0) in classes:
        rows = []
        for a in range(_B_QROWS):
            parts = [blocks[:, kr - rq[a] + NB_KH - 1] if r0[a] <= kr < r0[a] + NB_KH else neg
                     for kr in range(_B_KROWS)]
            rows.append(jnp.concatenate(parts, axis=-1))
        tabs.append(jnp.concatenate(rows, axis=1))
    return jnp.stack(tabs)


def _bias_c_tables(rel_bias_c):
    q = np.arange(BLOCK)[:, None]
    k = np.arange(BLOCK)[None, :]
    far = _lookup(rel_bias_c, _t5_bucket(jnp.asarray([-2 * BLOCK, 2 * BLOCK])))
    tiles = []
    for d in (-1, 0, 1):
        rel = k + d * BLOCK - q
        t = _lookup(rel_bias_c, _t5_bucket(jnp.asarray(rel))).transpose(2, 0, 1)
        base = far[0] if d < 0 else far[1]
        tiles.append(t - base[:, None, None])
    return jnp.stack(tiles, axis=1), far.T


def _norm_proj_kernel(x_ref, g_ref, w_ref, o_ref, h_ref):
    @pl.when(pl.program_id(1) == 0)
    def _():
        x = x_ref[...]
        ms = jnp.mean(x * x, axis=-1, keepdims=True)
        h_ref[...] = (x * lax.rsqrt(ms + EPS) * g_ref[...]).astype(BF16)

    o_ref[...] = jnp.dot(h_ref[...], w_ref[...], preferred_element_type=F32).astype(o_ref.dtype)


def _norm_proj(x, g, w, *, tm=512, tn=1280):
    n, d = x.shape
    width = w.shape[1]
    return pl.pallas_call(
        _norm_proj_kernel,
        grid=(n // tm, width // tn),
        in_specs=[
            pl.BlockSpec((tm, d), lambda i, j: (i, 0)),
            pl.BlockSpec((1, d), lambda i, j: (0, 0)),
            pl.BlockSpec((d, tn), lambda i, j: (0, j)),
        ],
        out_specs=pl.BlockSpec((tm, tn), lambda i, j: (i, j)),
        out_shape=jax.ShapeDtypeStruct((n, width), BF16),
        scratch_shapes=[pltpu.VMEM((tm, d), BF16)],
        compiler_params=_cparams(("parallel", "arbitrary")),
        name="norm_proj",
    )(x, g.reshape(1, d), w)


def _group_norm_store(o_ref, outs, g_ref):
    o = jnp.concatenate(outs, axis=-1)
    ms = jnp.mean(o * o, axis=-1, keepdims=True)
    o_ref[...] = (o * lax.rsqrt(ms + EPS) * g_ref[...]).astype(o_ref.dtype)


def _mixer_a_kernel(q_ref, kp_ref, kc_ref, kn_ref, vp_ref, vc_ref, vn_ref, bias_ref, sink_ref, g_ref, o_ref):
    i = pl.program_id(1)
    nb = pl.num_programs(1)
    scale = HEAD_DIM ** -0.5
    col = lax.broadcasted_iota(I32, (BLOCK, 3 * BLOCK), 1)
    lo = jnp.where(i == 0, BLOCK, 0)
    hi = jnp.where(i == nb - 1, 2 * BLOCK, 3 * BLOCK)
    in_seq = (col >= lo) & (col < hi)
    outs = []
    for kv in range(A_KV_HEADS):
        sl = slice(kv * HEAD_DIM, (kv + 1) * HEAD_DIM)
        k = jnp.concatenate([kp_ref[:, sl], kc_ref[:, sl], kn_ref[:, sl]], axis=0)
        v = jnp.concatenate([vp_ref[:, sl], vc_ref[:, sl], vn_ref[:, sl]], axis=0)
        for gq in range(A_GROUP):
            h = kv * A_GROUP + gq
            q = q_ref[:, h * HEAD_DIM:(h + 1) * HEAD_DIM]
            s = lax.dot_general(q, k, (((1,), (1,)), ((), ())), preferred_element_type=F32) * scale
            s = jnp.where(in_seq, s + bias_ref[h], NEG)
            sk = sink_ref[h]
            m = jnp.maximum(jnp.max(s, axis=-1, keepdims=True), sk)
            p = jnp.exp(s - m)
            denom = jnp.sum(p, axis=-1, keepdims=True) + jnp.exp(sk - m)
            p = (p / denom).astype(BF16)
            outs.append(jnp.dot(p, v, preferred_element_type=F32))
    _group_norm_store(o_ref, outs, g_ref)


def _mixer_a(proj, bias, sink, g, *, row0, n_seq, seq):
    nb = seq // BLOCK
    b0 = row0 // BLOCK

    def blk(shift, colblk):
        return lambda b, i: (b0 + b * nb + jnp.clip(i + shift, 0, nb - 1), colblk)

    kv_spec = lambda shift, colblk: pl.BlockSpec((BLOCK, A_KV), blk(shift, colblk))
    return pl.pallas_call(
        _mixer_a_kernel,
        grid=(n_seq, nb),
        in_specs=[
            pl.BlockSpec((BLOCK, A_WIDTH), blk(0, _COL_AQ)),
            kv_spec(-1, _COL_AK), kv_spec(0, _COL_AK), kv_spec(1, _COL_AK),
            kv_spec(-1, _COL_AV), kv_spec(0, _COL_AV), kv_spec(1, _COL_AV),
            pl.BlockSpec((A_Q_HEADS, BLOCK, 3 * BLOCK), lambda b, i: (0, 0, 0)),
            pl.BlockSpec(memory_space=pltpu.SMEM),
            pl.BlockSpec((1, A_WIDTH), lambda b, i: (0, 0)),
        ],
        out_specs=pl.BlockSpec((BLOCK, A_WIDTH), lambda b, i: (b * nb + i, 0)),
        out_shape=jax.ShapeDtypeStruct((n_seq * seq, A_WIDTH), BF16),
        compiler_params=_cparams(("parallel", "parallel")),
        name="mixer_a",
    )(proj, proj, proj, proj, proj, proj, proj, bias, sink, g.reshape(1, A_WIDTH))


_B_QROWS = 2
_B_KROWS = NB_KH + _B_QROWS - 1


def _mixer_b_kernel(q_ref, k_ref, v_ref, bias_ref, g_ref, o_ref, *, rows):
    i = pl.program_id(1)
    scale = HEAD_DIM ** -0.5
    kstart = jnp.clip(_B_QROWS * i - NB_KH // 2, 0, rows - _B_KROWS)
    ks = pl.multiple_of(kstart * GRID_W, GRID_W)
    nk = _B_KROWS * GRID_W
    outs = []
    for h in range(B_HEADS):
        sl = slice(h * HEAD_DIM, (h + 1) * HEAD_DIM)
        q = q_ref[:, sl]
        k = k_ref[pl.ds(ks, nk), sl]
        v = v_ref[pl.ds(ks, nk), sl]
        s = lax.dot_general(q, k, (((1,), (1,)), ((), ())), preferred_element_type=F32) * scale + bias_ref[0, h]
        m = jnp.max(s, axis=-1, keepdims=True)
        p = jnp.exp(s - m)
        p = (p / jnp.sum(p, axis=-1, keepdims=True)).astype(BF16)
        outs.append(jnp.dot(p, v, preferred_element_type=F32))
    _group_norm_store(o_ref, outs, g_ref)


def _mixer_b(proj, bias, g, *, row0, n_seq, seq):
    rows = seq // GRID_W
    nrp = rows // _B_QROWS
    tq = _B_QROWS * GRID_W
    b0 = row0 // tq
    s0 = row0 // seq

    def cls(i):
        return jnp.where(i == 0, 0, jnp.where(i == 1, 1, jnp.where(i == nrp - 2, 3, jnp.where(i == nrp - 1, 4, 2))))

    return pl.pallas_call(
        functools.partial(_mixer_b_kernel, rows=rows),
        grid=(n_seq, nrp),
        in_specs=[
            pl.BlockSpec((tq, B_WIDTH), lambda b, i: (b0 + b * nrp + i, _COL_BQ)),
            pl.BlockSpec((seq, B_WIDTH), lambda b, i: (s0 + b, _COL_BK)),
            pl.BlockSpec((seq, B_WIDTH), lambda b, i: (s0 + b, _COL_BV)),
            pl.BlockSpec((1, B_HEADS, tq, _B_KROWS * GRID_W), lambda b, i: (cls(i), 0, 0, 0)),
            pl.BlockSpec((1, B_WIDTH), lambda b, i: (0, 0)),
        ],
        out_specs=pl.BlockSpec((tq, B_WIDTH), lambda b, i: (b * nrp + i, 0)),
        out_shape=jax.ShapeDtypeStruct((n_seq * seq, B_WIDTH), BF16),
        compiler_params=_cparams(("parallel", "arbitrary")),
        name="mixer_b",
    )(proj, proj, proj, bias, g.reshape(1, B_WIDTH))


def _mixer_c_kernel(q_ref, k_ref, v_ref, tiles_ref, far_ref, lq1_ref, lk1_ref, lq2_ref, lk2_ref, g_ref, o_ref,
                    s_ref, *, seq, lam_init):
    i = pl.program_id(1)
    nb = seq // BLOCK
    scale = C_QK_DIM ** -0.5
    lam = (jnp.exp(jnp.sum(lq1_ref[...] * lk1_ref[...], axis=-1, keepdims=True))
           - jnp.exp(jnp.sum(lq2_ref[...] * lk2_ref[...], axis=-1, keepdims=True)) + lam_init)
    before = lax.broadcasted_iota(I32, (1, seq), 1) < i * BLOCK
    lane = lax.broadcasted_iota(I32, (BLOCK, HEAD_DIM), 1)
    outs = []
    for h in range(C_HEADS):
        sl = slice(h * HEAD_DIM, (h + 1) * HEAD_DIM)
        qh = q_ref[:, sl] * scale
        kh = k_ref[:, sl]
        exps, sums = [], []
        for c in range(2):
            hc = 2 * h + c
            qc = jnp.where((lane >= c * C_QK_DIM) & (lane < (c + 1) * C_QK_DIM), qh, jnp.zeros_like(qh))
            s = lax.dot_general(qc, kh, (((1,), (1,)), ((), ())), preferred_element_type=F32)
            s_ref[c] = s + jnp.where(before, far_ref[hc, 0], far_ref[hc, 1])
            for d in range(3):
                j = i + (d - 1)
                inside = jnp.where((j >= 0) & (j < nb), 1.0, 0.0)
                cols = pl.ds(pl.multiple_of(jnp.clip(j, 0, nb - 1) * BLOCK, BLOCK), BLOCK)
                s_ref[c, :, cols] += tiles_ref[hc, d] * inside
            s = s_ref[c]
            e = jnp.exp(s - jnp.max(s, axis=-1, keepdims=True))
            exps.append(e)
            sums.append(jnp.sum(e, axis=-1, keepdims=True))
        a = (exps[0] - (lam * sums[0] / sums[1]) * exps[1]).astype(BF16)
        o = jnp.dot(a, v_ref[:, sl], preferred_element_type=F32) / sums[0]
        ms = jnp.mean(o * o, axis=-1, keepdims=True)
        outs.append(o * lax.rsqrt(ms + EPS) * g_ref[...] * (1.0 - lam_init))
    o_ref[...] = jnp.concatenate(outs, axis=-1).astype(o_ref.dtype)


def _mixer_c(proj, tiles, far, lq1, lk1, lq2, lk2, g, *, row0, n_seq, seq, lam_init):
    nb = seq // BLOCK
    b0 = row0 // BLOCK
    s0 = row0 // seq
    vec = lambda a: a.reshape(1, C_QK_DIM)
    vspec = pl.BlockSpec((1, C_QK_DIM), lambda b, i: (0, 0))
    return pl.pallas_call(
        functools.partial(_mixer_c_kernel, seq=seq, lam_init=lam_init),
        grid=(n_seq, nb),
        in_specs=[
            pl.BlockSpec((BLOCK, C_QK), lambda b, i: (b0 + b * nb + i, _COL_CQ)),
            pl.BlockSpec((seq, C_QK), lambda b, i: (s0 + b, _COL_CK)),
            pl.BlockSpec((seq, C_WIDTH), lambda b, i: (s0 + b, _COL_CV)),
            pl.BlockSpec((2 * C_HEADS, 3, BLOCK, BLOCK), lambda b, i: (0, 0, 0, 0)),
            pl.BlockSpec(memory_space=pltpu.SMEM),
            vspec, vspec, vspec, vspec,
            pl.BlockSpec((1, C_V_DIM), lambda b, i: (0, 0)),
        ],
        out_specs=pl.BlockSpec((BLOCK, C_WIDTH), lambda b, i: (b * nb + i, 0)),
        out_shape=jax.ShapeDtypeStruct((n_seq * seq, C_WIDTH), BF16),
        scratch_shapes=[pltpu.VMEM((2, BLOCK, seq), F32)],
        compiler_params=_cparams(("parallel", "arbitrary")),
        name="mixer_c",
    )(proj, proj, proj, tiles, far, vec(lq1), vec(lk1), vec(lq2), vec(lk2), g.reshape(1, C_V_DIM))


def _out_proj_kernel(x_ref, ya_ref, yb_ref, yc_ref, w_ref, g_ref, x1_ref, h_ref):
    acc = jnp.dot(ya_ref[...], w_ref[0:A_WIDTH, :], preferred_element_type=F32)
    acc += jnp.dot(yb_ref[...], w_ref[A_WIDTH:A_WIDTH + B_WIDTH, :], preferred_element_type=F32)
    acc += jnp.dot(yc_ref[...], w_ref[A_WIDTH + B_WIDTH:, :], preferred_element_type=F32)
    x1 = x_ref[...] + acc
    x1_ref[...] = x1
    ms = jnp.mean(x1 * x1, axis=-1, keepdims=True)
    h_ref[...] = x1 * lax.rsqrt(ms + EPS) * g_ref[...]


def _out_proj(x, ya, yb, yc, w, g, *, tm=256):
    n, d = x.shape
    row = lambda width: pl.BlockSpec((tm, width), lambda i: (i, 0))
    return pl.pallas_call(
        _out_proj_kernel,
        grid=(n // tm,),
        in_specs=[row(d), row(A_WIDTH), row(B_WIDTH), row(C_WIDTH),
                  pl.BlockSpec((d, d), lambda i: (0, 0)),
                  pl.BlockSpec((1, d), lambda i: (0, 0))],
        out_specs=[row(d), row(d)],
        out_shape=[jax.ShapeDtypeStruct((n, d), F32), jax.ShapeDtypeStruct((n, d), F32)],
        compiler_params=_cparams(("parallel",)),
        name="out_proj",
    )(x, ya, yb, yc, w, g.reshape(1, d))


def _top16(x):
    r = x.shape[0]
    iota = lax.broadcasted_iota(I32, x.shape, 0)
    vals, idxs = [], []
    for _ in range(PEER_TOPK):
        m = jnp.max(x, axis=0, keepdims=True)
        idx = jnp.min(jnp.where(x == m, iota, r), axis=0, keepdims=True)
        vals.append(m)
        idxs.append(idx)
        x = jnp.where(iota == idx, -jnp.inf, x)
    return vals, idxs


def _peer_candidates(s1, i1, s2, i2):
    sub = 8
    s1a, i1a = jnp.concatenate(s1, axis=0), jnp.concatenate(i1, axis=0)
    s2a, i2a = jnp.concatenate(s2, axis=0), jnp.concatenate(i2, axis=0)
    row = lax.broadcasted_iota(I32, s2a[:sub].shape, 0)
    cand, cidx = [], []
    first_single = PEER_TOPK // 2
    for a in range(first_single):
        nb = PEER_TOPK // (a + 1)
        if nb > sub:
            cand.append(s1[a] + s2a)
            cidx.append(i1[a] * PEER_KEYS + i2a)
        else:
            cand.append(jnp.where(row < nb, s1[a] + s2a[:sub], -jnp.inf))
            cidx.append(i1[a] * PEER_KEYS + i2a[:sub])
    cand.append(s1a[first_single:] + s2[0])
    cidx.append(i1a[first_single:] * PEER_KEYS + i2[0])
    return jnp.concatenate(cand, axis=0), jnp.concatenate(cidx, axis=0)


def _peer_route_kernel(h_ref, wq_ref, keys_ref, eidx_ref, gate_ref):
    q = jnp.dot(h_ref[...].astype(BF16), wq_ref[...], preferred_element_type=F32).astype(BF16)
    tm = q.shape[0]
    for hp in range(PEER_HEADS):
        tops = []
        for c in range(2):
            hc = 2 * hp + c
            qc = q[:, hc * PEER_HALF:(hc + 1) * PEER_HALF]
            sc = lax.dot_general(keys_ref[hc], qc, (((1,), (1,)), ((), ())), preferred_element_type=F32)
            tops.append(_top16(sc))
        (s1, i1), (s2, i2) = tops
        cand, cidx = _peer_candidates(s1, i1, s2, i2)
        top_s, pos = _top16(cand)
        iota = lax.broadcasted_iota(I32, cand.shape, 0)
        eidx = [jnp.sum(jnp.where(iota == p, cidx, 0), axis=0, keepdims=True) for p in pos]
        ts = jnp.concatenate(top_s, axis=0)
        e = jnp.exp(ts - top_s[0])
        gate_ref[hp] = e / jnp.sum(e, axis=0, keepdims=True)
        eidx_ref[hp] = jnp.concatenate(eidx, axis=0)


def _peer_route(h, wq, keys, *, tm=256):
    n, d = h.shape
    out_spec = pl.BlockSpec((PEER_HEADS, PEER_TOPK, tm), lambda i: (0, 0, i))
    return pl.pallas_call(
        _peer_route_kernel,
        grid=(n // tm,),
        in_specs=[pl.BlockSpec((tm, d), lambda i: (i, 0)),
                  pl.BlockSpec((d, PEER_HEADS * PEER_QDIM), lambda i: (0, 0)),
                  pl.BlockSpec((2 * PEER_HEADS, PEER_KEYS, PEER_HALF), lambda i: (0, 0, 0))],
        out_specs=[out_spec, out_spec],
        out_shape=[jax.ShapeDtypeStruct((PEER_HEADS, PEER_TOPK, n), I32),
                   jax.ShapeDtypeStruct((PEER_HEADS, PEER_TOPK, n), F32)],
        compiler_params=_cparams(("parallel",)),
        name="peer_route",
    )(h, wq, keys)


def _gelu_exact(x):
    return 0.5 * x * (1.0 + lax.erf(x * (2.0 ** -0.5)))


_PEER_T = 8
_DMA_QUEUES = 2
_PEER_ROWS = _PEER_T * PEER_PICKS


def _pack_expert_tables(u, v):
    half = u.shape[1] // 2
    bits = lambda a: lax.bitcast_convert_type(a.astype(BF16), jnp.uint16).astype(jnp.uint32)
    lo = jnp.concatenate([bits(u[:, :half]), bits(v[:, :half])], axis=1)
    hi = jnp.concatenate([bits(u[:, half:]), bits(v[:, half:])], axis=1)
    return lax.bitcast_convert_type(lo | (hi << 16), I32)


def _peer_expert_kernel(idx_hbm, gate_ref, h_ref, x_ref, gf_ref, uv_hbm, o_ref,
                        idx_smem, buf, idx_sem, row_sem, *, final_norm):
    i = pl.program_id(0)
    n = pl.num_programs(0)

    def idx_copy(step, slot):
        dst = idx_smem.at[pl.ds(pl.multiple_of(slot * _PEER_ROWS, _PEER_ROWS), _PEER_ROWS)]
        return pltpu.make_async_copy(idx_hbm.at[step], dst, idx_sem.at[slot])

    def issue_rows(slot):
        def body(t, carry):
            base = slot * _PEER_ROWS + t * PEER_PICKS
            for k in range(PEER_PICKS):
                off = pl.multiple_of(idx_smem[base + k], D_MODEL)
                copy = pltpu.make_async_copy(uv_hbm.at[pl.ds(off, D_MODEL)], buf.at[slot, t, k], row_sem.at[slot])
                copy.start(priority=k % _DMA_QUEUES)
            return carry
        lax.fori_loop(0, _PEER_T, body, 0)

    def wait_rows(slot):
        pltpu.make_async_copy(buf.at[slot], buf.at[slot], row_sem.at[slot]).wait()

    slot = i % 2

    @pl.when(i == 0)
    def _():
        idx_copy(0, 0).start()
        idx_copy(0, 0).wait()
        issue_rows(0)

    @pl.when(i + 1 < n)
    def _():
        idx_copy(i + 1, 1 - slot).start()
        idx_copy(i + 1, 1 - slot).wait()
        issue_rows(1 - slot)

    wait_rows(slot)

    half = h_ref.shape[1] // 2
    h = h_ref[...]
    hh = jnp.concatenate([h[:, :half], h[:, half:]], axis=0).astype(BF16)
    tok = lax.broadcasted_iota(I32, (_PEER_T, 2 * PEER_PICKS), 0)
    a = jnp.zeros((_PEER_T, 2 * PEER_PICKS), F32)
    for t in range(_PEER_T):
        urows = pltpu.bitcast(buf[slot, t, :, :half], BF16)
        r = lax.dot_general(hh, urows, (((1,), (1,)), ((), ())), preferred_element_type=F32)
        r = r[:_PEER_T] + pltpu.roll(r[_PEER_T:], 2 * PEER_PICKS - 1, axis=1)
        a = jnp.where(tok == t, r, a)
    w_even = gate_ref[...] * _gelu_exact(a)
    w = jnp.concatenate([w_even, pltpu.roll(w_even, 1, axis=1)], axis=0).astype(BF16)
    y = x_ref[...]
    tok = lax.broadcasted_iota(I32, (_PEER_T, half), 0)
    lo = jnp.zeros((_PEER_T, half), F32)
    hi = jnp.zeros((_PEER_T, half), F32)
    for t in range(_PEER_T):
        vrows = pltpu.bitcast(buf[slot, t, :, half:], BF16)
        o = jnp.dot(w, vrows, preferred_element_type=F32)
        lo = jnp.where(tok == t, o[:_PEER_T], lo)
        hi = jnp.where(tok == t, o[_PEER_T:], hi)
    y = y + jnp.concatenate([lo, hi], axis=1)
    if final_norm:
        ms = jnp.mean(y * y, axis=-1, keepdims=True)
        y = y * lax.rsqrt(ms + EPS) * gf_ref[...]
    o_ref[...] = y


def _peer_experts(eidx, gate, h, x, uv, g_final, *, final_norm):
    n, d = x.shape
    steps = n // _PEER_T
    row = lambda width: pl.BlockSpec((_PEER_T, width), lambda i: (i, 0))
    gate2 = jnp.stack([gate, jnp.zeros_like(gate)], axis=-1).reshape(n, 2 * PEER_PICKS)
    return pl.pallas_call(
        functools.partial(_peer_expert_kernel, final_norm=final_norm),
        grid=(steps,),
        in_specs=[pl.BlockSpec(memory_space=pl.ANY), row(2 * PEER_PICKS), row(d), row(d),
                  pl.BlockSpec((1, d), lambda i: (0, 0)),
                  pl.BlockSpec(memory_space=pl.ANY)],
        out_specs=row(d),
        out_shape=jax.ShapeDtypeStruct((n, d), F32),
        scratch_shapes=[pltpu.SMEM((2 * _PEER_ROWS,), I32),
                        pltpu.VMEM((2, _PEER_T, PEER_PICKS, d), I32),
                        pltpu.SemaphoreType.DMA((2,)),
                        pltpu.SemaphoreType.DMA((2,))],
        compiler_params=_cparams(("arbitrary",)),
        name="peer_experts",
    )((eidx * d).reshape(steps, _PEER_ROWS), gate2, h, x, g_final.reshape(1, d), uv.reshape(-1))


def _encoder(x, groups, rel_bias, g_attn, w_in, a_sink, g_grp_a, nat_rpb, g_grp_b, lam_q1, lam_k1, lam_q2, lam_k2,
             g_subln, w_out, g_ffn, peer_wq, peer_keys, peer_u, peer_v, g_final):
    depth = w_in.shape[0]
    splits = np.cumsum((0,) + IN_SIZES)
    perm = np.concatenate([np.arange(splits[s], splits[s + 1]) for s in _PROJ_ORDER])
    bias_a = _bias_a_table(rel_bias[:, :A_Q_HEADS])
    tiles_c, far_c = _bias_c_tables(rel_bias[:, A_Q_HEADS:])
    n = x.shape[0]
    for l in range(depth):
        lam_init = 0.8 - 0.6 * math.exp(-0.3 * l)
        w_l = w_in[l][:, perm].astype(BF16)
        proj = _norm_proj(x, g_attn[l], w_l)
        bias_b = _bias_b_table(nat_rpb[l])
        ya, yb, yc = [], [], []
        for (row0, n_seq, seq) in groups:
            kw = dict(row0=row0, n_seq=n_seq, seq=seq)
            ya.append(_mixer_a(proj, bias_a, a_sink[l], g_grp_a[l], **kw))
            yb.append(_mixer_b(proj, bias_b, g_grp_b[l], **kw))
            yc.append(_mixer_c(proj, tiles_c, far_c, lam_q1[l], lam_k1[l], lam_q2[l], lam_k2[l], g_subln[l],
                               lam_init=lam_init, **kw))
        cat = lambda parts: parts[0] if len(parts) == 1 else jnp.concatenate(parts, axis=0)
        x1, h2 = _out_proj(x, cat(ya), cat(yb), cat(yc), w_out[l].astype(BF16), g_ffn[l])
        keys = peer_keys[l].reshape(2 * PEER_HEADS, PEER_KEYS, PEER_HALF).astype(BF16)
        eidx, gate = _peer_route(h2, peer_wq[l].astype(BF16), keys)
        eidx = eidx.reshape(PEER_PICKS, n).T
        gate = gate.reshape(PEER_PICKS, n).T
        uv = _pack_expert_tables(peer_u[l], peer_v[l])
        x = _peer_experts(eidx, gate, h2, x1, uv, g_final, final_norm=(l == depth - 1))
    return x


def kernel(x_prompt, x_sample, rel_bias, g_attn, w_in, a_sink, g_grp_a, nat_rpb, g_grp_b, lam_q1, lam_k1, lam_q2, lam_k2, g_subln, w_out, g_ffn, peer_wq, peer_keys, peer_u, peer_v, g_final):
    bp, sp, d = x_prompt.shape
    bs, ss, _ = x_sample.shape
    x = jnp.concatenate([x_prompt.reshape(bp * sp, d), x_sample.reshape(bs * ss, d)], axis=0)
    groups = ((0, bp, sp), (bp * sp, bs, ss))
    y = _encoder(x, groups, rel_bias, g_attn, w_in, a_sink, g_grp_a, nat_rpb, g_grp_b, lam_q1, lam_k1, lam_q2, lam_k2,
                 g_subln, w_out, g_ffn, peer_wq, peer_keys, peer_u, peer_v, g_final)
    return (y[:bp * sp].reshape(bp, sp, d), y[bp * sp:].reshape(bs, ss, d))
```

```python
import functools
import math

import numpy as np
import jax
import jax.numpy as jnp
from jax import lax
from jax.experimental import pallas as pl
from jax.experimental.pallas import tpu as pltpu

F32 = jnp.float32
BF16 = jnp.bfloat16
I32 = jnp.int32
EPS = 1e-6
NEG = -1e30

D_MODEL = 2048
DEPTH = 2
HEAD_DIM = 128
A_Q_HEADS = 6
A_KV_HEADS = 2
A_GROUP = A_Q_HEADS // A_KV_HEADS
WINDOW = 128
BLOCK = 128
B_HEADS = 6
GRID_W = 64
NB_KH = 8
NB_KW = 16
C_HEADS = 4
C_QK_DIM = 64
C_V_DIM = 128
A_WIDTH = A_Q_HEADS * HEAD_DIM
B_WIDTH = B_HEADS * HEAD_DIM
C_WIDTH = C_HEADS * C_V_DIM
A_KV = A_KV_HEADS * HEAD_DIM
C_QK = C_HEADS * 2 * C_QK_DIM
IN_SIZES = (A_WIDTH, A_KV, A_KV, B_WIDTH, B_WIDTH, B_WIDTH, C_QK, C_QK, C_WIDTH)
IN_WIDTH = sum(IN_SIZES)
REL_BUCKETS = 32
REL_MAX_DIST = 128
PEER_HEADS = 8
PEER_KEYS = 128
PEER_TOPK = 16
PEER_QDIM = 256
PEER_HALF = PEER_QDIM // 2
PEER_PICKS = PEER_HEADS * PEER_TOPK

_PROJ_ORDER = (0, 3, 4, 5, 1, 2, 6, 7, 8)
_COL_AQ, _COL_BQ, _COL_BK, _COL_BV = 0, 1, 2, 3
_COL_AK, _COL_AV = 12, 13
_COL_CQ, _COL_CK, _COL_CV = 7, 8, 9

_VMEM_LIMIT = 56 * 1024 * 1024


def _cparams(sem):
    return pltpu.CompilerParams(dimension_semantics=sem, vmem_limit_bytes=_VMEM_LIMIT)


def _t5_bucket(rel):
    nb = REL_BUCKETS // 2
    max_exact = nb // 2
    ret = jnp.where(rel > 0, nb, 0)
    n = jnp.abs(rel)
    nf = jnp.maximum(n, 1).astype(F32)
    large = max_exact + (jnp.log(nf / max_exact) / math.log(REL_MAX_DIST / max_exact) * (nb - max_exact)).astype(I32)
    large = jnp.minimum(large, nb - 1)
    return ret + jnp.where(n < max_exact, n, large)


def _lookup(table, idx):
    onehot = jax.nn.one_hot(idx, table.shape[0], dtype=F32)
    return jnp.einsum('...b,bc->...c', onehot, table.astype(F32), precision=lax.Precision.HIGHEST)


def _bias_a_table(rel_bias_a):
    rel = np.arange(3 * BLOCK)[None, :] - BLOCK - np.arange(BLOCK)[:, None]
    bias = _lookup(rel_bias_a, _t5_bucket(jnp.asarray(rel)))
    bias = jnp.where(jnp.asarray(np.abs(rel) <= WINDOW)[:, :, None], bias, NEG)
    return bias.transpose(2, 0, 1)


def _bias_b_table(rpb):
    classes = (((0, 1), (0, 0)), ((2, 3), (0, 0)), ((4, 5), (0, 1)), ((5, 6), (1, 1)), ((7, 8), (1, 1)))
    qc = np.arange(GRID_W)[:, None]
    kc = np.arange(GRID_W)[None, :]
    c0 = np.clip(qc - NB_KW // 2, 0, GRID_W - NB_KW)
    in_cols = (kc >= c0) & (kc < c0 + NB_KW)
    n_dc = 2 * NB_KW - 1
    dc_onehot = ((kc - qc + NB_KW - 1)[:, :, None] == np.arange(n_dc)).astype(np.float32)
    blocks = jnp.einsum('qkc,hdc->hdqk', jnp.asarray(dc_onehot), rpb.astype(F32), precision=lax.Precision.HIGHEST)
    blocks = jnp.where(jnp.asarray(in_cols), blocks, NEG)
    neg = jnp.full((rpb.shape[0], GRID_W, GRID_W), NEG, F32)
    tabs = []
    for (rq, r0) in classes:
        rows = []
        for a in range(_B_QROWS):
            parts = [blocks[:, kr - rq[a] + NB_KH - 1] if r0[a] <= kr < r0[a] + NB_KH else neg
                     for kr in range(_B_KROWS)]
            rows.append(jnp.concatenate(parts, axis=-1))
        tabs.append(jnp.concatenate(rows, axis=1))
    return jnp.stack(tabs)


def _bias_c_tables(rel_bias_c):
    q = np.arange(BLOCK)[:, None]
    k = np.arange(BLOCK)[None, :]
    far = _lookup(rel_bias_c, _t5_bucket(jnp.asarray([-2 * BLOCK, 2 * BLOCK])))
    tiles = []
    for d in (-1, 0, 1):
        rel = k + d * BLOCK - q
        t = _lookup(rel_bias_c, _t5_bucket(jnp.asarray(rel))).transpose(2, 0, 1)
        base = far[0] if d < 0 else far[1]
        tiles.append(t - base[:, None, None])
    return jnp.stack(tiles, axis=1), far.T


def _norm_proj_kernel(x_ref, g_ref, w_ref, o_ref, h_ref):
    @pl.when(pl.program_id(1) == 0)
    def _():
        x = x_ref[...]
        ms = jnp.mean(x * x, axis=-1, keepdims=True)
        h_ref[...] = (x * lax.rsqrt(ms + EPS) * g_ref[...]).astype(BF16)

    o_ref[...] = jnp.dot(h_ref[...], w_ref[...], preferred_element_type=F32).astype(o_ref.dtype)


def _norm_proj(x, g, w, *, tm=512, tn=1280):
    n, d = x.shape
    width = w.shape[1]
    return pl.pallas_call(
        _norm_proj_kernel,
        grid=(n // tm, width // tn),
        in_specs=[
            pl.BlockSpec((tm, d), lambda i, j: (i, 0)),
            pl.BlockSpec((1, d), lambda i, j: (0, 0)),
            pl.BlockSpec((d, tn), lambda i, j: (0, j)),
        ],
        out_specs=pl.BlockSpec((tm, tn), lambda i, j: (i, j)),
        out_shape=jax.ShapeDtypeStruct((n, width), BF16),
        scratch_shapes=[pltpu.VMEM((tm, d), BF16)],
        compiler_params=_cparams(("parallel", "arbitrary")),
        name="norm_proj",
    )(x, g.reshape(1, d), w)


def _group_norm_store(o_ref, outs, g_ref):
    o = jnp.concatenate(outs, axis=-1)
    ms = jnp.mean(o * o, axis=-1, keepdims=True)
    o_ref[...] = (o * lax.rsqrt(ms + EPS) * g_ref[...]).astype(o_ref.dtype)


def _mixer_a_kernel(q_ref, kp_ref, kc_ref, kn_ref, vp_ref, vc_ref, vn_ref, bias_ref, sink_ref, g_ref, o_ref):
    i = pl.program_id(1)
    nb = pl.num_programs(1)
    scale = HEAD_DIM ** -0.5
    col = lax.broadcasted_iota(I32, (BLOCK, 3 * BLOCK), 1)
    lo = jnp.where(i == 0, BLOCK, 0)
    hi = jnp.where(i == nb - 1, 2 * BLOCK, 3 * BLOCK)
    in_seq = (col >= lo) & (col < hi)
    outs = []
    for kv in range(A_KV_HEADS):
        sl = slice(kv * HEAD_DIM, (kv + 1) * HEAD_DIM)
        k = jnp.concatenate([kp_ref[:, sl], kc_ref[:, sl], kn_ref[:, sl]], axis=0)
        v = jnp.concatenate([vp_ref[:, sl], vc_ref[:, sl], vn_ref[:, sl]], axis=0)
        for gq in range(A_GROUP):
            h = kv * A_GROUP + gq
            q = q_ref[:, h * HEAD_DIM:(h + 1) * HEAD_DIM]
            s = lax.dot_general(q, k, (((1,), (1,)), ((), ())), preferred_element_type=F32) * scale
            s = jnp.where(in_seq, s + bias_ref[h], NEG)
            sk = sink_ref[h]
            m = jnp.maximum(jnp.max(s, axis=-1, keepdims=True), sk)
            p = jnp.exp(s - m)
            denom = jnp.sum(p, axis=-1, keepdims=True) + jnp.exp(sk - m)
            p = (p / denom).astype(BF16)
            outs.append(jnp.dot(p, v, preferred_element_type=F32))
    _group_norm_store(o_ref, outs, g_ref)


def _mixer_a(proj, bias, sink, g, *, row0, n_seq, seq):
    nb = seq // BLOCK
    b0 = row0 // BLOCK

    def blk(shift, colblk):
        return lambda b, i: (b0 + b * nb + jnp.clip(i + shift, 0, nb - 1), colblk)

    kv_spec = lambda shift, colblk: pl.BlockSpec((BLOCK, A_KV), blk(shift, colblk))
    return pl.pallas_call(
        _mixer_a_kernel,
        grid=(n_seq, nb),
        in_specs=[
            pl.BlockSpec((BLOCK, A_WIDTH), blk(0, _COL_AQ)),
            kv_spec(-1, _COL_AK), kv_spec(0, _COL_AK), kv_spec(1, _COL_AK),
            kv_spec(-1, _COL_AV), kv_spec(0, _COL_AV), kv_spec(1, _COL_AV),
            pl.BlockSpec((A_Q_HEADS, BLOCK, 3 * BLOCK), lambda b, i: (0, 0, 0)),
            pl.BlockSpec(memory_space=pltpu.SMEM),
            pl.BlockSpec((1, A_WIDTH), lambda b, i: (0, 0)),
        ],
        out_specs=pl.BlockSpec((BLOCK, A_WIDTH), lambda b, i: (b * nb + i, 0)),
        out_shape=jax.ShapeDtypeStruct((n_seq * seq, A_WIDTH), BF16),
        compiler_params=_cparams(("parallel", "parallel")),
        name="mixer_a",
    )(proj, proj, proj, proj, proj, proj, proj, bias, sink, g.reshape(1, A_WIDTH))


_B_QROWS = 2
_B_KROWS = NB_KH + _B_QROWS - 1


def _mixer_b_kernel(q_ref, k_ref, v_ref, bias_ref, g_ref, o_ref, *, rows):
    i = pl.program_id(1)
    scale = HEAD_DIM ** -0.5
    kstart = jnp.clip(_B_QROWS * i - NB_KH // 2, 0, rows - _B_KROWS)
    ks = pl.multiple_of(kstart * GRID_W, GRID_W)
    nk = _B_KROWS * GRID_W
    outs = []
    for h in range(B_HEADS):
        sl = slice(h * HEAD_DIM, (h + 1) * HEAD_DIM)
        q = q_ref[:, sl]
        k = k_ref[pl.ds(ks, nk), sl]
        v = v_ref[pl.ds(ks, nk), sl]
        s = lax.dot_general(q, k, (((1,), (1,)), ((), ())), preferred_element_type=F32) * scale + bias_ref[0, h]
        m = jnp.max(s, axis=-1, keepdims=True)
        p = jnp.exp(s - m)
        p = (p / jnp.sum(p, axis=-1, keepdims=True)).astype(BF16)
        outs.append(jnp.dot(p, v, preferred_element_type=F32))
    _group_norm_store(o_ref, outs, g_ref)


def _mixer_b(proj, bias, g, *, row0, n_seq, seq):
    rows = seq // GRID_W
    nrp = rows // _B_QROWS
    tq = _B_QROWS * GRID_W
    b0 = row0 // tq
    s0 = row0 // seq

    def cls(i):
        return jnp.where(i == 0, 0, jnp.where(i == 1, 1, jnp.where(i == nrp - 2, 3, jnp.where(i == nrp - 1, 4, 2))))

    return pl.pallas_call(
        functools.partial(_mixer_b_kernel, rows=rows),
        grid=(n_seq, nrp),
        in_specs=[
            pl.BlockSpec((tq, B_WIDTH), lambda b, i: (b0 + b * nrp + i, _COL_BQ)),
            pl.BlockSpec((seq, B_WIDTH), lambda b, i: (s0 + b, _COL_BK)),
            pl.BlockSpec((seq, B_WIDTH), lambda b, i: (s0 + b, _COL_BV)),
            pl.BlockSpec((1, B_HEADS, tq, _B_KROWS * GRID_W), lambda b, i: (cls(i), 0, 0, 0)),
            pl.BlockSpec((1, B_WIDTH), lambda b, i: (0, 0)),
        ],
        out_specs=pl.BlockSpec((tq, B_WIDTH), lambda b, i: (b * nrp + i, 0)),
        out_shape=jax.ShapeDtypeStruct((n_seq * seq, B_WIDTH), BF16),
        compiler_params=_cparams(("parallel", "arbitrary")),
        name="mixer_b",
    )(proj, proj, proj, bias, g.reshape(1, B_WIDTH))


def _mixer_c_kernel(q_ref, k_ref, v_ref, tiles_ref, far_ref, lq1_ref, lk1_ref, lq2_ref, lk2_ref, g_ref, o_ref,
                    s_ref, *, seq, lam_init):
    i = pl.program_id(1)
    nb = seq // BLOCK
    scale = C_QK_DIM ** -0.5
    lam = (jnp.exp(jnp.sum(lq1_ref[...] * lk1_ref[...], axis=-1, keepdims=True))
           - jnp.exp(jnp.sum(lq2_ref[...] * lk2_ref[...], axis=-1, keepdims=True)) + lam_init)
    before = lax.broadcasted_iota(I32, (1, seq), 1) < i * BLOCK
    lane = lax.broadcasted_iota(I32, (BLOCK, HEAD_DIM), 1)
    outs = []
    for h in range(C_HEADS):
        sl = slice(h * HEAD_DIM, (h + 1) * HEAD_DIM)
        qh = q_ref[:, sl] * scale
        kh = k_ref[:, sl]
        exps, sums = [], []
        for c in range(2):
            hc = 2 * h + c
            qc = jnp.where((lane >= c * C_QK_DIM) & (lane < (c + 1) * C_QK_DIM), qh, jnp.zeros_like(qh))
            s = lax.dot_general(qc, kh, (((1,), (1,)), ((), ())), preferred_element_type=F32)
            s_ref[c] = s + jnp.where(before, far_ref[hc, 0], far_ref[hc, 1])
            for d in range(3):
                j = i + (d - 1)
                inside = jnp.where((j >= 0) & (j < nb), 1.0, 0.0)
                cols = pl.ds(pl.multiple_of(jnp.clip(j, 0, nb - 1) * BLOCK, BLOCK), BLOCK)
                s_ref[c, :, cols] += tiles_ref[hc, d] * inside
            s = s_ref[c]
            e = jnp.exp(s - jnp.max(s, axis=-1, keepdims=True))
            exps.append(e)
            sums.append(jnp.sum(e, axis=-1, keepdims=True))
        a = (exps[0] - (lam * sums[0] / sums[1]) * exps[1]).astype(BF16)
        o = jnp.dot(a, v_ref[:, sl], preferred_element_type=F32) / sums[0]
        ms = jnp.mean(o * o, axis=-1, keepdims=True)
        outs.append(o * lax.rsqrt(ms + EPS) * g_ref[...] * (1.0 - lam_init))
    o_ref[...] = jnp.concatenate(outs, axis=-1).astype(o_ref.dtype)


def _mixer_c(proj, tiles, far, lq1, lk1, lq2, lk2, g, *, row0, n_seq, seq, lam_init):
    nb = seq // BLOCK
    b0 = row0 // BLOCK
    s0 = row0 // seq
    vec = lambda a: a.reshape(1, C_QK_DIM)
    vspec = pl.BlockSpec((1, C_QK_DIM), lambda b, i: (0, 0))
    return pl.pallas_call(
        functools.partial(_mixer_c_kernel, seq=seq, lam_init=lam_init),
        grid=(n_seq, nb),
        in_specs=[
            pl.BlockSpec((BLOCK, C_QK), lambda b, i: (b0 + b * nb + i, _COL_CQ)),
            pl.BlockSpec((seq, C_QK), lambda b, i: (s0 + b, _COL_CK)),
            pl.BlockSpec((seq, C_WIDTH), lambda b, i: (s0 + b, _COL_CV)),
            pl.BlockSpec((2 * C_HEADS, 3, BLOCK, BLOCK), lambda b, i: (0, 0, 0, 0)),
            pl.BlockSpec(memory_space=pltpu.SMEM),
            vspec, vspec, vspec, vspec,
            pl.BlockSpec((1, C_V_DIM), lambda b, i: (0, 0)),
        ],
        out_specs=pl.BlockSpec((BLOCK, C_WIDTH), lambda b, i: (b * nb + i, 0)),
        out_shape=jax.ShapeDtypeStruct((n_seq * seq, C_WIDTH), BF16),
        scratch_shapes=[pltpu.VMEM((2, BLOCK, seq), F32)],
        compiler_params=_cparams(("parallel", "arbitrary")),
        name="mixer_c",
    )(proj, proj, proj, tiles, far, vec(lq1), vec(lk1), vec(lq2), vec(lk2), g.reshape(1, C_V_DIM))


def _out_proj_kernel(x_ref, ya_ref, yb_ref, yc_ref, w_ref, g_ref, x1_ref, h_ref):
    acc = jnp.dot(ya_ref[...], w_ref[0:A_WIDTH, :], preferred_element_type=F32)
    acc += jnp.dot(yb_ref[...], w_ref[A_WIDTH:A_WIDTH + B_WIDTH, :], preferred_element_type=F32)
    acc += jnp.dot(yc_ref[...], w_ref[A_WIDTH + B_WIDTH:, :], preferred_element_type=F32)
    x1 = x_ref[...] + acc
    x1_ref[...] = x1
    ms = jnp.mean(x1 * x1, axis=-1, keepdims=True)
    h_ref[...] = x1 * lax.rsqrt(ms + EPS) * g_ref[...]


def _out_proj(x, ya, yb, yc, w, g, *, tm=256):
    n, d = x.shape
    row = lambda width: pl.BlockSpec((tm, width), lambda i: (i, 0))
    return pl.pallas_call(
        _out_proj_kernel,
        grid=(n // tm,),
        in_specs=[row(d), row(A_WIDTH), row(B_WIDTH), row(C_WIDTH),
                  pl.BlockSpec((d, d), lambda i: (0, 0)),
                  pl.BlockSpec((1, d), lambda i: (0, 0))],
        out_specs=[row(d), row(d)],
        out_shape=[jax.ShapeDtypeStruct((n, d), F32), jax.ShapeDtypeStruct((n, d), F32)],
        compiler_params=_cparams(("parallel",)),
        name="out_proj",
    )(x, ya, yb, yc, w, g.reshape(1, d))


def _top16(x):
    r = x.shape[0]
    iota = lax.broadcasted_iota(I32, x.shape, 0)
    vals, idxs = [], []
    for _ in range(PEER_TOPK):
        m = jnp.max(x, axis=0, keepdims=True)
        idx = jnp.min(jnp.where(x == m, iota, r), axis=0, keepdims=True)
        vals.append(m)
        idxs.append(idx)
        x = jnp.where(iota == idx, -jnp.inf, x)
    return vals, idxs


def _peer_candidates(s1, i1, s2, i2):
    sub = 8
    s1a, i1a = jnp.concatenate(s1, axis=0), jnp.concatenate(i1, axis=0)
    s2a, i2a = jnp.concatenate(s2, axis=0), jnp.concatenate(i2, axis=0)
    row = lax.broadcasted_iota(I32, s2a[:sub].shape, 0)
    cand, cidx = [], []
    first_single = PEER_TOPK // 2
    for a in range(first_single):
        nb = PEER_TOPK // (a + 1)
        if nb > sub:
            cand.append(s1[a] + s2a)
            cidx.append(i1[a] * PEER_KEYS + i2a)
        else:
            cand.append(jnp.where(row < nb, s1[a] + s2a[:sub], -jnp.inf))
            cidx.append(i1[a] * PEER_KEYS + i2a[:sub])
    cand.append(s1a[first_single:] + s2[0])
    cidx.append(i1a[first_single:] * PEER_KEYS + i2[0])
    return jnp.concatenate(cand, axis=0), jnp.concatenate(cidx, axis=0)


def _peer_route_kernel(h_ref, wq_ref, keys_ref, eidx_ref, gate_ref):
    q = jnp.dot(h_ref[...].astype(BF16), wq_ref[...], preferred_element_type=F32).astype(BF16)
    tm = q.shape[0]
    for hp in range(PEER_HEADS):
        tops = []
        for c in range(2):
            hc = 2 * hp + c
            qc = q[:, hc * PEER_HALF:(hc + 1) * PEER_HALF]
            sc = lax.dot_general(keys_ref[hc], qc, (((1,), (1,)), ((), ())), preferred_element_type=F32)
            tops.append(_top16(sc))
        (s1, i1), (s2, i2) = tops
        cand, cidx = _peer_candidates(s1, i1, s2, i2)
        top_s, pos = _top16(cand)
        iota = lax.broadcasted_iota(I32, cand.shape, 0)
        eidx = [jnp.sum(jnp.where(iota == p, cidx, 0), axis=0, keepdims=True) for p in pos]
        ts = jnp.concatenate(top_s, axis=0)
        e = jnp.exp(ts - top_s[0])
        gate_ref[hp] = e / jnp.sum(e, axis=0, keepdims=True)
        eidx_ref[hp] = jnp.concatenate(eidx, axis=0)


def _peer_route(h, wq, keys, *, tm=256):
    n, d = h.shape
    out_spec = pl.BlockSpec((PEER_HEADS, PEER_TOPK, tm), lambda i: (0, 0, i))
    return pl.pallas_call(
        _peer_route_kernel,
        grid=(n // tm,),
        in_specs=[pl.BlockSpec((tm, d), lambda i: (i, 0)),
                  pl.BlockSpec((d, PEER_HEADS * PEER_QDIM), lambda i: (0, 0)),
                  pl.BlockSpec((2 * PEER_HEADS, PEER_KEYS, PEER_HALF), lambda i: (0, 0, 0))],
        out_specs=[out_spec, out_spec],
        out_shape=[jax.ShapeDtypeStruct((PEER_HEADS, PEER_TOPK, n), I32),
                   jax.ShapeDtypeStruct((PEER_HEADS, PEER_TOPK, n), F32)],
        compiler_params=_cparams(("parallel",)),
        name="peer_route",
    )(h, wq, keys)


def _gelu_exact(x):
    return 0.5 * x * (1.0 + lax.erf(x * (2.0 ** -0.5)))


_PEER_T = 8
_DMA_QUEUES = 2
_PEER_ROWS = _PEER_T * PEER_PICKS


def _pack_expert_tables(u, v):
    half = u.shape[1] // 2
    bits = lambda a: lax.bitcast_convert_type(a.astype(BF16), jnp.uint16).astype(jnp.uint32)
    lo = jnp.concatenate([bits(u[:, :half]), bits(v[:, :half])], axis=1)
    hi = jnp.concatenate([bits(u[:, half:]), bits(v[:, half:])], axis=1)
    return lax.bitcast_convert_type(lo | (hi << 16), I32)


def _peer_expert_kernel(idx_hbm, gate_ref, h_ref, x_ref, gf_ref, uv_hbm, o_ref,
                        idx_smem, buf, idx_sem, row_sem, *, final_norm):
    i = pl.program_id(0)
    n = pl.num_programs(0)

    def idx_copy(step, slot):
        dst = idx_smem.at[pl.ds(pl.multiple_of(slot * _PEER_ROWS, _PEER_ROWS), _PEER_ROWS)]
        return pltpu.make_async_copy(idx_hbm.at[step], dst, idx_sem.at[slot])

    def issue_rows(slot):
        def body(t, carry):
            base = slot * _PEER_ROWS + t * PEER_PICKS
            for k in range(PEER_PICKS):
                off = pl.multiple_of(idx_smem[base + k], D_MODEL)
                copy = pltpu.make_async_copy(uv_hbm.at[pl.ds(off, D_MODEL)], buf.at[slot, t, k], row_sem.at[slot])
                copy.start(priority=k % _DMA_QUEUES)
            return carry
        lax.fori_loop(0, _PEER_T, body, 0)

    def wait_rows(slot):
        pltpu.make_async_copy(buf.at[slot], buf.at[slot], row_sem.at[slot]).wait()

    slot = i % 2

    @pl.when(i == 0)
    def _():
        idx_copy(0, 0).start()
        idx_copy(0, 0).wait()
        issue_rows(0)
        idx_copy(1, 1).start()

    idx_copy(i + 1, 1 - slot).wait()
    idx_copy(i + 2, slot).start()

    @pl.when(i + 1 < n)
    def _():
        issue_rows(1 - slot)

    wait_rows(slot)

    @pl.when(i == n - 1)
    def _():
        idx_copy(i + 2, slot).wait()

    half = h_ref.shape[1] // 2
    h = h_ref[...]
    hh = jnp.concatenate([h[:, :half], h[:, half:]], axis=0).astype(BF16)
    tok = lax.broadcasted_iota(I32, (_PEER_T, 2 * PEER_PICKS), 0)
    a = jnp.zeros((_PEER_T, 2 * PEER_PICKS), F32)
    for t in range(_PEER_T):
        urows = pltpu.bitcast(buf[slot, t, :, :half], BF16)
        r = lax.dot_general(hh, urows, (((1,), (1,)), ((), ())), preferred_element_type=F32)
        r = r[:_PEER_T] + pltpu.roll(r[_PEER_T:], 2 * PEER_PICKS - 1, axis=1)
        a = jnp.where(tok == t, r, a)
    w_even = gate_ref[...] * _gelu_exact(a)
    w = jnp.concatenate([w_even, pltpu.roll(w_even, 1, axis=1)], axis=0).astype(BF16)
    y = x_ref[...]
    tok = lax.broadcasted_iota(I32, (_PEER_T, half), 0)
    lo = jnp.zeros((_PEER_T, half), F32)
    hi = jnp.zeros((_PEER_T, half), F32)
    for t in range(_PEER_T):
        vrows = pltpu.bitcast(buf[slot, t, :, half:], BF16)
        o = jnp.dot(w, vrows, preferred_element_type=F32)
        lo = jnp.where(tok == t, o[:_PEER_T], lo)
        hi = jnp.where(tok == t, o[_PEER_T:], hi)
    y = y + jnp.concatenate([lo, hi], axis=1)
    if final_norm:
        ms = jnp.mean(y * y, axis=-1, keepdims=True)
        y = y * lax.rsqrt(ms + EPS) * gf_ref[...]
    o_ref[...] = y


def _peer_experts(eidx, gate, h, x, uv, g_final, *, final_norm):
    n, d = x.shape
    steps = n // _PEER_T
    row = lambda width: pl.BlockSpec((_PEER_T, width), lambda i: (i, 0))
    gate2 = jnp.stack([gate, jnp.zeros_like(gate)], axis=-1).reshape(n, 2 * PEER_PICKS)
    return pl.pallas_call(
        functools.partial(_peer_expert_kernel, final_norm=final_norm),
        grid=(steps,),
        in_specs=[pl.BlockSpec(memory_space=pl.ANY), row(2 * PEER_PICKS), row(d), row(d),
                  pl.BlockSpec((1, d), lambda i: (0, 0)),
                  pl.BlockSpec(memory_space=pl.ANY)],
        out_specs=row(d),
        out_shape=jax.ShapeDtypeStruct((n, d), F32),
        scratch_shapes=[pltpu.SMEM((2 * _PEER_ROWS,), I32),
                        pltpu.VMEM((2, _PEER_T, PEER_PICKS, d), I32),
                        pltpu.SemaphoreType.DMA((2,)),
                        pltpu.SemaphoreType.DMA((2,))],
        compiler_params=_cparams(("arbitrary",)),
        name="peer_experts",
    )(jnp.pad((eidx * d).reshape(steps, _PEER_ROWS), ((0, 2), (0, 0))), gate2, h, x, g_final.reshape(1, d),
      uv.reshape(-1))


def _encoder(x, groups, rel_bias, g_attn, w_in, a_sink, g_grp_a, nat_rpb, g_grp_b, lam_q1, lam_k1, lam_q2, lam_k2,
             g_subln, w_out, g_ffn, peer_wq, peer_keys, peer_u, peer_v, g_final):
    depth = w_in.shape[0]
    splits = np.cumsum((0,) + IN_SIZES)
    perm = np.concatenate([np.arange(splits[s], splits[s + 1]) for s in _PROJ_ORDER])
    bias_a = _bias_a_table(rel_bias[:, :A_Q_HEADS])
    tiles_c, far_c = _bias_c_tables(rel_bias[:, A_Q_HEADS:])
    n = x.shape[0]
    for l in range(depth):
        lam_init = 0.8 - 0.6 * math.exp(-0.3 * l)
        w_l = w_in[l][:, perm].astype(BF16)
        proj = _norm_proj(x, g_attn[l], w_l)
        bias_b = _bias_b_table(nat_rpb[l])
        ya, yb, yc = [], [], []
        for (row0, n_seq, seq) in groups:
            kw = dict(row0=row0, n_seq=n_seq, seq=seq)
            ya.append(_mixer_a(proj, bias_a, a_sink[l], g_grp_a[l], **kw))
            yb.append(_mixer_b(proj, bias_b, g_grp_b[l], **kw))
            yc.append(_mixer_c(proj, tiles_c, far_c, lam_q1[l], lam_k1[l], lam_q2[l], lam_k2[l], g_subln[l],
                               lam_init=lam_init, **kw))
        cat = lambda parts: parts[0] if len(parts) == 1 else jnp.concatenate(parts, axis=0)
        x1, h2 = _out_proj(x, cat(ya), cat(yb), cat(yc), w_out[l].astype(BF16), g_ffn[l])
        keys = peer_keys[l].reshape(2 * PEER_HEADS, PEER_KEYS, PEER_HALF).astype(BF16)
        eidx, gate = _peer_route(h2, peer_wq[l].astype(BF16), keys)
        eidx = eidx.reshape(PEER_PICKS, n).T
        gate = gate.reshape(PEER_PICKS, n).T
        uv = _pack_expert_tables(peer_u[l], peer_v[l])
        x = _peer_experts(eidx, gate, h2, x1, uv, g_final, final_norm=(l == depth - 1))
    return x


def kernel(x_prompt, x_sample, rel_bias, g_attn, w_in, a_sink, g_grp_a, nat_rpb, g_grp_b, lam_q1, lam_k1, lam_q2, lam_k2, g_subln, w_out, g_ffn, peer_wq, peer_keys, peer_u, peer_v, g_final):
    bp, sp, d = x_prompt.shape
    bs, ss, _ = x_sample.shape
    x = jnp.concatenate([x_prompt.reshape(bp * sp, d), x_sample.reshape(bs * ss, d)], axis=0)
    groups = ((0, bp, sp), (bp * sp, bs, ss))
    y = _encoder(x, groups, rel_bias, g_attn, w_in, a_sink, g_grp_a, nat_rpb, g_grp_b, lam_q1, lam_k1, lam_q2, lam_k2,
                 g_subln, w_out, g_ffn, peer_wq, peer_keys, peer_u, peer_v, g_final)
    return (y[:bp * sp].reshape(bp, sp, d), y[bp * sp:].reshape(bs, ss, d))
```

```python
import functools
import math

import numpy as np
import jax
import jax.numpy as jnp
from jax import lax
from jax.experimental import pallas as pl
from jax.experimental.pallas import tpu as pltpu

F32 = jnp.float32
BF16 = jnp.bfloat16
I32 = jnp.int32
EPS = 1e-6
NEG = -1e30

D_MODEL = 2048
DEPTH = 2
HEAD_DIM = 128
A_Q_HEADS = 6
A_KV_HEADS = 2
A_GROUP = A_Q_HEADS // A_KV_HEADS
WINDOW = 128
BLOCK = 128
B_HEADS = 6
GRID_W = 64
NB_KH = 8
NB_KW = 16
C_HEADS = 4
C_QK_DIM = 64
C_V_DIM = 128
A_WIDTH = A_Q_HEADS * HEAD_DIM
B_WIDTH = B_HEADS * HEAD_DIM
C_WIDTH = C_HEADS * C_V_DIM
A_KV = A_KV_HEADS * HEAD_DIM
C_QK = C_HEADS * 2 * C_QK_DIM
IN_SIZES = (A_WIDTH, A_KV, A_KV, B_WIDTH, B_WIDTH, B_WIDTH, C_QK, C_QK, C_WIDTH)
IN_WIDTH = sum(IN_SIZES)
REL_BUCKETS = 32
REL_MAX_DIST = 128
PEER_HEADS = 8
PEER_KEYS = 128
PEER_TOPK = 16
PEER_QDIM = 256
PEER_HALF = PEER_QDIM // 2
PEER_PICKS = PEER_HEADS * PEER_TOPK

_PROJ_ORDER = (0, 3, 4, 5, 1, 2, 6, 7, 8)
_COL_AQ, _COL_BQ, _COL_BK, _COL_BV = 0, 1, 2, 3
_COL_AK, _COL_AV = 12, 13
_COL_CQ, _COL_CK, _COL_CV = 7, 8, 9

_VMEM_LIMIT = 56 * 1024 * 1024


def _cparams(sem):
    return pltpu.CompilerParams(dimension_semantics=sem, vmem_limit_bytes=_VMEM_LIMIT)


def _t5_bucket(rel):
    nb = REL_BUCKETS // 2
    max_exact = nb // 2
    ret = jnp.where(rel > 0, nb, 0)
    n = jnp.abs(rel)
    nf = jnp.maximum(n, 1).astype(F32)
    large = max_exact + (jnp.log(nf / max_exact) / math.log(REL_MAX_DIST / max_exact) * (nb - max_exact)).astype(I32)
    large = jnp.minimum(large, nb - 1)
    return ret + jnp.where(n < max_exact, n, large)


def _lookup(table, idx):
    onehot = jax.nn.one_hot(idx, table.shape[0], dtype=F32)
    return jnp.einsum('...b,bc->...c', onehot, table.astype(F32), precision=lax.Precision.HIGHEST)


def _bias_a_table(rel_bias_a):
    rel = np.arange(3 * BLOCK)[None, :] - BLOCK - np.arange(BLOCK)[:, None]
    bias = _lookup(rel_bias_a, _t5_bucket(jnp.asarray(rel)))
    bias = jnp.where(jnp.asarray(np.abs(rel) <= WINDOW)[:, :, None], bias, NEG)
    return bias.transpose(2, 0, 1)


def _bias_b_table(rpb):
    classes = (((0, 1), (0, 0)), ((2, 3), (0, 0)), ((4, 5), (0, 1)), ((5, 6), (1, 1)), ((7, 8), (1, 1)))
    qc = np.arange(GRID_W)[:, None]
    kc = np.arange(GRID_W)[None, :]
    c0 = np.clip(qc - NB_KW // 2, 0, GRID_W - NB_KW)
    in_cols = (kc >= c0) & (kc < c0 + NB_KW)
    n_dc = 2 * NB_KW - 1
    dc_onehot = ((kc - qc + NB_KW - 1)[:, :, None] == np.arange(n_dc)).astype(np.float32)
    blocks = jnp.einsum('qkc,hdc->hdqk', jnp.asarray(dc_onehot), rpb.astype(F32), precision=lax.Precision.HIGHEST)
    blocks = jnp.where(jnp.asarray(in_cols), blocks, NEG)
    neg = jnp.full((rpb.shape[0], GRID_W, GRID_W), NEG, F32)
    tabs = []
    for (rq, r0) in classes:
        rows = []
        for a in range(_B_QROWS):
            parts = [blocks[:, kr - rq[a] + NB_KH - 1] if r0[a] <= kr < r0[a] + NB_KH else neg
                     for kr in range(_B_KROWS)]
            rows.append(jnp.concatenate(parts, axis=-1))
        tabs.append(jnp.concatenate(rows, axis=1))
    return jnp.stack(tabs)


def _bias_c_tables(rel_bias_c):
    q = np.arange(BLOCK)[:, None]
    k = np.arange(BLOCK)[None, :]
    far = _lookup(rel_bias_c, _t5_bucket(jnp.asarray([-2 * BLOCK, 2 * BLOCK])))
    tiles = []
    for d in (-1, 0, 1):
        rel = k + d * BLOCK - q
        t = _lookup(rel_bias_c, _t5_bucket(jnp.asarray(rel))).transpose(2, 0, 1)
        base = far[0] if d < 0 else far[1]
        tiles.append(t - base[:, None, None])
    return jnp.stack(tiles, axis=1), far.T


def _norm_proj_kernel(x_ref, g_ref, w_ref, o_ref, h_ref):
    @pl.when(pl.program_id(1) == 0)
    def _():
        x = x_ref[...]
        ms = jnp.mean(x * x, axis=-1, keepdims=True)
        h_ref[...] = (x * lax.rsqrt(ms + EPS) * g_ref[...]).astype(BF16)

    o_ref[...] = jnp.dot(h_ref[...], w_ref[...], preferred_element_type=F32).astype(o_ref.dtype)


def _norm_proj(x, g, w, *, tm=512, tn=1280):
    n, d = x.shape
    width = w.shape[1]
    return pl.pallas_call(
        _norm_proj_kernel,
        grid=(n // tm, width // tn),
        in_specs=[
            pl.BlockSpec((tm, d), lambda i, j: (i, 0)),
            pl.BlockSpec((1, d), lambda i, j: (0, 0)),
            pl.BlockSpec((d, tn), lambda i, j: (0, j)),
        ],
        out_specs=pl.BlockSpec((tm, tn), lambda i, j: (i, j)),
        out_shape=jax.ShapeDtypeStruct((n, width), BF16),
        scratch_shapes=[pltpu.VMEM((tm, d), BF16)],
        compiler_params=_cparams(("parallel", "arbitrary")),
        name="norm_proj",
    )(x, g.reshape(1, d), w)


def _group_norm_store(o_ref, outs, g_ref):
    o = jnp.concatenate(outs, axis=-1)
    ms = jnp.mean(o * o, axis=-1, keepdims=True)
    o_ref[...] = (o * lax.rsqrt(ms + EPS) * g_ref[...]).astype(o_ref.dtype)


def _mixer_a_kernel(q_ref, kp_ref, kc_ref, kn_ref, vp_ref, vc_ref, vn_ref, bias_ref, sink_ref, g_ref, o_ref):
    i = pl.program_id(1)
    nb = pl.num_programs(1)
    scale = HEAD_DIM ** -0.5
    col = lax.broadcasted_iota(I32, (BLOCK, 3 * BLOCK), 1)
    lo = jnp.where(i == 0, BLOCK, 0)
    hi = jnp.where(i == nb - 1, 2 * BLOCK, 3 * BLOCK)
    in_seq = (col >= lo) & (col < hi)
    outs = []
    for kv in range(A_KV_HEADS):
        sl = slice(kv * HEAD_DIM, (kv + 1) * HEAD_DIM)
        k = jnp.concatenate([kp_ref[:, sl], kc_ref[:, sl], kn_ref[:, sl]], axis=0)
        v = jnp.concatenate([vp_ref[:, sl], vc_ref[:, sl], vn_ref[:, sl]], axis=0)
        for gq in range(A_GROUP):
            h = kv * A_GROUP + gq
            q = q_ref[:, h * HEAD_DIM:(h + 1) * HEAD_DIM]
            s = lax.dot_general(q, k, (((1,), (1,)), ((), ())), preferred_element_type=F32) * scale
            s = jnp.where(in_seq, s + bias_ref[h], NEG)
            sk = sink_ref[h]
            m = jnp.maximum(jnp.max(s, axis=-1, keepdims=True), sk)
            p = jnp.exp(s - m)
            denom = jnp.sum(p, axis=-1, keepdims=True) + jnp.exp(sk - m)
            p = (p / denom).astype(BF16)
            outs.append(jnp.dot(p, v, preferred_element_type=F32))
    _group_norm_store(o_ref, outs, g_ref)


def _mixer_a(proj, bias, sink, g, *, row0, n_seq, seq):
    nb = seq // BLOCK
    b0 = row0 // BLOCK

    def blk(shift, colblk):
        return lambda b, i: (b0 + b * nb + jnp.clip(i + shift, 0, nb - 1), colblk)

    kv_spec = lambda shift, colblk: pl.BlockSpec((BLOCK, A_KV), blk(shift, colblk))
    return pl.pallas_call(
        _mixer_a_kernel,
        grid=(n_seq, nb),
        in_specs=[
            pl.BlockSpec((BLOCK, A_WIDTH), blk(0, _COL_AQ)),
            kv_spec(-1, _COL_AK), kv_spec(0, _COL_AK), kv_spec(1, _COL_AK),
            kv_spec(-1, _COL_AV), kv_spec(0, _COL_AV), kv_spec(1, _COL_AV),
            pl.BlockSpec((A_Q_HEADS, BLOCK, 3 * BLOCK), lambda b, i: (0, 0, 0)),
            pl.BlockSpec(memory_space=pltpu.SMEM),
            pl.BlockSpec((1, A_WIDTH), lambda b, i: (0, 0)),
        ],
        out_specs=pl.BlockSpec((BLOCK, A_WIDTH), lambda b, i: (b * nb + i, 0)),
        out_shape=jax.ShapeDtypeStruct((n_seq * seq, A_WIDTH), BF16),
        compiler_params=_cparams(("parallel", "parallel")),
        name="mixer_a",
    )(proj, proj, proj, proj, proj, proj, proj, bias, sink, g.reshape(1, A_WIDTH))


_B_QROWS = 2
_B_KROWS = NB_KH + _B_QROWS - 1


def _mixer_b_kernel(q_ref, k_ref, v_ref, bias_ref, g_ref, o_ref, *, rows):
    i = pl.program_id(1)
    scale = HEAD_DIM ** -0.5
    kstart = jnp.clip(_B_QROWS * i - NB_KH // 2, 0, rows - _B_KROWS)
    ks = pl.multiple_of(kstart * GRID_W, GRID_W)
    nk = _B_KROWS * GRID_W
    outs = []
    for h in range(B_HEADS):
        sl = slice(h * HEAD_DIM, (h + 1) * HEAD_DIM)
        q = q_ref[:, sl]
        k = k_ref[pl.ds(ks, nk), sl]
        v = v_ref[pl.ds(ks, nk), sl]
        s = lax.dot_general(q, k, (((1,), (1,)), ((), ())), preferred_element_type=F32) * scale + bias_ref[0, h]
        m = jnp.max(s, axis=-1, keepdims=True)
        p = jnp.exp(s - m)
        p = (p / jnp.sum(p, axis=-1, keepdims=True)).astype(BF16)
        outs.append(jnp.dot(p, v, preferred_element_type=F32))
    _group_norm_store(o_ref, outs, g_ref)


def _mixer_b(proj, bias, g, *, row0, n_seq, seq):
    rows = seq // GRID_W
    nrp = rows // _B_QROWS
    tq = _B_QROWS * GRID_W
    b0 = row0 // tq
    s0 = row0 // seq

    def cls(i):
        return jnp.where(i == 0, 0, jnp.where(i == 1, 1, jnp.where(i == nrp - 2, 3, jnp.where(i == nrp - 1, 4, 2))))

    return pl.pallas_call(
        functools.partial(_mixer_b_kernel, rows=rows),
        grid=(n_seq, nrp),
        in_specs=[
            pl.BlockSpec((tq, B_WIDTH), lambda b, i: (b0 + b * nrp + i, _COL_BQ)),
            pl.BlockSpec((seq, B_WIDTH), lambda b, i: (s0 + b, _COL_BK)),
            pl.BlockSpec((seq, B_WIDTH), lambda b, i: (s0 + b, _COL_BV)),
            pl.BlockSpec((1, B_HEADS, tq, _B_KROWS * GRID_W), lambda b, i: (cls(i), 0, 0, 0)),
            pl.BlockSpec((1, B_WIDTH), lambda b, i: (0, 0)),
        ],
        out_specs=pl.BlockSpec((tq, B_WIDTH), lambda b, i: (b * nrp + i, 0)),
        out_shape=jax.ShapeDtypeStruct((n_seq * seq, B_WIDTH), BF16),
        compiler_params=_cparams(("parallel", "arbitrary")),
        name="mixer_b",
    )(proj, proj, proj, bias, g.reshape(1, B_WIDTH))


def _mixer_c_kernel(q_ref, k_ref, v_ref, tiles_ref, far_ref, lq1_ref, lk1_ref, lq2_ref, lk2_ref, g_ref, o_ref,
                    s_ref, *, seq, lam_init):
    i = pl.program_id(1)
    nb = seq // BLOCK
    scale = C_QK_DIM ** -0.5
    lam = (jnp.exp(jnp.sum(lq1_ref[...] * lk1_ref[...], axis=-1, keepdims=True))
           - jnp.exp(jnp.sum(lq2_ref[...] * lk2_ref[...], axis=-1, keepdims=True)) + lam_init)
    before = lax.broadcasted_iota(I32, (1, seq), 1) < i * BLOCK
    lane = lax.broadcasted_iota(I32, (BLOCK, HEAD_DIM), 1)
    outs = []
    for h in range(C_HEADS):
        sl = slice(h * HEAD_DIM, (h + 1) * HEAD_DIM)
        qh = q_ref[:, sl] * scale
        kh = k_ref[:, sl]
        exps, sums = [], []
        for c in range(2):
            hc = 2 * h + c
            qc = jnp.where((lane >= c * C_QK_DIM) & (lane < (c + 1) * C_QK_DIM), qh, jnp.zeros_like(qh))
            s = lax.dot_general(qc, kh, (((1,), (1,)), ((), ())), preferred_element_type=F32)
            s_ref[c] = s + jnp.where(before, far_ref[hc, 0], far_ref[hc, 1])
            for d in range(3):
                j = i + (d - 1)
                inside = jnp.where((j >= 0) & (j < nb), 1.0, 0.0)
                cols = pl.ds(pl.multiple_of(jnp.clip(j, 0, nb - 1) * BLOCK, BLOCK), BLOCK)
                s_ref[c, :, cols] += tiles_ref[hc, d] * inside
            s = s_ref[c]
            e = jnp.exp(s - jnp.max(s, axis=-1, keepdims=True))
            exps.append(e)
            sums.append(jnp.sum(e, axis=-1, keepdims=True))
        a = (exps[0] - (lam * sums[0] / sums[1]) * exps[1]).astype(BF16)
        o = jnp.dot(a, v_ref[:, sl], preferred_element_type=F32) / sums[0]
        ms = jnp.mean(o * o, axis=-1, keepdims=True)
        outs.append(o * lax.rsqrt(ms + EPS) * g_ref[...] * (1.0 - lam_init))
    o_ref[...] = jnp.concatenate(outs, axis=-1).astype(o_ref.dtype)


def _mixer_c(proj, tiles, far, lq1, lk1, lq2, lk2, g, *, row0, n_seq, seq, lam_init):
    nb = seq // BLOCK
    b0 = row0 // BLOCK
    s0 = row0 // seq
    vec = lambda a: a.reshape(1, C_QK_DIM)
    vspec = pl.BlockSpec((1, C_QK_DIM), lambda b, i: (0, 0))
    return pl.pallas_call(
        functools.partial(_mixer_c_kernel, seq=seq, lam_init=lam_init),
        grid=(n_seq, nb),
        in_specs=[
            pl.BlockSpec((BLOCK, C_QK), lambda b, i: (b0 + b * nb + i, _COL_CQ)),
            pl.BlockSpec((seq, C_QK), lambda b, i: (s0 + b, _COL_CK)),
            pl.BlockSpec((seq, C_WIDTH), lambda b, i: (s0 + b, _COL_CV)),
            pl.BlockSpec((2 * C_HEADS, 3, BLOCK, BLOCK), lambda b, i: (0, 0, 0, 0)),
            pl.BlockSpec(memory_space=pltpu.SMEM),
            vspec, vspec, vspec, vspec,
            pl.BlockSpec((1, C_V_DIM), lambda b, i: (0, 0)),
        ],
        out_specs=pl.BlockSpec((BLOCK, C_WIDTH), lambda b, i: (b * nb + i, 0)),
        out_shape=jax.ShapeDtypeStruct((n_seq * seq, C_WIDTH), BF16),
        scratch_shapes=[pltpu.VMEM((2, BLOCK, seq), F32)],
        compiler_params=_cparams(("parallel", "arbitrary")),
        name="mixer_c",
    )(proj, proj, proj, tiles, far, vec(lq1), vec(lk1), vec(lq2), vec(lk2), g.reshape(1, C_V_DIM))


def _out_proj_kernel(x_ref, ya_ref, yb_ref, yc_ref, w_ref, g_ref, x1_ref, h_ref):
    acc = jnp.dot(ya_ref[...], w_ref[0:A_WIDTH, :], preferred_element_type=F32)
    acc += jnp.dot(yb_ref[...], w_ref[A_WIDTH:A_WIDTH + B_WIDTH, :], preferred_element_type=F32)
    acc += jnp.dot(yc_ref[...], w_ref[A_WIDTH + B_WIDTH:, :], preferred_element_type=F32)
    x1 = x_ref[...] + acc
    x1_ref[...] = x1
    ms = jnp.mean(x1 * x1, axis=-1, keepdims=True)
    h_ref[...] = x1 * lax.rsqrt(ms + EPS) * g_ref[...]


def _out_proj(x, ya, yb, yc, w, g, *, tm=256):
    n, d = x.shape
    row = lambda width: pl.BlockSpec((tm, width), lambda i: (i, 0))
    return pl.pallas_call(
        _out_proj_kernel,
        grid=(n // tm,),
        in_specs=[row(d), row(A_WIDTH), row(B_WIDTH), row(C_WIDTH),
                  pl.BlockSpec((d, d), lambda i: (0, 0)),
                  pl.BlockSpec((1, d), lambda i: (0, 0))],
        out_specs=[row(d), row(d)],
        out_shape=[jax.ShapeDtypeStruct((n, d), F32), jax.ShapeDtypeStruct((n, d), F32)],
        compiler_params=_cparams(("parallel",)),
        name="out_proj",
    )(x, ya, yb, yc, w, g.reshape(1, d))


def _top16(x):
    r = x.shape[0]
    iota = lax.broadcasted_iota(I32, x.shape, 0)
    vals, idxs = [], []
    for _ in range(PEER_TOPK):
        m = jnp.max(x, axis=0, keepdims=True)
        idx = jnp.min(jnp.where(x == m, iota, r), axis=0, keepdims=True)
        vals.append(m)
        idxs.append(idx)
        x = jnp.where(iota == idx, -jnp.inf, x)
    return vals, idxs


def _peer_candidates(s1, i1, s2, i2):
    sub = 8
    s1a, i1a = jnp.concatenate(s1, axis=0), jnp.concatenate(i1, axis=0)
    s2a, i2a = jnp.concatenate(s2, axis=0), jnp.concatenate(i2, axis=0)
    row = lax.broadcasted_iota(I32, s2a[:sub].shape, 0)
    cand, cidx = [], []
    first_single = PEER_TOPK // 2
    for a in range(first_single):
        nb = PEER_TOPK // (a + 1)
        if nb > sub:
            cand.append(s1[a] + s2a)
            cidx.append(i1[a] * PEER_KEYS + i2a)
        else:
            cand.append(jnp.where(row < nb, s1[a] + s2a[:sub], -jnp.inf))
            cidx.append(i1[a] * PEER_KEYS + i2a[:sub])
    cand.append(s1a[first_single:] + s2[0])
    cidx.append(i1a[first_single:] * PEER_KEYS + i2[0])
    return jnp.concatenate(cand, axis=0), jnp.concatenate(cidx, axis=0)


def _peer_route_kernel(h_ref, wq_ref, keys_ref, eidx_ref, gate_ref):
    q = jnp.dot(h_ref[...].astype(BF16), wq_ref[...], preferred_element_type=F32).astype(BF16)
    tm = q.shape[0]
    for hp in range(PEER_HEADS):
        tops = []
        for c in range(2):
            hc = 2 * hp + c
            qc = q[:, hc * PEER_HALF:(hc + 1) * PEER_HALF]
            sc = lax.dot_general(keys_ref[hc], qc, (((1,), (1,)), ((), ())), preferred_element_type=F32)
            tops.append(_top16(sc))
        (s1, i1), (s2, i2) = tops
        cand, cidx = _peer_candidates(s1, i1, s2, i2)
        top_s, pos = _top16(cand)
        iota = lax.broadcasted_iota(I32, cand.shape, 0)
        eidx = [jnp.sum(jnp.where(iota == p, cidx, 0), axis=0, keepdims=True) for p in pos]
        ts = jnp.concatenate(top_s, axis=0)
        e = jnp.exp(ts - top_s[0])
        gate_ref[hp] = e / jnp.sum(e, axis=0, keepdims=True)
        eidx_ref[hp] = jnp.concatenate(eidx, axis=0)


def _peer_route(h, wq, keys, *, tm=256):
    n, d = h.shape
    out_spec = pl.BlockSpec((PEER_HEADS, PEER_TOPK, tm), lambda i: (0, 0, i))
    return pl.pallas_call(
        _peer_route_kernel,
        grid=(n // tm,),
        in_specs=[pl.BlockSpec((tm, d), lambda i: (i, 0)),
                  pl.BlockSpec((d, PEER_HEADS * PEER_QDIM), lambda i: (0, 0)),
                  pl.BlockSpec((2 * PEER_HEADS, PEER_KEYS, PEER_HALF), lambda i: (0, 0, 0))],
        out_specs=[out_spec, out_spec],
        out_shape=[jax.ShapeDtypeStruct((PEER_HEADS, PEER_TOPK, n), I32),
                   jax.ShapeDtypeStruct((PEER_HEADS, PEER_TOPK, n), F32)],
        compiler_params=_cparams(("parallel",)),
        name="peer_route",
    )(h, wq, keys)


def _gelu_exact(x):
    return 0.5 * x * (1.0 + lax.erf(x * (2.0 ** -0.5)))


_PEER_T = 8
_DMA_QUEUES = 2
_PEER_ROWS = _PEER_T * PEER_PICKS


def _pack_expert_tables(u, v):
    half = u.shape[1] // 2
    bits = lambda a: lax.bitcast_convert_type(a.astype(BF16), jnp.uint16).astype(jnp.uint32)
    lo = jnp.concatenate([bits(u[:, :half]), bits(v[:, :half])], axis=1)
    hi = jnp.concatenate([bits(u[:, half:]), bits(v[:, half:])], axis=1)
    return lax.bitcast_convert_type(lo | (hi << 16), I32)


def _peer_expert_kernel(idx_hbm, gate_ref, h_ref, x_ref, gf_ref, uv_hbm, o_ref,
                        idx_smem, buf0, buf1, a_ref, idx_sem, row_sem, *, final_norm):
    i = pl.program_id(0)
    n = pl.num_programs(0)
    bufs = (buf0, buf1)
    half = h_ref.shape[1] // 2

    def idx_copy(step, slot):
        dst = idx_smem.at[pl.ds(slot * _PEER_ROWS, _PEER_ROWS)]
        return pltpu.make_async_copy(idx_hbm.at[step], dst, idx_sem.at[slot])

    def issue_token(slot, t):
        for k in range(PEER_PICKS):
            off = pl.multiple_of(idx_smem[slot * _PEER_ROWS + t * PEER_PICKS + k], D_MODEL)
            copy = pltpu.make_async_copy(uv_hbm.at[pl.ds(off, D_MODEL)], bufs[slot].at[t, k], row_sem.at[slot])
            copy.start(priority=k % _DMA_QUEUES)

    def wait_rows(slot):
        pltpu.make_async_copy(bufs[slot], bufs[slot], row_sem.at[slot]).wait()

    @pl.when(i == 0)
    def _():
        idx_copy(0, 0).start()
        idx_copy(0, 0).wait()
        lax.fori_loop(0, _PEER_T, lambda t, c: (issue_token(0, t), c)[1], 0)
        idx_copy(1, 1).start()

    def step(cur, nxt):
        idx_copy(i + 1, nxt).wait()
        idx_copy(i + 2, cur).start()
        wait_rows(cur)
        buf = bufs[cur]

        h = h_ref[...]
        hh = jnp.concatenate([h[:, :half], h[:, half:]], axis=0).astype(BF16)

        @pl.when(i >= 0)
        def _():
            for t in range(_PEER_T // 2):
                issue_token(nxt, t)
            tok = lax.broadcasted_iota(I32, (_PEER_T, 2 * PEER_PICKS), 0)
            a = jnp.zeros((_PEER_T, 2 * PEER_PICKS), F32)
            for t in range(_PEER_T):
                urows = pltpu.bitcast(buf[t, :, :half], BF16)
                r = lax.dot_general(hh, urows, (((1,), (1,)), ((), ())), preferred_element_type=F32)
                r = r[:_PEER_T] + pltpu.roll(r[_PEER_T:], 2 * PEER_PICKS - 1, axis=1)
                a = jnp.where(tok == t, r, a)
            a_ref[...] = a

        for t in range(_PEER_T // 2, _PEER_T):
            issue_token(nxt, t)
        w_even = gate_ref[...] * _gelu_exact(a_ref[...])
        w = jnp.concatenate([w_even, pltpu.roll(w_even, 1, axis=1)], axis=0).astype(BF16)
        tok = lax.broadcasted_iota(I32, (_PEER_T, half), 0)
        lo = jnp.zeros((_PEER_T, half), F32)
        hi = jnp.zeros((_PEER_T, half), F32)
        for t in range(_PEER_T):
            vrows = pltpu.bitcast(buf[t, :, half:], BF16)
            o = jnp.dot(w, vrows, preferred_element_type=F32)
            lo = jnp.where(tok == t, o[:_PEER_T], lo)
            hi = jnp.where(tok == t, o[_PEER_T:], hi)
        y = x_ref[...] + jnp.concatenate([lo, hi], axis=1)
        if final_norm:
            ms = jnp.mean(y * y, axis=-1, keepdims=True)
            y = y * lax.rsqrt(ms + EPS) * gf_ref[...]
        o_ref[...] = y

        @pl.when(i == n - 1)
        def _():
            wait_rows(nxt)
            idx_copy(i + 2, cur).wait()

    for parity in range(2):
        pl.when(i % 2 == parity)(functools.partial(step, parity, 1 - parity))


def _peer_experts(eidx, gate, h, x, uv, g_final, *, final_norm):
    n, d = x.shape
    steps = n // _PEER_T
    row = lambda width: pl.BlockSpec((_PEER_T, width), lambda i: (i, 0))
    gate2 = jnp.stack([gate, jnp.zeros_like(gate)], axis=-1).reshape(n, 2 * PEER_PICKS)
    return pl.pallas_call(
        functools.partial(_peer_expert_kernel, final_norm=final_norm),
        grid=(steps,),
        in_specs=[pl.BlockSpec(memory_space=pl.ANY), row(2 * PEER_PICKS), row(d), row(d),
                  pl.BlockSpec((1, d), lambda i: (0, 0)),
                  pl.BlockSpec(memory_space=pl.ANY)],
        out_specs=row(d),
        out_shape=jax.ShapeDtypeStruct((n, d), F32),
        scratch_shapes=[pltpu.SMEM((2 * _PEER_ROWS,), I32),
                        pltpu.VMEM((_PEER_T, PEER_PICKS, d), I32),
                        pltpu.VMEM((_PEER_T, PEER_PICKS, d), I32),
                        pltpu.VMEM((_PEER_T, 2 * PEER_PICKS), F32),
                        pltpu.SemaphoreType.DMA((2,)),
                        pltpu.SemaphoreType.DMA((2,))],
        compiler_params=_cparams(("arbitrary",)),
        name="peer_experts",
    )(jnp.pad((eidx * d).reshape(steps, _PEER_ROWS), ((0, 2), (0, 0))), gate2, h, x, g_final.reshape(1, d),
      uv.reshape(-1))


def _encoder(x, groups, rel_bias, g_attn, w_in, a_sink, g_grp_a, nat_rpb, g_grp_b, lam_q1, lam_k1, lam_q2, lam_k2,
             g_subln, w_out, g_ffn, peer_wq, peer_keys, peer_u, peer_v, g_final):
    depth = w_in.shape[0]
    splits = np.cumsum((0,) + IN_SIZES)
    perm = np.concatenate([np.arange(splits[s], splits[s + 1]) for s in _PROJ_ORDER])
    bias_a = _bias_a_table(rel_bias[:, :A_Q_HEADS])
    tiles_c, far_c = _bias_c_tables(rel_bias[:, A_Q_HEADS:])
    n = x.shape[0]
    for l in range(depth):
        lam_init = 0.8 - 0.6 * math.exp(-0.3 * l)
        w_l = w_in[l][:, perm].astype(BF16)
        proj = _norm_proj(x, g_attn[l], w_l)
        bias_b = _bias_b_table(nat_rpb[l])
        ya, yb, yc = [], [], []
        for (row0, n_seq, seq) in groups:
            kw = dict(row0=row0, n_seq=n_seq, seq=seq)
            ya.append(_mixer_a(proj, bias_a, a_sink[l], g_grp_a[l], **kw))
            yb.append(_mixer_b(proj, bias_b, g_grp_b[l], **kw))
            yc.append(_mixer_c(proj, tiles_c, far_c, lam_q1[l], lam_k1[l], lam_q2[l], lam_k2[l], g_subln[l],
                               lam_init=lam_init, **kw))
        cat = lambda parts: parts[0] if len(parts) == 1 else jnp.concatenate(parts, axis=0)
        x1, h2 = _out_proj(x, cat(ya), cat(yb), cat(yc), w_out[l].astype(BF16), g_ffn[l])
        keys = peer_keys[l].reshape(2 * PEER_HEADS, PEER_KEYS, PEER_HALF).astype(BF16)
        eidx, gate = _peer_route(h2, peer_wq[l].astype(BF16), keys)
        eidx = eidx.reshape(PEER_PICKS, n).T
        gate = gate.reshape(PEER_PICKS, n).T
        uv = _pack_expert_tables(peer_u[l], peer_v[l])
        x = _peer_experts(eidx, gate, h2, x1, uv, g_final, final_norm=(l == depth - 1))
    return x


def kernel(x_prompt, x_sample, rel_bias, g_attn, w_in, a_sink, g_grp_a, nat_rpb, g_grp_b, lam_q1, lam_k1, lam_q2, lam_k2, g_subln, w_out, g_ffn, peer_wq, peer_keys, peer_u, peer_v, g_final):
    bp, sp, d = x_prompt.shape
    bs, ss, _ = x_sample.shape
    x = jnp.concatenate([x_prompt.reshape(bp * sp, d), x_sample.reshape(bs * ss, d)], axis=0)
    groups = ((0, bp, sp), (bp * sp, bs, ss))
    y = _encoder(x, groups, rel_bias, g_attn, w_in, a_sink, g_grp_a, nat_rpb, g_grp_b, lam_q1, lam_k1, lam_q2, lam_k2,
                 g_subln, w_out, g_ffn, peer_wq, peer_keys, peer_u, peer_v, g_final)
    return (y[:bp * sp].reshape(bp, sp, d), y[bp * sp:].reshape(bs, ss, d))
```

```python
import functools
import math

import numpy as np
import jax
import jax.numpy as jnp
from jax import lax
from jax.experimental import pallas as pl
from jax.experimental.pallas import tpu as pltpu

F32 = jnp.float32
BF16 = jnp.bfloat16
I32 = jnp.int32
EPS = 1e-6
NEG = -1e30

D_MODEL = 2048
DEPTH = 2
HEAD_DIM = 128
A_Q_HEADS = 6
A_KV_HEADS = 2
A_GROUP = A_Q_HEADS // A_KV_HEADS
WINDOW = 128
BLOCK = 128
B_HEADS = 6
GRID_W = 64
NB_KH = 8
NB_KW = 16
C_HEADS = 4
C_QK_DIM = 64
C_V_DIM = 128
A_WIDTH = A_Q_HEADS * HEAD_DIM
B_WIDTH = B_HEADS * HEAD_DIM
C_WIDTH = C_HEADS * C_V_DIM
A_KV = A_KV_HEADS * HEAD_DIM
C_QK = C_HEADS * 2 * C_QK_DIM
IN_SIZES = (A_WIDTH, A_KV, A_KV, B_WIDTH, B_WIDTH, B_WIDTH, C_QK, C_QK, C_WIDTH)
IN_WIDTH = sum(IN_SIZES)
REL_BUCKETS = 32
REL_MAX_DIST = 128
PEER_HEADS = 8
PEER_KEYS = 128
PEER_TOPK = 16
PEER_QDIM = 256
PEER_HALF = PEER_QDIM // 2
PEER_PICKS = PEER_HEADS * PEER_TOPK

_PROJ_ORDER = (0, 3, 4, 5, 1, 2, 6, 7, 8)
_COL_AQ, _COL_BQ, _COL_BK, _COL_BV = 0, 1, 2, 3
_COL_AK, _COL_AV = 12, 13
_COL_CQ, _COL_CK, _COL_CV = 7, 8, 9

_VMEM_LIMIT = 56 * 1024 * 1024


def _cparams(sem):
    return pltpu.CompilerParams(dimension_semantics=sem, vmem_limit_bytes=_VMEM_LIMIT)


def _t5_bucket(rel):
    nb = REL_BUCKETS // 2
    max_exact = nb // 2
    ret = jnp.where(rel > 0, nb, 0)
    n = jnp.abs(rel)
    nf = jnp.maximum(n, 1).astype(F32)
    large = max_exact + (jnp.log(nf / max_exact) / math.log(REL_MAX_DIST / max_exact) * (nb - max_exact)).astype(I32)
    large = jnp.minimum(large, nb - 1)
    return ret + jnp.where(n < max_exact, n, large)


def _lookup(table, idx):
    onehot = jax.nn.one_hot(idx, table.shape[0], dtype=F32)
    return jnp.einsum('...b,bc->...c', onehot, table.astype(F32), precision=lax.Precision.HIGHEST)


def _bias_a_table(rel_bias_a):
    rel = np.arange(3 * BLOCK)[None, :] - BLOCK - np.arange(BLOCK)[:, None]
    bias = _lookup(rel_bias_a, _t5_bucket(jnp.asarray(rel)))
    bias = jnp.where(jnp.asarray(np.abs(rel) <= WINDOW)[:, :, None], bias, NEG)
    return bias.transpose(2, 0, 1)


def _bias_b_table(rpb):
    classes = (((0, 1), (0, 0)), ((2, 3), (0, 0)), ((4, 5), (0, 1)), ((5, 6), (1, 1)), ((7, 8), (1, 1)))
    qc = np.arange(GRID_W)[:, None]
    kc = np.arange(GRID_W)[None, :]
    c0 = np.clip(qc - NB_KW // 2, 0, GRID_W - NB_KW)
    in_cols = (kc >= c0) & (kc < c0 + NB_KW)
    n_dc = 2 * NB_KW - 1
    dc_onehot = ((kc - qc + NB_KW - 1)[:, :, None] == np.arange(n_dc)).astype(np.float32)
    blocks = jnp.einsum('qkc,hdc->hdqk', jnp.asarray(dc_onehot), rpb.astype(F32), precision=lax.Precision.HIGHEST)
    blocks = jnp.where(jnp.asarray(in_cols), blocks, NEG)
    neg = jnp.full((rpb.shape[0], GRID_W, GRID_W), NEG, F32)
    tabs = []
    for (rq, r0) in classes:
        rows = []
        for a in range(_B_QROWS):
            parts = [blocks[:, kr - rq[a] + NB_KH - 1] if r0[a] <= kr < r0[a] + NB_KH else neg
                     for kr in range(_B_KROWS)]
            rows.append(jnp.concatenate(parts, axis=-1))
        tabs.append(jnp.concatenate(rows, axis=1))
    return jnp.stack(tabs)


def _bias_c_tables(rel_bias_c):
    q = np.arange(BLOCK)[:, None]
    k = np.arange(BLOCK)[None, :]
    far = _lookup(rel_bias_c, _t5_bucket(jnp.asarray([-2 * BLOCK, 2 * BLOCK])))
    tiles = []
    for d in (-1, 0, 1):
        rel = k + d * BLOCK - q
        t = _lookup(rel_bias_c, _t5_bucket(jnp.asarray(rel))).transpose(2, 0, 1)
        base = far[0] if d < 0 else far[1]
        tiles.append(t - base[:, None, None])
    return jnp.stack(tiles, axis=1), far.T


def _norm_proj_kernel(x_ref, g_ref, w_ref, o_ref, h_ref):
    @pl.when(pl.program_id(1) == 0)
    def _():
        x = x_ref[...]
        ms = jnp.mean(x * x, axis=-1, keepdims=True)
        h_ref[...] = (x * lax.rsqrt(ms + EPS) * g_ref[...]).astype(BF16)

    o_ref[...] = jnp.dot(h_ref[...], w_ref[...], preferred_element_type=F32).astype(o_ref.dtype)


def _norm_proj(x, g, w, *, tm=512, tn=1280):
    n, d = x.shape
    width = w.shape[1]
    return pl.pallas_call(
        _norm_proj_kernel,
        grid=(n // tm, width // tn),
        in_specs=[
            pl.BlockSpec((tm, d), lambda i, j: (i, 0)),
            pl.BlockSpec((1, d), lambda i, j: (0, 0)),
            pl.BlockSpec((d, tn), lambda i, j: (0, j)),
        ],
        out_specs=pl.BlockSpec((tm, tn), lambda i, j: (i, j)),
        out_shape=jax.ShapeDtypeStruct((n, width), BF16),
        scratch_shapes=[pltpu.VMEM((tm, d), BF16)],
        compiler_params=_cparams(("parallel", "arbitrary")),
        name="norm_proj",
    )(x, g.reshape(1, d), w)


def _group_norm_store(o_ref, outs, g_ref):
    o = jnp.concatenate(outs, axis=-1)
    ms = jnp.mean(o * o, axis=-1, keepdims=True)
    o_ref[...] = (o * lax.rsqrt(ms + EPS) * g_ref[...]).astype(o_ref.dtype)


def _mixer_a_kernel(q_ref, kp_ref, kc_ref, kn_ref, vp_ref, vc_ref, vn_ref, bias_ref, sink_ref, g_ref, o_ref):
    i = pl.program_id(1)
    nb = pl.num_programs(1)
    scale = HEAD_DIM ** -0.5
    col = lax.broadcasted_iota(I32, (BLOCK, 3 * BLOCK), 1)
    lo = jnp.where(i == 0, BLOCK, 0)
    hi = jnp.where(i == nb - 1, 2 * BLOCK, 3 * BLOCK)
    in_seq = (col >= lo) & (col < hi)
    ks, vs, scores = [], [], []
    for kv in range(A_KV_HEADS):
        sl = slice(kv * HEAD_DIM, (kv + 1) * HEAD_DIM)
        ks.append(jnp.concatenate([kp_ref[:, sl], kc_ref[:, sl], kn_ref[:, sl]], axis=0))
        vs.append(jnp.concatenate([vp_ref[:, sl], vc_ref[:, sl], vn_ref[:, sl]], axis=0))
    for h in range(A_Q_HEADS):
        q = q_ref[:, h * HEAD_DIM:(h + 1) * HEAD_DIM]
        scores.append(lax.dot_general(q, ks[h // A_GROUP], (((1,), (1,)), ((), ())), preferred_element_type=F32))
    probs, denoms = [], []
    for h in range(A_Q_HEADS):
        s = jnp.where(in_seq, scores[h] * scale + bias_ref[h], NEG)
        sk = sink_ref[h]
        m = jnp.maximum(jnp.max(s, axis=-1, keepdims=True), sk)
        p = jnp.exp(s - m)
        denoms.append(jnp.sum(p, axis=-1, keepdims=True) + jnp.exp(sk - m))
        probs.append(p.astype(BF16))
    outs = [jnp.dot(probs[h], vs[h // A_GROUP], preferred_element_type=F32) / denoms[h] for h in range(A_Q_HEADS)]
    _group_norm_store(o_ref, outs, g_ref)


def _mixer_a(proj, bias, sink, g, *, row0, n_seq, seq):
    nb = seq // BLOCK
    b0 = row0 // BLOCK

    def blk(shift, colblk):
        return lambda b, i: (b0 + b * nb + jnp.clip(i + shift, 0, nb - 1), colblk)

    kv_spec = lambda shift, colblk: pl.BlockSpec((BLOCK, A_KV), blk(shift, colblk))
    return pl.pallas_call(
        _mixer_a_kernel,
        grid=(n_seq, nb),
        in_specs=[
            pl.BlockSpec((BLOCK, A_WIDTH), blk(0, _COL_AQ)),
            kv_spec(-1, _COL_AK), kv_spec(0, _COL_AK), kv_spec(1, _COL_AK),
            kv_spec(-1, _COL_AV), kv_spec(0, _COL_AV), kv_spec(1, _COL_AV),
            pl.BlockSpec((A_Q_HEADS, BLOCK, 3 * BLOCK), lambda b, i: (0, 0, 0)),
            pl.BlockSpec(memory_space=pltpu.SMEM),
            pl.BlockSpec((1, A_WIDTH), lambda b, i: (0, 0)),
        ],
        out_specs=pl.BlockSpec((BLOCK, A_WIDTH), lambda b, i: (b * nb + i, 0)),
        out_shape=jax.ShapeDtypeStruct((n_seq * seq, A_WIDTH), BF16),
        compiler_params=_cparams(("parallel", "parallel")),
        name="mixer_a",
    )(proj, proj, proj, proj, proj, proj, proj, bias, sink, g.reshape(1, A_WIDTH))


_B_QROWS = 2
_B_KROWS = NB_KH + _B_QROWS - 1


def _mixer_b_kernel(q_ref, k_ref, v_ref, bias_ref, g_ref, o_ref, *, rows):
    i = pl.program_id(1)
    scale = HEAD_DIM ** -0.5
    kstart = jnp.clip(_B_QROWS * i - NB_KH // 2, 0, rows - _B_KROWS)
    ks = pl.multiple_of(kstart * GRID_W, GRID_W)
    nk = _B_KROWS * GRID_W
    heads = [slice(h * HEAD_DIM, (h + 1) * HEAD_DIM) for h in range(B_HEADS)]
    scores = [lax.dot_general(q_ref[:, sl], k_ref[pl.ds(ks, nk), sl], (((1,), (1,)), ((), ())),
                              preferred_element_type=F32) for sl in heads]
    probs, denoms = [], []
    for h in range(B_HEADS):
        s = scores[h] * scale + bias_ref[0, h]
        p = jnp.exp(s - jnp.max(s, axis=-1, keepdims=True))
        denoms.append(jnp.sum(p, axis=-1, keepdims=True))
        probs.append(p.astype(BF16))
    outs = [jnp.dot(probs[h], v_ref[pl.ds(ks, nk), sl], preferred_element_type=F32) / denoms[h]
            for h, sl in enumerate(heads)]
    _group_norm_store(o_ref, outs, g_ref)


def _mixer_b(proj, bias, g, *, row0, n_seq, seq):
    rows = seq // GRID_W
    nrp = rows // _B_QROWS
    tq = _B_QROWS * GRID_W
    b0 = row0 // tq
    s0 = row0 // seq

    def cls(i):
        return jnp.where(i == 0, 0, jnp.where(i == 1, 1, jnp.where(i == nrp - 2, 3, jnp.where(i == nrp - 1, 4, 2))))

    return pl.pallas_call(
        functools.partial(_mixer_b_kernel, rows=rows),
        grid=(n_seq, nrp),
        in_specs=[
            pl.BlockSpec((tq, B_WIDTH), lambda b, i: (b0 + b * nrp + i, _COL_BQ)),
            pl.BlockSpec((seq, B_WIDTH), lambda b, i: (s0 + b, _COL_BK)),
            pl.BlockSpec((seq, B_WIDTH), lambda b, i: (s0 + b, _COL_BV)),
            pl.BlockSpec((1, B_HEADS, tq, _B_KROWS * GRID_W), lambda b, i: (cls(i), 0, 0, 0)),
            pl.BlockSpec((1, B_WIDTH), lambda b, i: (0, 0)),
        ],
        out_specs=pl.BlockSpec((tq, B_WIDTH), lambda b, i: (b * nrp + i, 0)),
        out_shape=jax.ShapeDtypeStruct((n_seq * seq, B_WIDTH), BF16),
        compiler_params=_cparams(("parallel", "arbitrary")),
        name="mixer_b",
    )(proj, proj, proj, bias, g.reshape(1, B_WIDTH))


def _mixer_c_kernel(q_ref, k_ref, v_ref, tiles_ref, far_ref, lq1_ref, lk1_ref, lq2_ref, lk2_ref, g_ref, o_ref,
                    *s_refs, seq, lam_init):
    i = pl.program_id(1)
    nb = seq // BLOCK
    scale = C_QK_DIM ** -0.5
    lam = (jnp.exp(jnp.sum(lq1_ref[...] * lk1_ref[...], axis=-1, keepdims=True))
           - jnp.exp(jnp.sum(lq2_ref[...] * lk2_ref[...], axis=-1, keepdims=True)) + lam_init)
    before = lax.broadcasted_iota(I32, (1, seq), 1) < i * BLOCK
    lane = lax.broadcasted_iota(I32, (BLOCK, HEAD_DIM), 1)
    for h in range(C_HEADS):
        sl = slice(h * HEAD_DIM, (h + 1) * HEAD_DIM)
        qh = q_ref[:, sl] * scale
        kh = k_ref[:, sl]
        qs = jnp.concatenate(
            [jnp.where((lane >= c * C_QK_DIM) & (lane < (c + 1) * C_QK_DIM), qh, jnp.zeros_like(qh)) for c in range(2)],
            axis=0)
        s = lax.dot_general(qs, kh, (((1,), (1,)), ((), ())), preferred_element_type=F32)
        for c in range(2):
            hc = 2 * h + c
            s_refs[hc][...] = s[c * BLOCK:(c + 1) * BLOCK] + jnp.where(before, far_ref[hc, 0], far_ref[hc, 1])
    outs = []
    for h in range(C_HEADS):
        sl = slice(h * HEAD_DIM, (h + 1) * HEAD_DIM)
        exps, sums = [], []
        for c in range(2):
            hc = 2 * h + c
            for d in range(3):
                j = i + (d - 1)
                inside = jnp.where((j >= 0) & (j < nb), 1.0, 0.0)
                cols = pl.ds(pl.multiple_of(jnp.clip(j, 0, nb - 1) * BLOCK, BLOCK), BLOCK)
                s_refs[hc][:, cols] += tiles_ref[hc, d] * inside
            s = s_refs[hc][...]
            e = jnp.exp(s - jnp.max(s, axis=-1, keepdims=True))
            exps.append(e)
            sums.append(jnp.sum(e, axis=-1, keepdims=True))
        a = (exps[0] - (lam * sums[0] / sums[1]) * exps[1]).astype(BF16)
        o = jnp.dot(a, v_ref[:, sl], preferred_element_type=F32) / sums[0]
        ms = jnp.mean(o * o, axis=-1, keepdims=True)
        outs.append(o * lax.rsqrt(ms + EPS) * g_ref[...] * (1.0 - lam_init))
    o_ref[...] = jnp.concatenate(outs, axis=-1).astype(o_ref.dtype)


def _mixer_c(proj, tiles, far, lq1, lk1, lq2, lk2, g, *, row0, n_seq, seq, lam_init):
    nb = seq // BLOCK
    b0 = row0 // BLOCK
    s0 = row0 // seq
    vec = lambda a: a.reshape(1, C_QK_DIM)
    vspec = pl.BlockSpec((1, C_QK_DIM), lambda b, i: (0, 0))
    return pl.pallas_call(
        functools.partial(_mixer_c_kernel, seq=seq, lam_init=lam_init),
        grid=(n_seq, nb),
        in_specs=[
            pl.BlockSpec((BLOCK, C_QK), lambda b, i: (b0 + b * nb + i, _COL_CQ)),
            pl.BlockSpec((seq, C_QK), lambda b, i: (s0 + b, _COL_CK)),
            pl.BlockSpec((seq, C_WIDTH), lambda b, i: (s0 + b, _COL_CV)),
            pl.BlockSpec((2 * C_HEADS, 3, BLOCK, BLOCK), lambda b, i: (0, 0, 0, 0)),
            pl.BlockSpec(memory_space=pltpu.SMEM),
            vspec, vspec, vspec, vspec,
            pl.BlockSpec((1, C_V_DIM), lambda b, i: (0, 0)),
        ],
        out_specs=pl.BlockSpec((BLOCK, C_WIDTH), lambda b, i: (b * nb + i, 0)),
        out_shape=jax.ShapeDtypeStruct((n_seq * seq, C_WIDTH), BF16),
        scratch_shapes=[pltpu.VMEM((BLOCK, seq), F32)] * (2 * C_HEADS),
        compiler_params=_cparams(("parallel", "arbitrary")),
        name="mixer_c",
    )(proj, proj, proj, tiles, far, vec(lq1), vec(lk1), vec(lq2), vec(lk2), g.reshape(1, C_V_DIM))


def _out_proj_kernel(x_ref, ya_ref, yb_ref, yc_ref, w_ref, g_ref, x1_ref, h_ref):
    acc = jnp.dot(ya_ref[...], w_ref[0:A_WIDTH, :], preferred_element_type=F32)
    acc += jnp.dot(yb_ref[...], w_ref[A_WIDTH:A_WIDTH + B_WIDTH, :], preferred_element_type=F32)
    acc += jnp.dot(yc_ref[...], w_ref[A_WIDTH + B_WIDTH:, :], preferred_element_type=F32)
    x1 = x_ref[...] + acc
    x1_ref[...] = x1
    ms = jnp.mean(x1 * x1, axis=-1, keepdims=True)
    h_ref[...] = x1 * lax.rsqrt(ms + EPS) * g_ref[...]


def _out_proj(x, ya, yb, yc, w, g, *, tm=256):
    n, d = x.shape
    row = lambda width: pl.BlockSpec((tm, width), lambda i: (i, 0))
    return pl.pallas_call(
        _out_proj_kernel,
        grid=(n // tm,),
        in_specs=[row(d), row(A_WIDTH), row(B_WIDTH), row(C_WIDTH),
                  pl.BlockSpec((d, d), lambda i: (0, 0)),
                  pl.BlockSpec((1, d), lambda i: (0, 0))],
        out_specs=[row(d), row(d)],
        out_shape=[jax.ShapeDtypeStruct((n, d), F32), jax.ShapeDtypeStruct((n, d), F32)],
        compiler_params=_cparams(("parallel",)),
        name="out_proj",
    )(x, ya, yb, yc, w, g.reshape(1, d))


def _top16(x):
    r = x.shape[0]
    iota = lax.broadcasted_iota(I32, x.shape, 0)
    vals, idxs = [], []
    for _ in range(PEER_TOPK):
        m = jnp.max(x, axis=0, keepdims=True)
        idx = jnp.min(jnp.where(x == m, iota, r), axis=0, keepdims=True)
        vals.append(m)
        idxs.append(idx)
        x = jnp.where(iota == idx, -jnp.inf, x)
    return vals, idxs


def _peer_candidates(s1, i1, s2, i2):
    sub = 8
    s1a, i1a = jnp.concatenate(s1, axis=0), jnp.concatenate(i1, axis=0)
    s2a, i2a = jnp.concatenate(s2, axis=0), jnp.concatenate(i2, axis=0)
    row = lax.broadcasted_iota(I32, s2a[:sub].shape, 0)
    cand, cidx = [], []
    first_single = PEER_TOPK // 2
    for a in range(first_single):
        nb = PEER_TOPK // (a + 1)
        if nb > sub:
            cand.append(s1[a] + s2a)
            cidx.append(i1[a] * PEER_KEYS + i2a)
        else:
            cand.append(jnp.where(row < nb, s1[a] + s2a[:sub], -jnp.inf))
            cidx.append(i1[a] * PEER_KEYS + i2a[:sub])
    cand.append(s1a[first_single:] + s2[0])
    cidx.append(i1a[first_single:] * PEER_KEYS + i2[0])
    return jnp.concatenate(cand, axis=0), jnp.concatenate(cidx, axis=0)


def _peer_route_kernel(h_ref, wq_ref, keys_ref, eidx_ref, gate_ref):
    q = jnp.dot(h_ref[...].astype(BF16), wq_ref[...], preferred_element_type=F32).astype(BF16)
    tm = q.shape[0]
    for hp in range(PEER_HEADS):
        tops = []
        for c in range(2):
            hc = 2 * hp + c
            qc = q[:, hc * PEER_HALF:(hc + 1) * PEER_HALF]
            sc = lax.dot_general(keys_ref[hc], qc, (((1,), (1,)), ((), ())), preferred_element_type=F32)
            tops.append(_top16(sc))
        (s1, i1), (s2, i2) = tops
        cand, cidx = _peer_candidates(s1, i1, s2, i2)
        top_s, pos = _top16(cand)
        iota = lax.broadcasted_iota(I32, cand.shape, 0)
        eidx = [jnp.sum(jnp.where(iota == p, cidx, 0), axis=0, keepdims=True) for p in pos]
        ts = jnp.concatenate(top_s, axis=0)
        e = jnp.exp(ts - top_s[0])
        gate_ref[hp] = e / jnp.sum(e, axis=0, keepdims=True)
        eidx_ref[hp] = jnp.concatenate(eidx, axis=0)


def _peer_route(h, wq, keys, *, tm=256):
    n, d = h.shape
    out_spec = pl.BlockSpec((PEER_HEADS, PEER_TOPK, tm), lambda i: (0, 0, i))
    return pl.pallas_call(
        _peer_route_kernel,
        grid=(n // tm,),
        in_specs=[pl.BlockSpec((tm, d), lambda i: (i, 0)),
                  pl.BlockSpec((d, PEER_HEADS * PEER_QDIM), lambda i: (0, 0)),
                  pl.BlockSpec((2 * PEER_HEADS, PEER_KEYS, PEER_HALF), lambda i: (0, 0, 0))],
        out_specs=[out_spec, out_spec],
        out_shape=[jax.ShapeDtypeStruct((PEER_HEADS, PEER_TOPK, n), I32),
                   jax.ShapeDtypeStruct((PEER_HEADS, PEER_TOPK, n), F32)],
        compiler_params=_cparams(("parallel",)),
        name="peer_route",
    )(h, wq, keys)


def _gelu_exact(x):
    return 0.5 * x * (1.0 + lax.erf(x * (2.0 ** -0.5)))


_PEER_T = 8
_DMA_QUEUES = 2
_PEER_ROWS = _PEER_T * PEER_PICKS


def _pack_expert_tables(u, v):
    half = u.shape[1] // 2
    bits = lambda a: lax.bitcast_convert_type(a.astype(BF16), jnp.uint16).astype(jnp.uint32)
    lo = jnp.concatenate([bits(u[:, :half]), bits(v[:, :half])], axis=1)
    hi = jnp.concatenate([bits(u[:, half:]), bits(v[:, half:])], axis=1)
    return lax.bitcast_convert_type(lo | (hi << 16), I32)


def _peer_expert_kernel(idx_hbm, gate_ref, h_ref, x_ref, gf_ref, uv_hbm, o_ref,
                        idx_smem, buf0, buf1, a_ref, idx_sem, row_sem, *, final_norm):
    i = pl.program_id(0)
    n = pl.num_programs(0)
    bufs = (buf0, buf1)
    half = h_ref.shape[1] // 2

    def idx_copy(step, slot):
        dst = idx_smem.at[pl.ds(slot * _PEER_ROWS, _PEER_ROWS)]
        return pltpu.make_async_copy(idx_hbm.at[step], dst, idx_sem.at[slot])

    def issue_token(slot, t):
        for k in range(PEER_PICKS):
            off = pl.multiple_of(idx_smem[slot * _PEER_ROWS + t * PEER_PICKS + k], D_MODEL)
            copy = pltpu.make_async_copy(uv_hbm.at[pl.ds(off, D_MODEL)], bufs[slot].at[t, k], row_sem.at[slot])
            copy.start(priority=k % _DMA_QUEUES)

    def wait_rows(slot):
        pltpu.make_async_copy(bufs[slot], bufs[slot], row_sem.at[slot]).wait()

    @pl.when(i == 0)
    def _():
        idx_copy(0, 0).start()
        idx_copy(0, 0).wait()
        lax.fori_loop(0, _PEER_T, lambda t, c: (issue_token(0, t), c)[1], 0)
        idx_copy(1, 1).start()

    def step(cur, nxt):
        idx_copy(i + 1, nxt).wait()
        idx_copy(i + 2, cur).start()
        wait_rows(cur)
        buf = bufs[cur]

        h = h_ref[...]
        hh = jnp.concatenate([h[:, :half], h[:, half:]], axis=0).astype(BF16)

        @pl.when(i >= 0)
        def _():
            for t in range(_PEER_T // 2):
                issue_token(nxt, t)
            tok = lax.broadcasted_iota(I32, (_PEER_T, 2 * PEER_PICKS), 0)
            a = jnp.zeros((_PEER_T, 2 * PEER_PICKS), F32)
            for t in range(_PEER_T):
                urows = pltpu.bitcast(buf[t, :, :half], BF16)
                r = lax.dot_general(hh, urows, (((1,), (1,)), ((), ())), preferred_element_type=F32)
                r = r[:_PEER_T] + pltpu.roll(r[_PEER_T:], 2 * PEER_PICKS - 1, axis=1)
                a = jnp.where(tok == t, r, a)
            a_ref[...] = a

        for t in range(_PEER_T // 2, _PEER_T):
            issue_token(nxt, t)
        w_even = gate_ref[...] * _gelu_exact(a_ref[...])
        w = jnp.concatenate([w_even, pltpu.roll(w_even, 1, axis=1)], axis=0).astype(BF16)
        tok = lax.broadcasted_iota(I32, (_PEER_T, half), 0)
        lo = jnp.zeros((_PEER_T, half), F32)
        hi = jnp.zeros((_PEER_T, half), F32)
        for t in range(_PEER_T):
            vrows = pltpu.bitcast(buf[t, :, half:], BF16)
            o = jnp.dot(w, vrows, preferred_element_type=F32)
            lo = jnp.where(tok == t, o[:_PEER_T], lo)
            hi = jnp.where(tok == t, o[_PEER_T:], hi)
        y = x_ref[...] + jnp.concatenate([lo, hi], axis=1)
        if final_norm:
            ms = jnp.mean(y * y, axis=-1, keepdims=True)
            y = y * lax.rsqrt(ms + EPS) * gf_ref[...]
        o_ref[...] = y

        @pl.when(i == n - 1)
        def _():
            wait_rows(nxt)
            idx_copy(i + 2, cur).wait()

    for parity in range(2):
        pl.when(i % 2 == parity)(functools.partial(step, parity, 1 - parity))


def _peer_experts(eidx, gate, h, x, uv, g_final, *, final_norm):
    n, d = x.shape
    steps = n // _PEER_T
    row = lambda width: pl.BlockSpec((_PEER_T, width), lambda i: (i, 0))
    gate2 = jnp.stack([gate, jnp.zeros_like(gate)], axis=-1).reshape(n, 2 * PEER_PICKS)
    return pl.pallas_call(
        functools.partial(_peer_expert_kernel, final_norm=final_norm),
        grid=(steps,),
        in_specs=[pl.BlockSpec(memory_space=pl.ANY), row(2 * PEER_PICKS), row(d), row(d),
                  pl.BlockSpec((1, d), lambda i: (0, 0)),
                  pl.BlockSpec(memory_space=pl.ANY)],
        out_specs=row(d),
        out_shape=jax.ShapeDtypeStruct((n, d), F32),
        scratch_shapes=[pltpu.SMEM((2 * _PEER_ROWS,), I32),
                        pltpu.VMEM((_PEER_T, PEER_PICKS, d), I32),
                        pltpu.VMEM((_PEER_T, PEER_PICKS, d), I32),
                        pltpu.VMEM((_PEER_T, 2 * PEER_PICKS), F32),
                        pltpu.SemaphoreType.DMA((2,)),
                        pltpu.SemaphoreType.DMA((2,))],
        compiler_params=_cparams(("arbitrary",)),
        name="peer_experts",
    )(jnp.pad((eidx * d).reshape(steps, _PEER_ROWS), ((0, 2), (0, 0))), gate2, h, x, g_final.reshape(1, d),
      uv.reshape(-1))


def _encoder(x, groups, rel_bias, g_attn, w_in, a_sink, g_grp_a, nat_rpb, g_grp_b, lam_q1, lam_k1, lam_q2, lam_k2,
             g_subln, w_out, g_ffn, peer_wq, peer_keys, peer_u, peer_v, g_final):
    depth = w_in.shape[0]
    splits = [int(c) for c in np.cumsum((0,) + IN_SIZES)]
    bias_a = _bias_a_table(rel_bias[:, :A_Q_HEADS])
    tiles_c, far_c = _bias_c_tables(rel_bias[:, A_Q_HEADS:])
    n = x.shape[0]
    for l in range(depth):
        lam_init = 0.8 - 0.6 * math.exp(-0.3 * l)
        w_l = jnp.concatenate([w_in[l][:, splits[s]:splits[s + 1]] for s in _PROJ_ORDER], axis=1).astype(BF16)
        proj = _norm_proj(x, g_attn[l], w_l)
        bias_b = _bias_b_table(nat_rpb[l])
        ya, yb, yc = [], [], []
        for (row0, n_seq, seq) in groups:
            kw = dict(row0=row0, n_seq=n_seq, seq=seq)
            ya.append(_mixer_a(proj, bias_a, a_sink[l], g_grp_a[l], **kw))
            yb.append(_mixer_b(proj, bias_b, g_grp_b[l], **kw))
            yc.append(_mixer_c(proj, tiles_c, far_c, lam_q1[l], lam_k1[l], lam_q2[l], lam_k2[l], g_subln[l],
                               lam_init=lam_init, **kw))
        cat = lambda parts: parts[0] if len(parts) == 1 else jnp.concatenate(parts, axis=0)
        x1, h2 = _out_proj(x, cat(ya), cat(yb), cat(yc), w_out[l].astype(BF16), g_ffn[l])
        keys = peer_keys[l].reshape(2 * PEER_HEADS, PEER_KEYS, PEER_HALF).astype(BF16)
        eidx, gate = _peer_route(h2, peer_wq[l].astype(BF16), keys)
        eidx = eidx.reshape(PEER_PICKS, n).T
        gate = gate.reshape(PEER_PICKS, n).T
        uv = _pack_expert_tables(peer_u[l], peer_v[l])
        x = _peer_experts(eidx, gate, h2, x1, uv, g_final, final_norm=(l == depth - 1))
    return x


def kernel(x_prompt, x_sample, rel_bias, g_attn, w_in, a_sink, g_grp_a, nat_rpb, g_grp_b, lam_q1, lam_k1, lam_q2, lam_k2, g_subln, w_out, g_ffn, peer_wq, peer_keys, peer_u, peer_v, g_final):
    bp, sp, d = x_prompt.shape
    bs, ss, _ = x_sample.shape
    x = jnp.concatenate([x_prompt.reshape(bp * sp, d), x_sample.reshape(bs * ss, d)], axis=0)
    groups = ((0, bp, sp), (bp * sp, bs, ss))
    y = _encoder(x, groups, rel_bias, g_attn, w_in, a_sink, g_grp_a, nat_rpb, g_grp_b, lam_q1, lam_k1, lam_q2, lam_k2,
                 g_subln, w_out, g_ffn, peer_wq, peer_keys, peer_u, peer_v, g_final)
    return (y[:bp * sp].reshape(bp, sp, d), y[bp * sp:].reshape(bs, ss, d))
```

```python
import functools
import math

import numpy as np
import jax
import jax.numpy as jnp
from jax import lax
from jax.experimental import pallas as pl
from jax.experimental.pallas import tpu as pltpu

F32 = jnp.float32
BF16 = jnp.bfloat16
I32 = jnp.int32
EPS = 1e-6
NEG = -1e30

D_MODEL = 2048
DEPTH = 2
HEAD_DIM = 128
A_Q_HEADS = 6
A_KV_HEADS = 2
A_GROUP = A_Q_HEADS // A_KV_HEADS
WINDOW = 128
BLOCK = 128
B_HEADS = 6
GRID_W = 64
NB_KH = 8
NB_KW = 16
C_HEADS = 4
C_QK_DIM = 64
C_V_DIM = 128
A_WIDTH = A_Q_HEADS * HEAD_DIM
B_WIDTH = B_HEADS * HEAD_DIM
C_WIDTH = C_HEADS * C_V_DIM
A_KV = A_KV_HEADS * HEAD_DIM
C_QK = C_HEADS * 2 * C_QK_DIM
IN_SIZES = (A_WIDTH, A_KV, A_KV, B_WIDTH, B_WIDTH, B_WIDTH, C_QK, C_QK, C_WIDTH)
IN_WIDTH = sum(IN_SIZES)
REL_BUCKETS = 32
REL_MAX_DIST = 128
PEER_HEADS = 8
PEER_KEYS = 128
PEER_TOPK = 16
PEER_QDIM = 256
PEER_HALF = PEER_QDIM // 2
PEER_PICKS = PEER_HEADS * PEER_TOPK

_PROJ_ORDER = (0, 3, 4, 5, 1, 2, 6, 7, 8)
_COL_AQ, _COL_BQ, _COL_BK, _COL_BV = 0, 1, 2, 3
_COL_AK, _COL_AV = 12, 13
_COL_CQ, _COL_CK, _COL_CV = 7, 8, 9

_VMEM_LIMIT = 56 * 1024 * 1024


def _cparams(sem):
    return pltpu.CompilerParams(dimension_semantics=sem, vmem_limit_bytes=_VMEM_LIMIT)


def _t5_bucket(rel):
    nb = REL_BUCKETS // 2
    max_exact = nb // 2
    ret = jnp.where(rel > 0, nb, 0)
    n = jnp.abs(rel)
    nf = jnp.maximum(n, 1).astype(F32)
    large = max_exact + (jnp.log(nf / max_exact) / math.log(REL_MAX_DIST / max_exact) * (nb - max_exact)).astype(I32)
    large = jnp.minimum(large, nb - 1)
    return ret + jnp.where(n < max_exact, n, large)


def _lookup(table, idx):
    onehot = jax.nn.one_hot(idx, table.shape[0], dtype=F32)
    return jnp.einsum('...b,bc->...c', onehot, table.astype(F32), precision=lax.Precision.HIGHEST)


def _bias_a_table(rel_bias_a):
    rel = np.arange(3 * BLOCK)[None, :] - BLOCK - np.arange(BLOCK)[:, None]
    bias = _lookup(rel_bias_a, _t5_bucket(jnp.asarray(rel)))
    bias = jnp.where(jnp.asarray(np.abs(rel) <= WINDOW)[:, :, None], bias, NEG)
    return bias.transpose(2, 0, 1)


def _bias_b_table(rpb):
    classes = (((0, 1), (0, 0)), ((2, 3), (0, 0)), ((4, 5), (0, 1)), ((5, 6), (1, 1)), ((7, 8), (1, 1)))
    qc = np.arange(GRID_W)[:, None]
    kc = np.arange(GRID_W)[None, :]
    c0 = np.clip(qc - NB_KW // 2, 0, GRID_W - NB_KW)
    in_cols = (kc >= c0) & (kc < c0 + NB_KW)
    n_dc = 2 * NB_KW - 1
    dc_onehot = ((kc - qc + NB_KW - 1)[:, :, None] == np.arange(n_dc)).astype(np.float32)
    blocks = jnp.einsum('qkc,hdc->hdqk', jnp.asarray(dc_onehot), rpb.astype(F32), precision=lax.Precision.HIGHEST)
    blocks = jnp.where(jnp.asarray(in_cols), blocks, NEG)
    neg = jnp.full((rpb.shape[0], GRID_W, GRID_W), NEG, F32)
    tabs = []
    for (rq, r0) in classes:
        rows = []
        for a in range(_B_QROWS):
            parts = [blocks[:, kr - rq[a] + NB_KH - 1] if r0[a] <= kr < r0[a] + NB_KH else neg
                     for kr in range(_B_KROWS)]
            rows.append(jnp.concatenate(parts, axis=-1))
        tabs.append(jnp.concatenate(rows, axis=1))
    return jnp.stack(tabs)


def _bias_c_tables(rel_bias_c):
    q = np.arange(BLOCK)[:, None]
    k = np.arange(BLOCK)[None, :]
    far = _lookup(rel_bias_c, _t5_bucket(jnp.asarray([-2 * BLOCK, 2 * BLOCK])))
    tiles = []
    for d in (-1, 0, 1):
        rel = k + d * BLOCK - q
        t = _lookup(rel_bias_c, _t5_bucket(jnp.asarray(rel))).transpose(2, 0, 1)
        base = far[0] if d < 0 else far[1]
        tiles.append(t - base[:, None, None])
    return jnp.stack(tiles, axis=1), far.T


def _norm_proj_kernel(x_ref, g_ref, w_ref, o_ref, h_ref):
    @pl.when(pl.program_id(1) == 0)
    def _():
        x = x_ref[...]
        ms = jnp.mean(x * x, axis=-1, keepdims=True)
        h_ref[...] = (x * lax.rsqrt(ms + EPS) * g_ref[...]).astype(BF16)

    o_ref[...] = jnp.dot(h_ref[...], w_ref[...], preferred_element_type=F32).astype(o_ref.dtype)


def _norm_proj(x, g, w, *, tm=512, tn=1280):
    n, d = x.shape
    width = w.shape[1]
    return pl.pallas_call(
        _norm_proj_kernel,
        grid=(n // tm, width // tn),
        in_specs=[
            pl.BlockSpec((tm, d), lambda i, j: (i, 0)),
            pl.BlockSpec((1, d), lambda i, j: (0, 0)),
            pl.BlockSpec((d, tn), lambda i, j: (0, j)),
        ],
        out_specs=pl.BlockSpec((tm, tn), lambda i, j: (i, j)),
        out_shape=jax.ShapeDtypeStruct((n, width), BF16),
        scratch_shapes=[pltpu.VMEM((tm, d), BF16)],
        compiler_params=_cparams(("parallel", "arbitrary")),
        name="norm_proj",
    )(x, g.reshape(1, d), w)


def _group_norm_store(o_ref, outs, g_ref):
    o = jnp.concatenate(outs, axis=-1)
    ms = jnp.mean(o * o, axis=-1, keepdims=True)
    o_ref[...] = (o * lax.rsqrt(ms + EPS) * g_ref[...]).astype(o_ref.dtype)


def _mixer_a_kernel(q_ref, kp_ref, kc_ref, kn_ref, vp_ref, vc_ref, vn_ref, bias_ref, sink_ref, g_ref, o_ref):
    i = pl.program_id(1)
    nb = pl.num_programs(1)
    scale = HEAD_DIM ** -0.5
    col = lax.broadcasted_iota(I32, (BLOCK, 3 * BLOCK), 1)
    lo = jnp.where(i == 0, BLOCK, 0)
    hi = jnp.where(i == nb - 1, 2 * BLOCK, 3 * BLOCK)
    in_seq = (col >= lo) & (col < hi)
    ks, vs, scores = [], [], []
    for kv in range(A_KV_HEADS):
        sl = slice(kv * HEAD_DIM, (kv + 1) * HEAD_DIM)
        ks.append(jnp.concatenate([kp_ref[:, sl], kc_ref[:, sl], kn_ref[:, sl]], axis=0))
        vs.append(jnp.concatenate([vp_ref[:, sl], vc_ref[:, sl], vn_ref[:, sl]], axis=0))
    for h in range(A_Q_HEADS):
        q = q_ref[:, h * HEAD_DIM:(h + 1) * HEAD_DIM]
        scores.append(lax.dot_general(q, ks[h // A_GROUP], (((1,), (1,)), ((), ())), preferred_element_type=F32))
    probs, denoms = [], []
    for h in range(A_Q_HEADS):
        s = jnp.where(in_seq, scores[h] * scale + bias_ref[h], NEG)
        sk = sink_ref[h]
        m = jnp.maximum(jnp.max(s, axis=-1, keepdims=True), sk)
        p = jnp.exp(s - m)
        denoms.append(jnp.sum(p, axis=-1, keepdims=True) + jnp.exp(sk - m))
        probs.append(p.astype(BF16))
    outs = [jnp.dot(probs[h], vs[h // A_GROUP], preferred_element_type=F32) / denoms[h] for h in range(A_Q_HEADS)]
    _group_norm_store(o_ref, outs, g_ref)


def _mixer_a(proj, bias, sink, g, *, row0, n_seq, seq):
    nb = seq // BLOCK
    b0 = row0 // BLOCK

    def blk(shift, colblk):
        return lambda b, i: (b0 + b * nb + jnp.clip(i + shift, 0, nb - 1), colblk)

    kv_spec = lambda shift, colblk: pl.BlockSpec((BLOCK, A_KV), blk(shift, colblk))
    return pl.pallas_call(
        _mixer_a_kernel,
        grid=(n_seq, nb),
        in_specs=[
            pl.BlockSpec((BLOCK, A_WIDTH), blk(0, _COL_AQ)),
            kv_spec(-1, _COL_AK), kv_spec(0, _COL_AK), kv_spec(1, _COL_AK),
            kv_spec(-1, _COL_AV), kv_spec(0, _COL_AV), kv_spec(1, _COL_AV),
            pl.BlockSpec((A_Q_HEADS, BLOCK, 3 * BLOCK), lambda b, i: (0, 0, 0)),
            pl.BlockSpec(memory_space=pltpu.SMEM),
            pl.BlockSpec((1, A_WIDTH), lambda b, i: (0, 0)),
        ],
        out_specs=pl.BlockSpec((BLOCK, A_WIDTH), lambda b, i: (b * nb + i, 0)),
        out_shape=jax.ShapeDtypeStruct((n_seq * seq, A_WIDTH), BF16),
        compiler_params=_cparams(("parallel", "parallel")),
        name="mixer_a",
    )(proj, proj, proj, proj, proj, proj, proj, bias, sink, g.reshape(1, A_WIDTH))


_B_QROWS = 2
_B_KROWS = NB_KH + _B_QROWS - 1


def _mixer_b_kernel(q_ref, k_ref, v_ref, bias_ref, g_ref, o_ref, *, rows):
    i = pl.program_id(1)
    scale = HEAD_DIM ** -0.5
    kstart = jnp.clip(_B_QROWS * i - NB_KH // 2, 0, rows - _B_KROWS)
    ks = pl.multiple_of(kstart * GRID_W, GRID_W)
    nk = _B_KROWS * GRID_W
    heads = [slice(h * HEAD_DIM, (h + 1) * HEAD_DIM) for h in range(B_HEADS)]
    scores = [lax.dot_general(q_ref[:, sl], k_ref[pl.ds(ks, nk), sl], (((1,), (1,)), ((), ())),
                              preferred_element_type=F32) for sl in heads]
    probs, denoms = [], []
    for h in range(B_HEADS):
        s = scores[h] * scale + bias_ref[0, h]
        p = jnp.exp(s - jnp.max(s, axis=-1, keepdims=True))
        denoms.append(jnp.sum(p, axis=-1, keepdims=True))
        probs.append(p.astype(BF16))
    outs = [jnp.dot(probs[h], v_ref[pl.ds(ks, nk), sl], preferred_element_type=F32) / denoms[h]
            for h, sl in enumerate(heads)]
    _group_norm_store(o_ref, outs, g_ref)


def _mixer_b(proj, bias, g, *, row0, n_seq, seq):
    rows = seq // GRID_W
    nrp = rows // _B_QROWS
    tq = _B_QROWS * GRID_W
    b0 = row0 // tq
    s0 = row0 // seq

    def cls(i):
        return jnp.where(i == 0, 0, jnp.where(i == 1, 1, jnp.where(i == nrp - 2, 3, jnp.where(i == nrp - 1, 4, 2))))

    return pl.pallas_call(
        functools.partial(_mixer_b_kernel, rows=rows),
        grid=(n_seq, nrp),
        in_specs=[
            pl.BlockSpec((tq, B_WIDTH), lambda b, i: (b0 + b * nrp + i, _COL_BQ)),
            pl.BlockSpec((seq, B_WIDTH), lambda b, i: (s0 + b, _COL_BK)),
            pl.BlockSpec((seq, B_WIDTH), lambda b, i: (s0 + b, _COL_BV)),
            pl.BlockSpec((1, B_HEADS, tq, _B_KROWS * GRID_W), lambda b, i: (cls(i), 0, 0, 0)),
            pl.BlockSpec((1, B_WIDTH), lambda b, i: (0, 0)),
        ],
        out_specs=pl.BlockSpec((tq, B_WIDTH), lambda b, i: (b * nrp + i, 0)),
        out_shape=jax.ShapeDtypeStruct((n_seq * seq, B_WIDTH), BF16),
        compiler_params=_cparams(("parallel", "arbitrary")),
        name="mixer_b",
    )(proj, proj, proj, bias, g.reshape(1, B_WIDTH))


def _mixer_c_kernel(q_ref, k_ref, v_ref, tiles_ref, far_ref, lq1_ref, lk1_ref, lq2_ref, lk2_ref, g_ref, o_ref,
                    *s_refs, seq, lam_init):
    i = pl.program_id(1)
    nb = seq // BLOCK
    scale = C_QK_DIM ** -0.5
    lam = (jnp.exp(jnp.sum(lq1_ref[...] * lk1_ref[...], axis=-1, keepdims=True))
           - jnp.exp(jnp.sum(lq2_ref[...] * lk2_ref[...], axis=-1, keepdims=True)) + lam_init)
    before = lax.broadcasted_iota(I32, (1, seq), 1) < i * BLOCK
    lane = lax.broadcasted_iota(I32, (BLOCK, HEAD_DIM), 1)
    for h in range(C_HEADS):
        sl = slice(h * HEAD_DIM, (h + 1) * HEAD_DIM)
        qh = q_ref[:, sl] * scale
        kh = k_ref[:, sl]
        qs = jnp.concatenate(
            [jnp.where((lane >= c * C_QK_DIM) & (lane < (c + 1) * C_QK_DIM), qh, jnp.zeros_like(qh)) for c in range(2)],
            axis=0)
        s = lax.dot_general(qs, kh, (((1,), (1,)), ((), ())), preferred_element_type=F32)
        for c in range(2):
            hc = 2 * h + c
            s_refs[hc][...] = s[c * BLOCK:(c + 1) * BLOCK] + jnp.where(before, far_ref[hc, 0], far_ref[hc, 1])
    outs = []
    for h in range(C_HEADS):
        sl = slice(h * HEAD_DIM, (h + 1) * HEAD_DIM)
        exps, sums = [], []
        for c in range(2):
            hc = 2 * h + c
            for d in range(3):
                j = i + (d - 1)
                inside = jnp.where((j >= 0) & (j < nb), 1.0, 0.0)
                cols = pl.ds(pl.multiple_of(jnp.clip(j, 0, nb - 1) * BLOCK, BLOCK), BLOCK)
                s_refs[hc][:, cols] += tiles_ref[hc, d] * inside
            s = s_refs[hc][...]
            e = jnp.exp(s - jnp.max(s, axis=-1, keepdims=True))
            exps.append(e)
            sums.append(jnp.sum(e, axis=-1, keepdims=True))
        a = (exps[0] - (lam * sums[0] / sums[1]) * exps[1]).astype(BF16)
        o = jnp.dot(a, v_ref[:, sl], preferred_element_type=F32) / sums[0]
        ms = jnp.mean(o * o, axis=-1, keepdims=True)
        outs.append(o * lax.rsqrt(ms + EPS) * g_ref[...] * (1.0 - lam_init))
    o_ref[...] = jnp.concatenate(outs, axis=-1).astype(o_ref.dtype)


def _mixer_c(proj, tiles, far, lq1, lk1, lq2, lk2, g, *, row0, n_seq, seq, lam_init):
    nb = seq // BLOCK
    b0 = row0 // BLOCK
    s0 = row0 // seq
    vec = lambda a: a.reshape(1, C_QK_DIM)
    vspec = pl.BlockSpec((1, C_QK_DIM), lambda b, i: (0, 0))
    return pl.pallas_call(
        functools.partial(_mixer_c_kernel, seq=seq, lam_init=lam_init),
        grid=(n_seq, nb),
        in_specs=[
            pl.BlockSpec((BLOCK, C_QK), lambda b, i: (b0 + b * nb + i, _COL_CQ)),
            pl.BlockSpec((seq, C_QK), lambda b, i: (s0 + b, _COL_CK)),
            pl.BlockSpec((seq, C_WIDTH), lambda b, i: (s0 + b, _COL_CV)),
            pl.BlockSpec((2 * C_HEADS, 3, BLOCK, BLOCK), lambda b, i: (0, 0, 0, 0)),
            pl.BlockSpec(memory_space=pltpu.SMEM),
            vspec, vspec, vspec, vspec,
            pl.BlockSpec((1, C_V_DIM), lambda b, i: (0, 0)),
        ],
        out_specs=pl.BlockSpec((BLOCK, C_WIDTH), lambda b, i: (b * nb + i, 0)),
        out_shape=jax.ShapeDtypeStruct((n_seq * seq, C_WIDTH), BF16),
        scratch_shapes=[pltpu.VMEM((BLOCK, seq), F32)] * (2 * C_HEADS),
        compiler_params=_cparams(("parallel", "arbitrary")),
        name="mixer_c",
    )(proj, proj, proj, tiles, far, vec(lq1), vec(lk1), vec(lq2), vec(lk2), g.reshape(1, C_V_DIM))


def _out_proj_kernel(x_ref, ya_ref, yb_ref, yc_ref, w_ref, g_ref, x1_ref, h_ref):
    acc = jnp.dot(ya_ref[...], w_ref[0:A_WIDTH, :], preferred_element_type=F32)
    acc += jnp.dot(yb_ref[...], w_ref[A_WIDTH:A_WIDTH + B_WIDTH, :], preferred_element_type=F32)
    acc += jnp.dot(yc_ref[...], w_ref[A_WIDTH + B_WIDTH:, :], preferred_element_type=F32)
    x1 = x_ref[...] + acc
    x1_ref[...] = x1
    ms = jnp.mean(x1 * x1, axis=-1, keepdims=True)
    h_ref[...] = x1 * lax.rsqrt(ms + EPS) * g_ref[...]


def _out_proj(x, ya, yb, yc, w, g, *, tm=256):
    n, d = x.shape
    row = lambda width: pl.BlockSpec((tm, width), lambda i: (i, 0))
    return pl.pallas_call(
        _out_proj_kernel,
        grid=(n // tm,),
        in_specs=[row(d), row(A_WIDTH), row(B_WIDTH), row(C_WIDTH),
                  pl.BlockSpec((d, d), lambda i: (0, 0)),
                  pl.BlockSpec((1, d), lambda i: (0, 0))],
        out_specs=[row(d), row(d)],
        out_shape=[jax.ShapeDtypeStruct((n, d), F32), jax.ShapeDtypeStruct((n, d), F32)],
        compiler_params=_cparams(("parallel",)),
        name="out_proj",
    )(x, ya, yb, yc, w, g.reshape(1, d))


def _top16(x):
    r = float(x.shape[0])
    iota = lax.broadcasted_iota(I32, x.shape, 0).astype(F32)
    vals, idxs = [], []
    for _ in range(PEER_TOPK):
        m = jnp.max(x, axis=0, keepdims=True)
        idx = jnp.min(jnp.where(x == m, iota, r), axis=0, keepdims=True)
        vals.append(m)
        idxs.append(idx)
        x = jnp.where(iota == idx, -jnp.inf, x)
    return vals, idxs


def _peer_candidates(s1, i1, s2, i2):
    sub = 8
    s1a, i1a = jnp.concatenate(s1, axis=0), jnp.concatenate(i1, axis=0)
    s2a, i2a = jnp.concatenate(s2, axis=0), jnp.concatenate(i2, axis=0)
    row = lax.broadcasted_iota(I32, s2a[:sub].shape, 0)
    cand, cidx = [], []
    first_single = PEER_TOPK // 2
    for a in range(first_single):
        nb = PEER_TOPK // (a + 1)
        if nb > sub:
            cand.append(s1[a] + s2a)
            cidx.append(i1[a] * PEER_KEYS + i2a)
        else:
            cand.append(jnp.where(row < nb, s1[a] + s2a[:sub], -jnp.inf))
            cidx.append(i1[a] * PEER_KEYS + i2a[:sub])
    cand.append(s1a[first_single:] + s2[0])
    cidx.append(i1a[first_single:] * PEER_KEYS + i2[0])
    return jnp.concatenate(cand, axis=0), jnp.concatenate(cidx, axis=0)


def _peer_route_kernel(h_ref, wq_ref, keys_ref, eidx_ref, gate_ref):
    q = jnp.dot(h_ref[...].astype(BF16), wq_ref[...], preferred_element_type=F32).astype(BF16)
    tm = q.shape[0]
    for hp in range(PEER_HEADS):
        tops = []
        for c in range(2):
            hc = 2 * hp + c
            qc = q[:, hc * PEER_HALF:(hc + 1) * PEER_HALF]
            sc = lax.dot_general(keys_ref[hc], qc, (((1,), (1,)), ((), ())), preferred_element_type=F32)
            tops.append(_top16(sc))
        (s1, i1), (s2, i2) = tops
        cand, cidx = _peer_candidates(s1, i1, s2, i2)
        top_s, pos = _top16(cand)
        iota = lax.broadcasted_iota(I32, cand.shape, 0).astype(F32)
        eidx = [jnp.sum(jnp.where(iota == p, cidx, 0.0), axis=0, keepdims=True) for p in pos]
        ts = jnp.concatenate(top_s, axis=0)
        e = jnp.exp(ts - top_s[0])
        gate_ref[hp] = e / jnp.sum(e, axis=0, keepdims=True)
        eidx_ref[hp] = jnp.concatenate(eidx, axis=0).astype(I32)


def _peer_route(h, wq, keys, *, tm=256):
    n, d = h.shape
    out_spec = pl.BlockSpec((PEER_HEADS, PEER_TOPK, tm), lambda i: (0, 0, i))
    return pl.pallas_call(
        _peer_route_kernel,
        grid=(n // tm,),
        in_specs=[pl.BlockSpec((tm, d), lambda i: (i, 0)),
                  pl.BlockSpec((d, PEER_HEADS * PEER_QDIM), lambda i: (0, 0)),
                  pl.BlockSpec((2 * PEER_HEADS, PEER_KEYS, PEER_HALF), lambda i: (0, 0, 0))],
        out_specs=[out_spec, out_spec],
        out_shape=[jax.ShapeDtypeStruct((PEER_HEADS, PEER_TOPK, n), I32),
                   jax.ShapeDtypeStruct((PEER_HEADS, PEER_TOPK, n), F32)],
        compiler_params=_cparams(("parallel",)),
        name="peer_route",
    )(h, wq, keys)


def _gelu_exact(x):
    return 0.5 * x * (1.0 + lax.erf(x * (2.0 ** -0.5)))


_PEER_T = 8
_DMA_QUEUES = 2
_PEER_ROWS = _PEER_T * PEER_PICKS


def _pack_expert_tables(u, v):
    half = u.shape[1] // 2
    bits = lambda a: lax.bitcast_convert_type(a.astype(BF16), jnp.uint16).astype(jnp.uint32)
    lo = jnp.concatenate([bits(u[:, :half]), bits(v[:, :half])], axis=1)
    hi = jnp.concatenate([bits(u[:, half:]), bits(v[:, half:])], axis=1)
    return lax.bitcast_convert_type(lo | (hi << 16), I32)


def _peer_expert_kernel(idx_hbm, gate_ref, h_ref, x_ref, gf_ref, uv_hbm, o_ref,
                        idx_smem, buf0, buf1, a_ref, idx_sem, row_sem, *, final_norm):
    i = pl.program_id(0)
    n = pl.num_programs(0)
    bufs = (buf0, buf1)
    half = h_ref.shape[1] // 2

    def idx_copy(step, slot):
        dst = idx_smem.at[pl.ds(slot * _PEER_ROWS, _PEER_ROWS)]
        return pltpu.make_async_copy(idx_hbm.at[step], dst, idx_sem.at[slot])

    def issue_token(slot, t):
        for k in range(PEER_PICKS):
            off = pl.multiple_of(idx_smem[slot * _PEER_ROWS + t * PEER_PICKS + k], D_MODEL)
            copy = pltpu.make_async_copy(uv_hbm.at[pl.ds(off, D_MODEL)], bufs[slot].at[t, k], row_sem.at[slot])
            copy.start(priority=k % _DMA_QUEUES)

    def wait_rows(slot):
        pltpu.make_async_copy(bufs[slot], bufs[slot], row_sem.at[slot]).wait()

    @pl.when(i == 0)
    def _():
        idx_copy(0, 0).start()
        idx_copy(0, 0).wait()
        lax.fori_loop(0, _PEER_T, lambda t, c: (issue_token(0, t), c)[1], 0)
        idx_copy(1, 1).start()

    def step(cur, nxt):
        idx_copy(i + 1, nxt).wait()
        idx_copy(i + 2, cur).start()
        wait_rows(cur)
        buf = bufs[cur]

        h = h_ref[...]
        hh = jnp.concatenate([h[:, :half], h[:, half:]], axis=0).astype(BF16)

        @pl.when(i >= 0)
        def _():
            for t in range(_PEER_T // 2):
                issue_token(nxt, t)
            tok = lax.broadcasted_iota(I32, (_PEER_T, 2 * PEER_PICKS), 0)
            a = jnp.zeros((_PEER_T, 2 * PEER_PICKS), F32)
            for t in range(_PEER_T):
                urows = pltpu.bitcast(buf[t, :, :half], BF16)
                r = lax.dot_general(hh, urows, (((1,), (1,)), ((), ())), preferred_element_type=F32)
                r = r[:_PEER_T] + pltpu.roll(r[_PEER_T:], 2 * PEER_PICKS - 1, axis=1)
                a = jnp.where(tok == t, r, a)
            a_ref[...] = a

        for t in range(_PEER_T // 2, _PEER_T):
            issue_token(nxt, t)
        w_even = gate_ref[...] * _gelu_exact(a_ref[...])
        w = jnp.concatenate([w_even, pltpu.roll(w_even, 1, axis=1)], axis=0).astype(BF16)
        tok = lax.broadcasted_iota(I32, (_PEER_T, half), 0)
        lo = jnp.zeros((_PEER_T, half), F32)
        hi = jnp.zeros((_PEER_T, half), F32)
        for t in range(_PEER_T):
            vrows = pltpu.bitcast(buf[t, :, half:], BF16)
            o = jnp.dot(w, vrows, preferred_element_type=F32)
            lo = jnp.where(tok == t, o[:_PEER_T], lo)
            hi = jnp.where(tok == t, o[_PEER_T:], hi)
        y = x_ref[...] + jnp.concatenate([lo, hi], axis=1)
        if final_norm:
            ms = jnp.mean(y * y, axis=-1, keepdims=True)
            y = y * lax.rsqrt(ms + EPS) * gf_ref[...]
        o_ref[...] = y

        @pl.when(i == n - 1)
        def _():
            wait_rows(nxt)
            idx_copy(i + 2, cur).wait()

    for parity in range(2):
        pl.when(i % 2 == parity)(functools.partial(step, parity, 1 - parity))


def _peer_experts(eidx, gate, h, x, uv, g_final, *, final_norm):
    n, d = x.shape
    steps = n // _PEER_T
    row = lambda width: pl.BlockSpec((_PEER_T, width), lambda i: (i, 0))
    gate2 = jnp.stack([gate, jnp.zeros_like(gate)], axis=-1).reshape(n, 2 * PEER_PICKS)
    return pl.pallas_call(
        functools.partial(_peer_expert_kernel, final_norm=final_norm),
        grid=(steps,),
        in_specs=[pl.BlockSpec(memory_space=pl.ANY), row(2 * PEER_PICKS), row(d), row(d),
                  pl.BlockSpec((1, d), lambda i: (0, 0)),
                  pl.BlockSpec(memory_space=pl.ANY)],
        out_specs=row(d),
        out_shape=jax.ShapeDtypeStruct((n, d), F32),
        scratch_shapes=[pltpu.SMEM((2 * _PEER_ROWS,), I32),
                        pltpu.VMEM((_PEER_T, PEER_PICKS, d), I32),
                        pltpu.VMEM((_PEER_T, PEER_PICKS, d), I32),
                        pltpu.VMEM((_PEER_T, 2 * PEER_PICKS), F32),
                        pltpu.SemaphoreType.DMA((2,)),
                        pltpu.SemaphoreType.DMA((2,))],
        compiler_params=_cparams(("arbitrary",)),
        name="peer_experts",
    )(jnp.pad((eidx * d).reshape(steps, _PEER_ROWS), ((0, 2), (0, 0))), gate2, h, x, g_final.reshape(1, d),
      uv.reshape(-1))


def _encoder(xs, rel_bias, g_attn, w_in, a_sink, g_grp_a, nat_rpb, g_grp_b, lam_q1, lam_k1, lam_q2, lam_k2,
             g_subln, w_out, g_ffn, peer_wq, peer_keys, peer_u, peer_v, g_final):
    depth = w_in.shape[0]
    d = xs[0].shape[-1]
    splits = [int(c) for c in np.cumsum((0,) + IN_SIZES)]
    bias_a = _bias_a_table(rel_bias[:, :A_Q_HEADS])
    tiles_c, far_c = _bias_c_tables(rel_bias[:, A_Q_HEADS:])
    shapes = [x.shape[:2] for x in xs]
    xs = [x.reshape(-1, d) for x in xs]
    for l in range(depth):
        lam_init = 0.8 - 0.6 * math.exp(-0.3 * l)
        w_l = jnp.concatenate([w_in[l][:, splits[s]:splits[s + 1]] for s in _PROJ_ORDER], axis=1).astype(BF16)
        w_o = w_out[l].astype(BF16)
        w_q = peer_wq[l].astype(BF16)
        bias_b = _bias_b_table(nat_rpb[l])
        keys = peer_keys[l].reshape(2 * PEER_HEADS, PEER_KEYS, PEER_HALF).astype(BF16)
        uv = _pack_expert_tables(peer_u[l], peer_v[l])
        for b, (n_seq, seq) in enumerate(shapes):
            x = xs[b]
            n = x.shape[0]
            kw = dict(row0=0, n_seq=n_seq, seq=seq)
            proj = _norm_proj(x, g_attn[l], w_l)
            ya = _mixer_a(proj, bias_a, a_sink[l], g_grp_a[l], **kw)
            yb = _mixer_b(proj, bias_b, g_grp_b[l], **kw)
            yc = _mixer_c(proj, tiles_c, far_c, lam_q1[l], lam_k1[l], lam_q2[l], lam_k2[l], g_subln[l],
                          lam_init=lam_init, **kw)
            x1, h2 = _out_proj(x, ya, yb, yc, w_o, g_ffn[l])
            eidx, gate = _peer_route(h2, w_q, keys)
            eidx = eidx.reshape(PEER_PICKS, n).T
            gate = gate.reshape(PEER_PICKS, n).T
            xs[b] = _peer_experts(eidx, gate, h2, x1, uv, g_final, final_norm=(l == depth - 1))
    return tuple(x.reshape(n_seq, seq, d) for x, (n_seq, seq) in zip(xs, shapes))


def kernel(x_prompt, x_sample, rel_bias, g_attn, w_in, a_sink, g_grp_a, nat_rpb, g_grp_b, lam_q1, lam_k1, lam_q2, lam_k2, g_subln, w_out, g_ffn, peer_wq, peer_keys, peer_u, peer_v, g_final):
    return _encoder([x_prompt, x_sample], rel_bias, g_attn, w_in, a_sink, g_grp_a, nat_rpb, g_grp_b, lam_q1, lam_k1,
                    lam_q2, lam_k2, g_subln, w_out, g_ffn, peer_wq, peer_keys, peer_u, peer_v, g_final)
```

```python
import functools
import math

import numpy as np
import jax
import jax.numpy as jnp
from jax import lax
from jax.experimental import pallas as pl
from jax.experimental.pallas import tpu as pltpu

F32 = jnp.float32
BF16 = jnp.bfloat16
I32 = jnp.int32
EPS = 1e-6
NEG = -1e30

D_MODEL = 2048
DEPTH = 2
HEAD_DIM = 128
A_Q_HEADS = 6
A_KV_HEADS = 2
A_GROUP = A_Q_HEADS // A_KV_HEADS
WINDOW = 128
BLOCK = 128
B_HEADS = 6
GRID_W = 64
NB_KH = 8
NB_KW = 16
C_HEADS = 4
C_QK_DIM = 64
C_V_DIM = 128
A_WIDTH = A_Q_HEADS * HEAD_DIM
B_WIDTH = B_HEADS * HEAD_DIM
C_WIDTH = C_HEADS * C_V_DIM
A_KV = A_KV_HEADS * HEAD_DIM
C_QK = C_HEADS * 2 * C_QK_DIM
IN_SIZES = (A_WIDTH, A_KV, A_KV, B_WIDTH, B_WIDTH, B_WIDTH, C_QK, C_QK, C_WIDTH)
IN_WIDTH = sum(IN_SIZES)
REL_BUCKETS = 32
REL_MAX_DIST = 128
PEER_HEADS = 8
PEER_KEYS = 128
PEER_TOPK = 16
PEER_QDIM = 256
PEER_HALF = PEER_QDIM // 2
PEER_PICKS = PEER_HEADS * PEER_TOPK

_PROJ_ORDER = (0, 3, 4, 5, 1, 2, 6, 7, 8)
_COL_AQ, _COL_BQ, _COL_BK, _COL_BV = 0, 1, 2, 3
_COL_AK, _COL_AV = 12, 13
_COL_CQ, _COL_CK, _COL_CV = 7, 8, 9

_VMEM_LIMIT = 56 * 1024 * 1024


def _cparams(sem):
    return pltpu.CompilerParams(dimension_semantics=sem, vmem_limit_bytes=_VMEM_LIMIT)


def _t5_bucket(rel):
    nb = REL_BUCKETS // 2
    max_exact = nb // 2
    ret = jnp.where(rel > 0, nb, 0)
    n = jnp.abs(rel)
    nf = jnp.maximum(n, 1).astype(F32)
    large = max_exact + (jnp.log(nf / max_exact) / math.log(REL_MAX_DIST / max_exact) * (nb - max_exact)).astype(I32)
    large = jnp.minimum(large, nb - 1)
    return ret + jnp.where(n < max_exact, n, large)


def _lookup(table, idx):
    onehot = jax.nn.one_hot(idx, table.shape[0], dtype=F32)
    return jnp.einsum('...b,bc->...c', onehot, table.astype(F32), precision=lax.Precision.HIGHEST)


def _bias_a_table(rel_bias_a):
    rel = np.arange(3 * BLOCK)[None, :] - BLOCK - np.arange(BLOCK)[:, None]
    bias = _lookup(rel_bias_a, _t5_bucket(jnp.asarray(rel)))
    bias = jnp.where(jnp.asarray(np.abs(rel) <= WINDOW)[:, :, None], bias, NEG)
    return bias.transpose(2, 0, 1)


def _bias_b_table(rpb):
    classes = (((0, 1), (0, 0)), ((2, 3), (0, 0)), ((4, 5), (0, 1)), ((5, 6), (1, 1)), ((7, 8), (1, 1)))
    qc = np.arange(GRID_W)[:, None]
    kc = np.arange(GRID_W)[None, :]
    c0 = np.clip(qc - NB_KW // 2, 0, GRID_W - NB_KW)
    in_cols = (kc >= c0) & (kc < c0 + NB_KW)
    n_dc = 2 * NB_KW - 1
    dc_onehot = ((kc - qc + NB_KW - 1)[:, :, None] == np.arange(n_dc)).astype(np.float32)
    blocks = jnp.einsum('qkc,hdc->hdqk', jnp.asarray(dc_onehot), rpb.astype(F32), precision=lax.Precision.HIGHEST)
    blocks = jnp.where(jnp.asarray(in_cols), blocks, NEG)
    neg = jnp.full((rpb.shape[0], GRID_W, GRID_W), NEG, F32)
    tabs = []
    for (rq, r0) in classes:
        rows = []
        for a in range(_B_QROWS):
            parts = [blocks[:, kr - rq[a] + NB_KH - 1] if r0[a] <= kr < r0[a] + NB_KH else neg
                     for kr in range(_B_KROWS)]
            rows.append(jnp.concatenate(parts, axis=-1))
        tabs.append(jnp.concatenate(rows, axis=1))
    return jnp.stack(tabs)


def _bias_c_tables(rel_bias_c):
    q = np.arange(BLOCK)[:, None]
    k = np.arange(BLOCK)[None, :]
    far = _lookup(rel_bias_c, _t5_bucket(jnp.asarray([-2 * BLOCK, 2 * BLOCK])))
    tiles = []
    for d in (-1, 0, 1):
        rel = k + d * BLOCK - q
        t = _lookup(rel_bias_c, _t5_bucket(jnp.asarray(rel))).transpose(2, 0, 1)
        base = far[0] if d < 0 else far[1]
        tiles.append(t - base[:, None, None])
    return jnp.stack(tiles, axis=1), far.T


def _norm_proj_kernel(x_ref, g_ref, w_ref, o_ref, h_ref):
    @pl.when(pl.program_id(1) == 0)
    def _():
        x = x_ref[...]
        ms = jnp.mean(x * x, axis=-1, keepdims=True)
        h_ref[...] = (x * lax.rsqrt(ms + EPS) * g_ref[...]).astype(BF16)

    o_ref[...] = jnp.dot(h_ref[...], w_ref[...], preferred_element_type=F32).astype(o_ref.dtype)


def _norm_proj(x, g, w, *, tm=512, tn=1280):
    n, d = x.shape
    width = w.shape[1]
    return pl.pallas_call(
        _norm_proj_kernel,
        grid=(n // tm, width // tn),
        in_specs=[
            pl.BlockSpec((tm, d), lambda i, j: (i, 0)),
            pl.BlockSpec((1, d), lambda i, j: (0, 0)),
            pl.BlockSpec((d, tn), lambda i, j: (0, j)),
        ],
        out_specs=pl.BlockSpec((tm, tn), lambda i, j: (i, j)),
        out_shape=jax.ShapeDtypeStruct((n, width), BF16),
        scratch_shapes=[pltpu.VMEM((tm, d), BF16)],
        compiler_params=_cparams(("parallel", "arbitrary")),
        name="norm_proj",
    )(x, g.reshape(1, d), w)


def _group_norm_store(o_ref, outs, g_ref):
    o = jnp.concatenate(outs, axis=-1)
    ms = jnp.mean(o * o, axis=-1, keepdims=True)
    o_ref[...] = (o * lax.rsqrt(ms + EPS) * g_ref[...]).astype(o_ref.dtype)


def _mixer_a_kernel(q_ref, kp_ref, kc_ref, kn_ref, vp_ref, vc_ref, vn_ref, bias_ref, sink_ref, g_ref, o_ref):
    i = pl.program_id(1)
    nb = pl.num_programs(1)
    scale = HEAD_DIM ** -0.5
    col = lax.broadcasted_iota(I32, (BLOCK, 3 * BLOCK), 1)
    lo = jnp.where(i == 0, BLOCK, 0)
    hi = jnp.where(i == nb - 1, 2 * BLOCK, 3 * BLOCK)
    in_seq = (col >= lo) & (col < hi)
    ks, vs, scores = [], [], []
    for kv in range(A_KV_HEADS):
        sl = slice(kv * HEAD_DIM, (kv + 1) * HEAD_DIM)
        ks.append(jnp.concatenate([kp_ref[:, sl], kc_ref[:, sl], kn_ref[:, sl]], axis=0))
        vs.append(jnp.concatenate([vp_ref[:, sl], vc_ref[:, sl], vn_ref[:, sl]], axis=0))
    for h in range(A_Q_HEADS):
        q = q_ref[:, h * HEAD_DIM:(h + 1) * HEAD_DIM]
        scores.append(lax.dot_general(q, ks[h // A_GROUP], (((1,), (1,)), ((), ())), preferred_element_type=F32))
    probs, denoms = [], []
    for h in range(A_Q_HEADS):
        s = jnp.where(in_seq, scores[h] * scale + bias_ref[h], NEG)
        sk = sink_ref[h]
        m = jnp.maximum(jnp.max(s, axis=-1, keepdims=True), sk)
        p = jnp.exp(s - m)
        denoms.append(jnp.sum(p, axis=-1, keepdims=True) + jnp.exp(sk - m))
        probs.append(p.astype(BF16))
    outs = [jnp.dot(probs[h], vs[h // A_GROUP], preferred_element_type=F32) / denoms[h] for h in range(A_Q_HEADS)]
    _group_norm_store(o_ref, outs, g_ref)


def _mixer_a(proj, bias, sink, g, *, row0, n_seq, seq):
    nb = seq // BLOCK
    b0 = row0 // BLOCK

    def blk(shift, colblk):
        return lambda b, i: (b0 + b * nb + jnp.clip(i + shift, 0, nb - 1), colblk)

    kv_spec = lambda shift, colblk: pl.BlockSpec((BLOCK, A_KV), blk(shift, colblk))
    return pl.pallas_call(
        _mixer_a_kernel,
        grid=(n_seq, nb),
        in_specs=[
            pl.BlockSpec((BLOCK, A_WIDTH), blk(0, _COL_AQ)),
            kv_spec(-1, _COL_AK), kv_spec(0, _COL_AK), kv_spec(1, _COL_AK),
            kv_spec(-1, _COL_AV), kv_spec(0, _COL_AV), kv_spec(1, _COL_AV),
            pl.BlockSpec((A_Q_HEADS, BLOCK, 3 * BLOCK), lambda b, i: (0, 0, 0)),
            pl.BlockSpec(memory_space=pltpu.SMEM),
            pl.BlockSpec((1, A_WIDTH), lambda b, i: (0, 0)),
        ],
        out_specs=pl.BlockSpec((BLOCK, A_WIDTH), lambda b, i: (b * nb + i, 0)),
        out_shape=jax.ShapeDtypeStruct((n_seq * seq, A_WIDTH), BF16),
        compiler_params=_cparams(("parallel", "parallel")),
        name="mixer_a",
    )(proj, proj, proj, proj, proj, proj, proj, bias, sink, g.reshape(1, A_WIDTH))


_B_QROWS = 2
_B_KROWS = NB_KH + _B_QROWS - 1


def _mixer_b_kernel(q_ref, k_ref, v_ref, bias_ref, g_ref, o_ref, *, rows):
    i = pl.program_id(1)
    scale = HEAD_DIM ** -0.5
    kstart = jnp.clip(_B_QROWS * i - NB_KH // 2, 0, rows - _B_KROWS)
    ks = pl.multiple_of(kstart * GRID_W, GRID_W)
    nk = _B_KROWS * GRID_W
    heads = [slice(h * HEAD_DIM, (h + 1) * HEAD_DIM) for h in range(B_HEADS)]
    scores = [lax.dot_general(q_ref[:, sl], k_ref[pl.ds(ks, nk), sl], (((1,), (1,)), ((), ())),
                              preferred_element_type=F32) for sl in heads]
    probs, denoms = [], []
    for h in range(B_HEADS):
        s = scores[h] * scale + bias_ref[0, h]
        p = jnp.exp(s - jnp.max(s, axis=-1, keepdims=True))
        denoms.append(jnp.sum(p, axis=-1, keepdims=True))
        probs.append(p.astype(BF16))
    outs = [jnp.dot(probs[h], v_ref[pl.ds(ks, nk), sl], preferred_element_type=F32) / denoms[h]
            for h, sl in enumerate(heads)]
    _group_norm_store(o_ref, outs, g_ref)


def _mixer_b(proj, bias, g, *, row0, n_seq, seq):
    rows = seq // GRID_W
    nrp = rows // _B_QROWS
    tq = _B_QROWS * GRID_W
    b0 = row0 // tq
    s0 = row0 // seq

    def cls(i):
        return jnp.where(i == 0, 0, jnp.where(i == 1, 1, jnp.where(i == nrp - 2, 3, jnp.where(i == nrp - 1, 4, 2))))

    return pl.pallas_call(
        functools.partial(_mixer_b_kernel, rows=rows),
        grid=(n_seq, nrp),
        in_specs=[
            pl.BlockSpec((tq, B_WIDTH), lambda b, i: (b0 + b * nrp + i, _COL_BQ)),
            pl.BlockSpec((seq, B_WIDTH), lambda b, i: (s0 + b, _COL_BK)),
            pl.BlockSpec((seq, B_WIDTH), lambda b, i: (s0 + b, _COL_BV)),
            pl.BlockSpec((1, B_HEADS, tq, _B_KROWS * GRID_W), lambda b, i: (cls(i), 0, 0, 0)),
            pl.BlockSpec((1, B_WIDTH), lambda b, i: (0, 0)),
        ],
        out_specs=pl.BlockSpec((tq, B_WIDTH), lambda b, i: (b * nrp + i, 0)),
        out_shape=jax.ShapeDtypeStruct((n_seq * seq, B_WIDTH), BF16),
        compiler_params=_cparams(("parallel", "arbitrary")),
        name="mixer_b",
    )(proj, proj, proj, bias, g.reshape(1, B_WIDTH))


def _mixer_c_kernel(q_ref, k_ref, v_ref, tiles_ref, far_ref, lq1_ref, lk1_ref, lq2_ref, lk2_ref, g_ref, o_ref,
                    *s_refs, seq, lam_init):
    i = pl.program_id(1)
    nb = seq // BLOCK
    scale = C_QK_DIM ** -0.5
    lam = (jnp.exp(jnp.sum(lq1_ref[...] * lk1_ref[...], axis=-1, keepdims=True))
           - jnp.exp(jnp.sum(lq2_ref[...] * lk2_ref[...], axis=-1, keepdims=True)) + lam_init)
    before = lax.broadcasted_iota(I32, (1, seq), 1) < i * BLOCK
    lane = lax.broadcasted_iota(I32, (BLOCK, HEAD_DIM), 1)
    for h in range(C_HEADS):
        sl = slice(h * HEAD_DIM, (h + 1) * HEAD_DIM)
        qh = q_ref[:, sl] * scale
        kh = k_ref[:, sl]
        qs = jnp.concatenate(
            [jnp.where((lane >= c * C_QK_DIM) & (lane < (c + 1) * C_QK_DIM), qh, jnp.zeros_like(qh)) for c in range(2)],
            axis=0)
        s = lax.dot_general(qs, kh, (((1,), (1,)), ((), ())), preferred_element_type=F32)
        for c in range(2):
            hc = 2 * h + c
            s_refs[hc][...] = s[c * BLOCK:(c + 1) * BLOCK] + jnp.where(before, far_ref[hc, 0], far_ref[hc, 1])
    outs = []
    for h in range(C_HEADS):
        sl = slice(h * HEAD_DIM, (h + 1) * HEAD_DIM)
        exps, sums = [], []
        for c in range(2):
            hc = 2 * h + c
            for d in range(3):
                j = i + (d - 1)
                inside = jnp.where((j >= 0) & (j < nb), 1.0, 0.0)
                cols = pl.ds(pl.multiple_of(jnp.clip(j, 0, nb - 1) * BLOCK, BLOCK), BLOCK)
                s_refs[hc][:, cols] += tiles_ref[hc, d] * inside
            s = s_refs[hc][...]
            e = jnp.exp(s - jnp.max(s, axis=-1, keepdims=True))
            exps.append(e)
            sums.append(jnp.sum(e, axis=-1, keepdims=True))
        a = (exps[0] - (lam * sums[0] / sums[1]) * exps[1]).astype(BF16)
        o = jnp.dot(a, v_ref[:, sl], preferred_element_type=F32) / sums[0]
        ms = jnp.mean(o * o, axis=-1, keepdims=True)
        outs.append(o * lax.rsqrt(ms + EPS) * g_ref[...] * (1.0 - lam_init))
    o_ref[...] = jnp.concatenate(outs, axis=-1).astype(o_ref.dtype)


def _mixer_c(proj, tiles, far, lq1, lk1, lq2, lk2, g, *, row0, n_seq, seq, lam_init):
    nb = seq // BLOCK
    b0 = row0 // BLOCK
    s0 = row0 // seq
    vec = lambda a: a.reshape(1, C_QK_DIM)
    vspec = pl.BlockSpec((1, C_QK_DIM), lambda b, i: (0, 0))
    return pl.pallas_call(
        functools.partial(_mixer_c_kernel, seq=seq, lam_init=lam_init),
        grid=(n_seq, nb),
        in_specs=[
            pl.BlockSpec((BLOCK, C_QK), lambda b, i: (b0 + b * nb + i, _COL_CQ)),
            pl.BlockSpec((seq, C_QK), lambda b, i: (s0 + b, _COL_CK)),
            pl.BlockSpec((seq, C_WIDTH), lambda b, i: (s0 + b, _COL_CV)),
            pl.BlockSpec((2 * C_HEADS, 3, BLOCK, BLOCK), lambda b, i: (0, 0, 0, 0)),
            pl.BlockSpec(memory_space=pltpu.SMEM),
            vspec, vspec, vspec, vspec,
            pl.BlockSpec((1, C_V_DIM), lambda b, i: (0, 0)),
        ],
        out_specs=pl.BlockSpec((BLOCK, C_WIDTH), lambda b, i: (b * nb + i, 0)),
        out_shape=jax.ShapeDtypeStruct((n_seq * seq, C_WIDTH), BF16),
        scratch_shapes=[pltpu.VMEM((BLOCK, seq), F32)] * (2 * C_HEADS),
        compiler_params=_cparams(("parallel", "arbitrary")),
        name="mixer_c",
    )(proj, proj, proj, tiles, far, vec(lq1), vec(lk1), vec(lq2), vec(lk2), g.reshape(1, C_V_DIM))


def _out_proj_kernel(x_ref, ya_ref, yb_ref, yc_ref, w_ref, g_ref, x1_ref, h_ref):
    acc = jnp.dot(ya_ref[...], w_ref[0:A_WIDTH, :], preferred_element_type=F32)
    acc += jnp.dot(yb_ref[...], w_ref[A_WIDTH:A_WIDTH + B_WIDTH, :], preferred_element_type=F32)
    acc += jnp.dot(yc_ref[...], w_ref[A_WIDTH + B_WIDTH:, :], preferred_element_type=F32)
    x1 = x_ref[...] + acc
    x1_ref[...] = x1
    ms = jnp.mean(x1 * x1, axis=-1, keepdims=True)
    h_ref[...] = x1 * lax.rsqrt(ms + EPS) * g_ref[...]


def _out_proj(x, ya, yb, yc, w, g, *, tm=256):
    n, d = x.shape
    row = lambda width: pl.BlockSpec((tm, width), lambda i: (i, 0))
    return pl.pallas_call(
        _out_proj_kernel,
        grid=(n // tm,),
        in_specs=[row(d), row(A_WIDTH), row(B_WIDTH), row(C_WIDTH),
                  pl.BlockSpec((d, d), lambda i: (0, 0)),
                  pl.BlockSpec((1, d), lambda i: (0, 0))],
        out_specs=[row(d), row(d)],
        out_shape=[jax.ShapeDtypeStruct((n, d), F32), jax.ShapeDtypeStruct((n, d), F32)],
        compiler_params=_cparams(("parallel",)),
        name="out_proj",
    )(x, ya, yb, yc, w, g.reshape(1, d))


def _top16(x):
    r = float(x.shape[0])
    iota = lax.broadcasted_iota(I32, x.shape, 0).astype(F32)
    vals, idxs = [], []
    for _ in range(PEER_TOPK):
        m = jnp.max(x, axis=0, keepdims=True)
        idx = jnp.min(jnp.where(x == m, iota, r), axis=0, keepdims=True)
        vals.append(m)
        idxs.append(idx)
        x = jnp.where(iota == idx, -jnp.inf, x)
    return vals, idxs


def _peer_candidates(s1, i1, s2, i2):
    sub = 8
    s1a, i1a = jnp.concatenate(s1, axis=0), jnp.concatenate(i1, axis=0)
    s2a, i2a = jnp.concatenate(s2, axis=0), jnp.concatenate(i2, axis=0)
    row = lax.broadcasted_iota(I32, s2a[:sub].shape, 0)
    cand, cidx = [], []
    first_single = PEER_TOPK // 2
    for a in range(first_single):
        nb = PEER_TOPK // (a + 1)
        if nb > sub:
            cand.append(s1[a] + s2a)
            cidx.append(i1[a] * PEER_KEYS + i2a)
        else:
            cand.append(jnp.where(row < nb, s1[a] + s2a[:sub], -jnp.inf))
            cidx.append(i1[a] * PEER_KEYS + i2a[:sub])
    cand.append(s1a[first_single:] + s2[0])
    cidx.append(i1a[first_single:] * PEER_KEYS + i2[0])
    return jnp.concatenate(cand, axis=0), jnp.concatenate(cidx, axis=0)


def _peer_route_kernel(h_ref, wq_ref, keys_ref, eidx_ref, gate_ref):
    q = jnp.dot(h_ref[...].astype(BF16), wq_ref[...], preferred_element_type=F32).astype(BF16)
    tm = q.shape[0]
    for hp in range(PEER_HEADS):
        tops = []
        for c in range(2):
            hc = 2 * hp + c
            qc = q[:, hc * PEER_HALF:(hc + 1) * PEER_HALF]
            sc = lax.dot_general(keys_ref[hc], qc, (((1,), (1,)), ((), ())), preferred_element_type=F32)
            tops.append(_top16(sc))
        (s1, i1), (s2, i2) = tops
        cand, cidx = _peer_candidates(s1, i1, s2, i2)
        top_s, pos = _top16(cand)
        iota = lax.broadcasted_iota(I32, cand.shape, 0).astype(F32)
        eidx = [jnp.sum(jnp.where(iota == p, cidx, 0.0), axis=0, keepdims=True) for p in pos]
        ts = jnp.concatenate(top_s, axis=0)
        e = jnp.exp(ts - top_s[0])
        gate_ref[hp] = e / jnp.sum(e, axis=0, keepdims=True)
        eidx_ref[hp] = jnp.concatenate(eidx, axis=0).astype(I32)


def _peer_route(h, wq, keys, *, tm=256):
    n, d = h.shape
    out_spec = pl.BlockSpec((PEER_HEADS, PEER_TOPK, tm), lambda i: (0, 0, i))
    return pl.pallas_call(
        _peer_route_kernel,
        grid=(n // tm,),
        in_specs=[pl.BlockSpec((tm, d), lambda i: (i, 0)),
                  pl.BlockSpec((d, PEER_HEADS * PEER_QDIM), lambda i: (0, 0)),
                  pl.BlockSpec((2 * PEER_HEADS, PEER_KEYS, PEER_HALF), lambda i: (0, 0, 0))],
        out_specs=[out_spec, out_spec],
        out_shape=[jax.ShapeDtypeStruct((PEER_HEADS, PEER_TOPK, n), I32),
                   jax.ShapeDtypeStruct((PEER_HEADS, PEER_TOPK, n), F32)],
        compiler_params=_cparams(("parallel",)),
        name="peer_route",
    )(h, wq, keys)


def _gelu_exact(x):
    return 0.5 * x * (1.0 + lax.erf(x * (2.0 ** -0.5)))


_PEER_T = 8
_DMA_QUEUES = 2
_PEER_ROWS = _PEER_T * PEER_PICKS


def _pack_expert_tables(u, v):
    half = u.shape[1] // 2
    bits = lambda a: lax.bitcast_convert_type(a.astype(BF16), jnp.uint16).astype(jnp.uint32)
    lo = jnp.concatenate([bits(u[:, :half]), bits(v[:, :half])], axis=1)
    hi = jnp.concatenate([bits(u[:, half:]), bits(v[:, half:])], axis=1)
    return lax.bitcast_convert_type(lo | (hi << 16), I32)


def _peer_expert_kernel(idx_hbm, gate_ref, h_ref, x_ref, gf_ref, uv_hbm, o_ref,
                        idx_smem, buf0, buf1, a_ref, idx_sem, row_sem, *, final_norm):
    i = pl.program_id(0)
    n = pl.num_programs(0)
    bufs = (buf0, buf1)
    half = h_ref.shape[1] // 2

    def idx_copy(step, slot):
        dst = idx_smem.at[pl.ds(slot * _PEER_ROWS, _PEER_ROWS)]
        return pltpu.make_async_copy(idx_hbm.at[step], dst, idx_sem.at[slot])

    def issue_token(slot, t):
        for k in range(PEER_PICKS):
            off = pl.multiple_of(idx_smem[slot * _PEER_ROWS + t * PEER_PICKS + k], D_MODEL)
            copy = pltpu.make_async_copy(uv_hbm.at[pl.ds(off, D_MODEL)], bufs[slot].at[t, k], row_sem.at[slot])
            copy.start(priority=k % _DMA_QUEUES)

    def wait_rows(slot):
        pltpu.make_async_copy(bufs[slot], bufs[slot], row_sem.at[slot]).wait()

    @pl.when(i == 0)
    def _():
        idx_copy(0, 0).start()
        idx_copy(0, 0).wait()
        lax.fori_loop(0, _PEER_T, lambda t, c: (issue_token(0, t), c)[1], 0)
        idx_copy(1, 1).start()

    def step(cur, nxt):
        idx_copy(i + 1, nxt).wait()
        idx_copy(i + 2, cur).start()
        wait_rows(cur)
        buf = bufs[cur]

        h = h_ref[...]
        hh = jnp.concatenate([h[:, :half], h[:, half:]], axis=0).astype(BF16)

        @pl.when(i >= 0)
        def _():
            for t in range(_PEER_T // 2):
                issue_token(nxt, t)
            tok = lax.broadcasted_iota(I32, (_PEER_T, 2 * PEER_PICKS), 0)
            a = jnp.zeros((_PEER_T, 2 * PEER_PICKS), F32)
            for t in range(_PEER_T):
                urows = pltpu.bitcast(buf[t, :, :half], BF16)
                r = lax.dot_general(hh, urows, (((1,), (1,)), ((), ())), preferred_element_type=F32)
                r = r[:_PEER_T] + pltpu.roll(r[_PEER_T:], 2 * PEER_PICKS - 1, axis=1)
                a = jnp.where(tok == t, r, a)
            a_ref[...] = a

        for t in range(_PEER_T // 2, _PEER_T):
            issue_token(nxt, t)
        w_even = gate_ref[...] * _gelu_exact(a_ref[...])
        w = jnp.concatenate([w_even, pltpu.roll(w_even, 1, axis=1)], axis=0).astype(BF16)
        tok = lax.broadcasted_iota(I32, (_PEER_T, half), 0)
        lo = jnp.zeros((_PEER_T, half), F32)
        hi = jnp.zeros((_PEER_T, half), F32)
        for t in range(_PEER_T):
            vrows = pltpu.bitcast(buf[t, :, half:], BF16)
            o = jnp.dot(w, vrows, preferred_element_type=F32)
            lo = jnp.where(tok == t, o[:_PEER_T], lo)
            hi = jnp.where(tok == t, o[_PEER_T:], hi)
        y = x_ref[...] + jnp.concatenate([lo, hi], axis=1)
        if final_norm:
            ms = jnp.mean(y * y, axis=-1, keepdims=True)
            y = y * lax.rsqrt(ms + EPS) * gf_ref[...]
        o_ref[...] = y

        @pl.when(i == n - 1)
        def _():
            wait_rows(nxt)
            idx_copy(i + 2, cur).wait()

    for parity in range(2):
        pl.when(i % 2 == parity)(functools.partial(step, parity, 1 - parity))


def _peer_experts(eidx, gate, h, x, uv, g_final, *, final_norm):
    n, d = x.shape
    steps = n // _PEER_T
    row = lambda width: pl.BlockSpec((_PEER_T, width), lambda i: (i, 0))
    gate2 = jnp.stack([gate, jnp.zeros_like(gate)], axis=-1).reshape(n, 2 * PEER_PICKS)
    return pl.pallas_call(
        functools.partial(_peer_expert_kernel, final_norm=final_norm),
        grid=(steps,),
        in_specs=[pl.BlockSpec(memory_space=pl.ANY), row(2 * PEER_PICKS), row(d), row(d),
                  pl.BlockSpec((1, d), lambda i: (0, 0)),
                  pl.BlockSpec(memory_space=pl.ANY)],
        out_specs=row(d),
        out_shape=jax.ShapeDtypeStruct((n, d), F32),
        scratch_shapes=[pltpu.SMEM((2 * _PEER_ROWS,), I32),
                        pltpu.VMEM((_PEER_T, PEER_PICKS, d), I32),
                        pltpu.VMEM((_PEER_T, PEER_PICKS, d), I32),
                        pltpu.VMEM((_PEER_T, 2 * PEER_PICKS), F32),
                        pltpu.SemaphoreType.DMA((2,)),
                        pltpu.SemaphoreType.DMA((2,))],
        compiler_params=_cparams(("arbitrary",)),
        name="peer_experts",
    )(jnp.pad((eidx * d).reshape(steps, _PEER_ROWS), ((0, 2), (0, 0))), gate2, h, x, g_final.reshape(1, d),
      uv.reshape(-1))


_MACRO = 256
_MACRO_STEPS = _MACRO // _PEER_T
_ROUTE_PHASES = PEER_HEADS + 2


def _route_experts_kernel(h_ref, x_ref, gf_ref, hall_hbm, wq_hbm, keys_hbm, uv_hbm, o_ref,
                          wq_ref, keys_ref, hbuf, q_ref, idx_pk, gate_pk, g2_ref, idx_tok, gate_tok,
                          idx_s0, idx_s1, buf0, buf1, a_ref, w_sem, idx_sem, row_sem, *, final_norm, n_tiles):
    i = pl.program_id(0)
    n = pl.num_programs(0)
    bufs = (buf0, buf1)
    idx_smem = (idx_s0, idx_s1)
    half = h_ref.shape[1] // 2

    def phase_project(mt):
        cp = pltpu.make_async_copy(hall_hbm.at[pl.ds(pl.multiple_of(mt * _MACRO, _MACRO), _MACRO)], hbuf, w_sem)
        cp.start()
        cp.wait()
        q_ref[...] = jnp.dot(hbuf[...].astype(BF16), wq_ref[...], preferred_element_type=F32).astype(BF16)

    def phase_head(hp, mt):
        del mt
        tops = []
        for c in range(2):
            hc = 2 * hp + c
            qc = q_ref[:, hc * PEER_HALF:(hc + 1) * PEER_HALF]
            sc = lax.dot_general(keys_ref[hc], qc, (((1,), (1,)), ((), ())), preferred_element_type=F32)
            tops.append(_top16(sc))
        (s1, i1), (s2, i2) = tops
        cand, cidx = _peer_candidates(s1, i1, s2, i2)
        top_s, pos = _top16(cand)
        iota = lax.broadcasted_iota(I32, cand.shape, 0).astype(F32)
        eidx = [jnp.sum(jnp.where(iota == p, cidx, 0.0), axis=0, keepdims=True) for p in pos]
        ts = jnp.concatenate(top_s, axis=0)
        e = jnp.exp(ts - top_s[0])
        rows = slice(hp * PEER_TOPK, (hp + 1) * PEER_TOPK)
        gate_pk[rows, :] = e / jnp.sum(e, axis=0, keepdims=True)
        idx_pk[rows, :] = jnp.concatenate(eidx, axis=0)

    def phase_finalize(mt):
        slot = mt % 2
        lanes = 128
        for hf in range(_MACRO // lanes):
            toks = slice(hf * lanes, (hf + 1) * lanes)
            g = gate_pk[:, toks]
            g2_ref[hf, pl.ds(0, PEER_PICKS, stride=2), :] = g
            g2_ref[hf, pl.ds(1, PEER_PICKS, stride=2), :] = jnp.zeros_like(g)
            gate_tok[slot, toks, :] = g2_ref[hf].T
            idx_tok[slot, toks, :] = (idx_pk[:, toks].astype(I32) * D_MODEL).T

    phases = ([phase_project] + [functools.partial(phase_head, hp) for hp in range(PEER_HEADS)] + [phase_finalize])

    def run_phase(ph, mt):
        lax.switch(ph, [lambda mt, f=f: f(mt) for f in phases], mt)

    def idx_copy(step, slot):
        src = idx_tok.at[(step // _MACRO_STEPS) % 2, pl.ds(pl.multiple_of((step % _MACRO_STEPS) * _PEER_T, _PEER_T), _PEER_T)]
        return pltpu.make_async_copy(src, idx_smem[slot], idx_sem.at[slot])

    def issue_token(slot, t):
        for k in range(PEER_PICKS):
            off = pl.multiple_of(idx_smem[slot][t, k], D_MODEL)
            copy = pltpu.make_async_copy(uv_hbm.at[pl.ds(off, D_MODEL)], bufs[slot].at[t, k], row_sem.at[slot])
            copy.start(priority=k % _DMA_QUEUES)

    def wait_rows(slot):
        pltpu.make_async_copy(bufs[slot], bufs[slot], row_sem.at[slot]).wait()

    @pl.when(i == 0)
    def _():
        cw = pltpu.make_async_copy(wq_hbm, wq_ref, w_sem)
        cw.start()
        cw.wait()
        ck = pltpu.make_async_copy(keys_hbm, keys_ref, w_sem)
        ck.start()
        ck.wait()

    def routed(r, carry):
        first = (i == 0) & (r < _ROUTE_PHASES)
        ph = jnp.where(first, r, i % _MACRO_STEPS)
        mt = jnp.where(first, 0, i // _MACRO_STEPS + 1)

        @pl.when(first | ((ph < _ROUTE_PHASES) & (mt < n_tiles)))
        def _():
            run_phase(ph, mt)
        return carry
    lax.fori_loop(0, jnp.where(i == 0, _ROUTE_PHASES + 1, 1), routed, 0)

    @pl.when(i == 0)
    def _():
        idx_copy(0, 0).start()
        idx_copy(0, 0).wait()
        lax.fori_loop(0, _PEER_T, lambda t, c: (issue_token(0, t), c)[1], 0)
        idx_copy(1, 1).start()

    def step(cur, nxt):
        idx_copy(i + 1, nxt).wait()
        idx_copy(i + 2, cur).start()
        wait_rows(cur)
        buf = bufs[cur]
        tile_slot = (i // _MACRO_STEPS) % 2
        tile_row = pl.multiple_of((i % _MACRO_STEPS) * _PEER_T, _PEER_T)

        h = h_ref[...]
        hh = jnp.concatenate([h[:, :half], h[:, half:]], axis=0).astype(BF16)

        @pl.when(i >= 0)
        def _():
            for t in range(_PEER_T // 2):
                issue_token(nxt, t)
            tok = lax.broadcasted_iota(I32, (_PEER_T, 2 * PEER_PICKS), 0)
            a = jnp.zeros((_PEER_T, 2 * PEER_PICKS), F32)
            for t in range(_PEER_T):
                urows = pltpu.bitcast(buf[t, :, :half], BF16)
                r = lax.dot_general(hh, urows, (((1,), (1,)), ((), ())), preferred_element_type=F32)
                r = r[:_PEER_T] + pltpu.roll(r[_PEER_T:], 2 * PEER_PICKS - 1, axis=1)
                a = jnp.where(tok == t, r, a)
            a_ref[...] = a

        for t in range(_PEER_T // 2, _PEER_T):
            issue_token(nxt, t)
        w_even = gate_tok[tile_slot, pl.ds(tile_row, _PEER_T), :] * _gelu_exact(a_ref[...])
        w = jnp.concatenate([w_even, pltpu.roll(w_even, 1, axis=1)], axis=0).astype(BF16)
        tok = lax.broadcasted_iota(I32, (_PEER_T, half), 0)
        lo = jnp.zeros((_PEER_T, half), F32)
        hi = jnp.zeros((_PEER_T, half), F32)
        for t in range(_PEER_T):
            vrows = pltpu.bitcast(buf[t, :, half:], BF16)
            o = jnp.dot(w, vrows, preferred_element_type=F32)
            lo = jnp.where(tok == t, o[:_PEER_T], lo)
            hi = jnp.where(tok == t, o[_PEER_T:], hi)
        y = x_ref[...] + jnp.concatenate([lo, hi], axis=1)
        if final_norm:
            ms = jnp.mean(y * y, axis=-1, keepdims=True)
            y = y * lax.rsqrt(ms + EPS) * gf_ref[...]
        o_ref[...] = y

        @pl.when(i == n - 1)
        def _():
            wait_rows(nxt)
            idx_copy(i + 2, cur).wait()

    for parity in range(2):
        pl.when(i % 2 == parity)(functools.partial(step, parity, 1 - parity))


def _peer_route_experts(h, x, wq, keys, uv, g_final, *, final_norm):
    n, d = x.shape
    assert n % _MACRO == 0
    steps = n // _PEER_T
    row = lambda width: pl.BlockSpec((_PEER_T, width), lambda i: (i, 0))
    anyspec = pl.BlockSpec(memory_space=pl.ANY)
    return pl.pallas_call(
        functools.partial(_route_experts_kernel, final_norm=final_norm, n_tiles=n // _MACRO),
        grid=(steps,),
        in_specs=[row(d), row(d), pl.BlockSpec((1, d), lambda i: (0, 0)), anyspec, anyspec, anyspec, anyspec],
        out_specs=row(d),
        out_shape=jax.ShapeDtypeStruct((n, d), F32),
        scratch_shapes=[pltpu.VMEM(wq.shape, BF16),
                        pltpu.VMEM(keys.shape, BF16),
                        pltpu.VMEM((_MACRO, d), F32),
                        pltpu.VMEM((_MACRO, wq.shape[1]), BF16),
                        pltpu.VMEM((PEER_PICKS, _MACRO), F32),
                        pltpu.VMEM((PEER_PICKS, _MACRO), F32),
                        pltpu.VMEM((_MACRO // 128, 2 * PEER_PICKS, 128), F32),
                        pltpu.VMEM((2, _MACRO, PEER_PICKS), I32),
                        pltpu.VMEM((2, _MACRO, 2 * PEER_PICKS), F32),
                        pltpu.SMEM((_PEER_T, PEER_PICKS), I32),
                        pltpu.SMEM((_PEER_T, PEER_PICKS), I32),
                        pltpu.VMEM((_PEER_T, PEER_PICKS, d), I32),
                        pltpu.VMEM((_PEER_T, PEER_PICKS, d), I32),
                        pltpu.VMEM((_PEER_T, 2 * PEER_PICKS), F32),
                        pltpu.SemaphoreType.DMA,
                        pltpu.SemaphoreType.DMA((2,)),
                        pltpu.SemaphoreType.DMA((2,))],
        compiler_params=_cparams(("arbitrary",)),
        name="peer_route_experts",
    )(h, x, g_final.reshape(1, d), h, wq, keys, uv.reshape(-1))


def _encoder(xs, rel_bias, g_attn, w_in, a_sink, g_grp_a, nat_rpb, g_grp_b, lam_q1, lam_k1, lam_q2, lam_k2,
             g_subln, w_out, g_ffn, peer_wq, peer_keys, peer_u, peer_v, g_final):
    depth = w_in.shape[0]
    d = xs[0].shape[-1]
    splits = [int(c) for c in np.cumsum((0,) + IN_SIZES)]
    bias_a = _bias_a_table(rel_bias[:, :A_Q_HEADS])
    tiles_c, far_c = _bias_c_tables(rel_bias[:, A_Q_HEADS:])
    shapes = [x.shape[:2] for x in xs]
    xs = [x.reshape(-1, d) for x in xs]
    for l in range(depth):
        lam_init = 0.8 - 0.6 * math.exp(-0.3 * l)
        w_l = jnp.concatenate([w_in[l][:, splits[s]:splits[s + 1]] for s in _PROJ_ORDER], axis=1).astype(BF16)
        w_o = w_out[l].astype(BF16)
        w_q = peer_wq[l].astype(BF16)
        bias_b = _bias_b_table(nat_rpb[l])
        keys = peer_keys[l].reshape(2 * PEER_HEADS, PEER_KEYS, PEER_HALF).astype(BF16)
        uv = _pack_expert_tables(peer_u[l], peer_v[l])
        for b, (n_seq, seq) in enumerate(shapes):
            x = xs[b]
            n = x.shape[0]
            kw = dict(row0=0, n_seq=n_seq, seq=seq)
            proj = _norm_proj(x, g_attn[l], w_l)
            ya = _mixer_a(proj, bias_a, a_sink[l], g_grp_a[l], **kw)
            yb = _mixer_b(proj, bias_b, g_grp_b[l], **kw)
            yc = _mixer_c(proj, tiles_c, far_c, lam_q1[l], lam_k1[l], lam_q2[l], lam_k2[l], g_subln[l],
                          lam_init=lam_init, **kw)
            x1, h2 = _out_proj(x, ya, yb, yc, w_o, g_ffn[l])
            xs[b] = _peer_route_experts(h2, x1, w_q, keys, uv, g_final, final_norm=(l == depth - 1))
    return tuple(x.reshape(n_seq, seq, d) for x, (n_seq, seq) in zip(xs, shapes))


def kernel(x_prompt, x_sample, rel_bias, g_attn, w_in, a_sink, g_grp_a, nat_rpb, g_grp_b, lam_q1, lam_k1, lam_q2, lam_k2, g_subln, w_out, g_ffn, peer_wq, peer_keys, peer_u, peer_v, g_final):
    return _encoder([x_prompt, x_sample], rel_bias, g_attn, w_in, a_sink, g_grp_a, nat_rpb, g_grp_b, lam_q1, lam_k1,
                    lam_q2, lam_k2, g_subln, w_out, g_ffn, peer_wq, peer_keys, peer_u, peer_v, g_final)
```

```python
import functools
import math

import numpy as np
import jax
import jax.numpy as jnp
from jax import lax
from jax.experimental import pallas as pl
from jax.experimental.pallas import tpu as pltpu

F32 = jnp.float32
BF16 = jnp.bfloat16
I32 = jnp.int32
EPS = 1e-6
NEG = -1e30

D_MODEL = 2048
DEPTH = 2
HEAD_DIM = 128
A_Q_HEADS = 6
A_KV_HEADS = 2
A_GROUP = A_Q_HEADS // A_KV_HEADS
WINDOW = 128
BLOCK = 128
B_HEADS = 6
GRID_W = 64
NB_KH = 8
NB_KW = 16
C_HEADS = 4
C_QK_DIM = 64
C_V_DIM = 128
A_WIDTH = A_Q_HEADS * HEAD_DIM
B_WIDTH = B_HEADS * HEAD_DIM
C_WIDTH = C_HEADS * C_V_DIM
A_KV = A_KV_HEADS * HEAD_DIM
C_QK = C_HEADS * 2 * C_QK_DIM
IN_SIZES = (A_WIDTH, A_KV, A_KV, B_WIDTH, B_WIDTH, B_WIDTH, C_QK, C_QK, C_WIDTH)
IN_WIDTH = sum(IN_SIZES)
REL_BUCKETS = 32
REL_MAX_DIST = 128
PEER_HEADS = 8
PEER_KEYS = 128
PEER_TOPK = 16
PEER_QDIM = 256
PEER_HALF = PEER_QDIM // 2
PEER_PICKS = PEER_HEADS * PEER_TOPK

_PROJ_ORDER = (0, 3, 4, 5, 1, 2, 6, 7, 8)
_COL_AQ, _COL_BQ, _COL_BK, _COL_BV = 0, 1, 2, 3
_COL_AK, _COL_AV = 12, 13
_COL_CQ, _COL_CK, _COL_CV = 7, 8, 9

_VMEM_LIMIT = 56 * 1024 * 1024


def _cparams(sem):
    return pltpu.CompilerParams(dimension_semantics=sem, vmem_limit_bytes=_VMEM_LIMIT)


def _t5_bucket(rel):
    nb = REL_BUCKETS // 2
    max_exact = nb // 2
    ret = jnp.where(rel > 0, nb, 0)
    n = jnp.abs(rel)
    nf = jnp.maximum(n, 1).astype(F32)
    large = max_exact + (jnp.log(nf / max_exact) / math.log(REL_MAX_DIST / max_exact) * (nb - max_exact)).astype(I32)
    large = jnp.minimum(large, nb - 1)
    return ret + jnp.where(n < max_exact, n, large)


def _lookup(table, idx):
    onehot = jax.nn.one_hot(idx, table.shape[0], dtype=F32)
    return jnp.einsum('...b,bc->...c', onehot, table.astype(F32), precision=lax.Precision.HIGHEST)


def _bias_a_table(rel_bias_a):
    rel = np.arange(3 * BLOCK)[None, :] - BLOCK - np.arange(BLOCK)[:, None]
    bias = _lookup(rel_bias_a, _t5_bucket(jnp.asarray(rel)))
    bias = jnp.where(jnp.asarray(np.abs(rel) <= WINDOW)[:, :, None], bias, NEG)
    return bias.transpose(2, 0, 1)


def _bias_b_table(rpb):
    classes = (((0, 1), (0, 0)), ((2, 3), (0, 0)), ((4, 5), (0, 1)), ((5, 6), (1, 1)), ((7, 8), (1, 1)))
    qc = np.arange(GRID_W)[:, None]
    kc = np.arange(GRID_W)[None, :]
    c0 = np.clip(qc - NB_KW // 2, 0, GRID_W - NB_KW)
    in_cols = (kc >= c0) & (kc < c0 + NB_KW)
    n_dc = 2 * NB_KW - 1
    dc_onehot = ((kc - qc + NB_KW - 1)[:, :, None] == np.arange(n_dc)).astype(np.float32)
    blocks = jnp.einsum('qkc,hdc->hdqk', jnp.asarray(dc_onehot), rpb.astype(F32), precision=lax.Precision.HIGHEST)
    blocks = jnp.where(jnp.asarray(in_cols), blocks, NEG)
    neg = jnp.full((rpb.shape[0], GRID_W, GRID_W), NEG, F32)
    tabs = []
    for (rq, r0) in classes:
        rows = []
        for a in range(_B_QROWS):
            parts = [blocks[:, kr - rq[a] + NB_KH - 1] if r0[a] <= kr < r0[a] + NB_KH else neg
                     for kr in range(_B_KROWS)]
            rows.append(jnp.concatenate(parts, axis=-1))
        tabs.append(jnp.concatenate(rows, axis=1))
    return jnp.stack(tabs)


def _bias_c_tables(rel_bias_c):
    q = np.arange(BLOCK)[:, None]
    k = np.arange(BLOCK)[None, :]
    far = _lookup(rel_bias_c, _t5_bucket(jnp.asarray([-2 * BLOCK, 2 * BLOCK])))
    tiles = []
    for d in (-1, 0, 1):
        rel = k + d * BLOCK - q
        t = _lookup(rel_bias_c, _t5_bucket(jnp.asarray(rel))).transpose(2, 0, 1)
        base = far[0] if d < 0 else far[1]
        tiles.append(t - base[:, None, None])
    return jnp.stack(tiles, axis=1), far.T


def _norm_proj_kernel(x_ref, g_ref, w_ref, o_ref, h_ref):
    @pl.when(pl.program_id(1) == 0)
    def _():
        x = x_ref[...]
        ms = jnp.mean(x * x, axis=-1, keepdims=True)
        h_ref[...] = (x * lax.rsqrt(ms + EPS) * g_ref[...]).astype(BF16)

    o_ref[...] = jnp.dot(h_ref[...], w_ref[...], preferred_element_type=F32).astype(o_ref.dtype)


def _norm_proj(x, g, w, *, tm=512, tn=1280):
    n, d = x.shape
    width = w.shape[1]
    return pl.pallas_call(
        _norm_proj_kernel,
        grid=(n // tm, width // tn),
        in_specs=[
            pl.BlockSpec((tm, d), lambda i, j: (i, 0)),
            pl.BlockSpec((1, d), lambda i, j: (0, 0)),
            pl.BlockSpec((d, tn), lambda i, j: (0, j)),
        ],
        out_specs=pl.BlockSpec((tm, tn), lambda i, j: (i, j)),
        out_shape=jax.ShapeDtypeStruct((n, width), BF16),
        scratch_shapes=[pltpu.VMEM((tm, d), BF16)],
        compiler_params=_cparams(("parallel", "arbitrary")),
        name="norm_proj",
    )(x, g.reshape(1, d), w)


def _group_norm_store(o_ref, outs, g_ref):
    o = jnp.concatenate(outs, axis=-1)
    ms = jnp.mean(o * o, axis=-1, keepdims=True)
    o_ref[...] = (o * lax.rsqrt(ms + EPS) * g_ref[...]).astype(o_ref.dtype)


def _mixer_a_kernel(q_ref, kp_ref, kc_ref, kn_ref, vp_ref, vc_ref, vn_ref, bias_ref, sink_ref, g_ref, o_ref):
    i = pl.program_id(1)
    nb = pl.num_programs(1)
    scale = HEAD_DIM ** -0.5
    col = lax.broadcasted_iota(I32, (BLOCK, 3 * BLOCK), 1)
    lo = jnp.where(i == 0, BLOCK, 0)
    hi = jnp.where(i == nb - 1, 2 * BLOCK, 3 * BLOCK)
    in_seq = (col >= lo) & (col < hi)
    ks, vs, scores = [], [], []
    for kv in range(A_KV_HEADS):
        sl = slice(kv * HEAD_DIM, (kv + 1) * HEAD_DIM)
        ks.append(jnp.concatenate([kp_ref[:, sl], kc_ref[:, sl], kn_ref[:, sl]], axis=0))
        vs.append(jnp.concatenate([vp_ref[:, sl], vc_ref[:, sl], vn_ref[:, sl]], axis=0))
    for h in range(A_Q_HEADS):
        q = q_ref[:, h * HEAD_DIM:(h + 1) * HEAD_DIM]
        scores.append(lax.dot_general(q, ks[h // A_GROUP], (((1,), (1,)), ((), ())), preferred_element_type=F32))
    probs, denoms = [], []
    for h in range(A_Q_HEADS):
        s = jnp.where(in_seq, scores[h] * scale + bias_ref[h], NEG)
        sk = sink_ref[h]
        m = jnp.maximum(jnp.max(s, axis=-1, keepdims=True), sk)
        p = jnp.exp(s - m)
        denoms.append(jnp.sum(p, axis=-1, keepdims=True) + jnp.exp(sk - m))
        probs.append(p.astype(BF16))
    outs = [jnp.dot(probs[h], vs[h // A_GROUP], preferred_element_type=F32) / denoms[h] for h in range(A_Q_HEADS)]
    _group_norm_store(o_ref, outs, g_ref)


def _mixer_a(proj, bias, sink, g, *, row0, n_seq, seq):
    nb = seq // BLOCK
    b0 = row0 // BLOCK

    def blk(shift, colblk):
        return lambda b, i: (b0 + b * nb + jnp.clip(i + shift, 0, nb - 1), colblk)

    kv_spec = lambda shift, colblk: pl.BlockSpec((BLOCK, A_KV), blk(shift, colblk))
    return pl.pallas_call(
        _mixer_a_kernel,
        grid=(n_seq, nb),
        in_specs=[
            pl.BlockSpec((BLOCK, A_WIDTH), blk(0, _COL_AQ)),
            kv_spec(-1, _COL_AK), kv_spec(0, _COL_AK), kv_spec(1, _COL_AK),
            kv_spec(-1, _COL_AV), kv_spec(0, _COL_AV), kv_spec(1, _COL_AV),
            pl.BlockSpec((A_Q_HEADS, BLOCK, 3 * BLOCK), lambda b, i: (0, 0, 0)),
            pl.BlockSpec(memory_space=pltpu.SMEM),
            pl.BlockSpec((1, A_WIDTH), lambda b, i: (0, 0)),
        ],
        out_specs=pl.BlockSpec((BLOCK, A_WIDTH), lambda b, i: (b * nb + i, 0)),
        out_shape=jax.ShapeDtypeStruct((n_seq * seq, A_WIDTH), BF16),
        compiler_params=_cparams(("parallel", "parallel")),
        name="mixer_a",
    )(proj, proj, proj, proj, proj, proj, proj, bias, sink, g.reshape(1, A_WIDTH))


_B_QROWS = 2
_B_KROWS = NB_KH + _B_QROWS - 1


def _mixer_b_kernel(q_ref, k_ref, v_ref, bias_ref, g_ref, o_ref, *, rows):
    i = pl.program_id(1)
    scale = HEAD_DIM ** -0.5
    kstart = jnp.clip(_B_QROWS * i - NB_KH // 2, 0, rows - _B_KROWS)
    ks = pl.multiple_of(kstart * GRID_W, GRID_W)
    nk = _B_KROWS * GRID_W
    heads = [slice(h * HEAD_DIM, (h + 1) * HEAD_DIM) for h in range(B_HEADS)]
    scores = [lax.dot_general(q_ref[:, sl], k_ref[pl.ds(ks, nk), sl], (((1,), (1,)), ((), ())),
                              preferred_element_type=F32) for sl in heads]
    probs, denoms = [], []
    for h in range(B_HEADS):
        s = scores[h] * scale + bias_ref[0, h]
        p = jnp.exp(s - jnp.max(s, axis=-1, keepdims=True))
        denoms.append(jnp.sum(p, axis=-1, keepdims=True))
        probs.append(p.astype(BF16))
    outs = [jnp.dot(probs[h], v_ref[pl.ds(ks, nk), sl], preferred_element_type=F32) / denoms[h]
            for h, sl in enumerate(heads)]
    _group_norm_store(o_ref, outs, g_ref)


def _mixer_b(proj, bias, g, *, row0, n_seq, seq):
    rows = seq // GRID_W
    nrp = rows // _B_QROWS
    tq = _B_QROWS * GRID_W
    b0 = row0 // tq
    s0 = row0 // seq

    def cls(i):
        return jnp.where(i == 0, 0, jnp.where(i == 1, 1, jnp.where(i == nrp - 2, 3, jnp.where(i == nrp - 1, 4, 2))))

    return pl.pallas_call(
        functools.partial(_mixer_b_kernel, rows=rows),
        grid=(n_seq, nrp),
        in_specs=[
            pl.BlockSpec((tq, B_WIDTH), lambda b, i: (b0 + b * nrp + i, _COL_BQ)),
            pl.BlockSpec((seq, B_WIDTH), lambda b, i: (s0 + b, _COL_BK)),
            pl.BlockSpec((seq, B_WIDTH), lambda b, i: (s0 + b, _COL_BV)),
            pl.BlockSpec((1, B_HEADS, tq, _B_KROWS * GRID_W), lambda b, i: (cls(i), 0, 0, 0)),
            pl.BlockSpec((1, B_WIDTH), lambda b, i: (0, 0)),
        ],
        out_specs=pl.BlockSpec((tq, B_WIDTH), lambda b, i: (b * nrp + i, 0)),
        out_shape=jax.ShapeDtypeStruct((n_seq * seq, B_WIDTH), BF16),
        compiler_params=_cparams(("parallel", "arbitrary")),
        name="mixer_b",
    )(proj, proj, proj, bias, g.reshape(1, B_WIDTH))


def _mixer_c_kernel(q_ref, k_ref, v_ref, tiles_ref, far_ref, lq1_ref, lk1_ref, lq2_ref, lk2_ref, g_ref, o_ref,
                    *s_refs, seq, lam_init):
    i = pl.program_id(1)
    nb = seq // BLOCK
    scale = C_QK_DIM ** -0.5
    lam = (jnp.exp(jnp.sum(lq1_ref[...] * lk1_ref[...], axis=-1, keepdims=True))
           - jnp.exp(jnp.sum(lq2_ref[...] * lk2_ref[...], axis=-1, keepdims=True)) + lam_init)
    before = lax.broadcasted_iota(I32, (1, seq), 1) < i * BLOCK
    lane = lax.broadcasted_iota(I32, (BLOCK, HEAD_DIM), 1)
    for h in range(C_HEADS):
        sl = slice(h * HEAD_DIM, (h + 1) * HEAD_DIM)
        qh = q_ref[:, sl] * scale
        kh = k_ref[:, sl]
        qs = jnp.concatenate(
            [jnp.where((lane >= c * C_QK_DIM) & (lane < (c + 1) * C_QK_DIM), qh, jnp.zeros_like(qh)) for c in range(2)],
            axis=0)
        s = lax.dot_general(qs, kh, (((1,), (1,)), ((), ())), preferred_element_type=F32)
        for c in range(2):
            hc = 2 * h + c
            s_refs[hc][...] = s[c * BLOCK:(c + 1) * BLOCK] + jnp.where(before, far_ref[hc, 0], far_ref[hc, 1])
    outs = []
    for h in range(C_HEADS):
        sl = slice(h * HEAD_DIM, (h + 1) * HEAD_DIM)
        exps, sums = [], []
        for c in range(2):
            hc = 2 * h + c
            for d in range(3):
                j = i + (d - 1)
                inside = jnp.where((j >= 0) & (j < nb), 1.0, 0.0)
                cols = pl.ds(pl.multiple_of(jnp.clip(j, 0, nb - 1) * BLOCK, BLOCK), BLOCK)
                s_refs[hc][:, cols] += tiles_ref[hc, d] * inside
            s = s_refs[hc][...]
            e = jnp.exp(s - jnp.max(s, axis=-1, keepdims=True))
            exps.append(e)
            sums.append(jnp.sum(e, axis=-1, keepdims=True))
        a = (exps[0] - (lam * sums[0] / sums[1]) * exps[1]).astype(BF16)
        o = jnp.dot(a, v_ref[:, sl], preferred_element_type=F32) / sums[0]
        ms = jnp.mean(o * o, axis=-1, keepdims=True)
        outs.append(o * lax.rsqrt(ms + EPS) * g_ref[...] * (1.0 - lam_init))
    o_ref[...] = jnp.concatenate(outs, axis=-1).astype(o_ref.dtype)


def _mixer_c(proj, tiles, far, lq1, lk1, lq2, lk2, g, *, row0, n_seq, seq, lam_init):
    nb = seq // BLOCK
    b0 = row0 // BLOCK
    s0 = row0 // seq
    vec = lambda a: a.reshape(1, C_QK_DIM)
    vspec = pl.BlockSpec((1, C_QK_DIM), lambda b, i: (0, 0))
    return pl.pallas_call(
        functools.partial(_mixer_c_kernel, seq=seq, lam_init=lam_init),
        grid=(n_seq, nb),
        in_specs=[
            pl.BlockSpec((BLOCK, C_QK), lambda b, i: (b0 + b * nb + i, _COL_CQ)),
            pl.BlockSpec((seq, C_QK), lambda b, i: (s0 + b, _COL_CK)),
            pl.BlockSpec((seq, C_WIDTH), lambda b, i: (s0 + b, _COL_CV)),
            pl.BlockSpec((2 * C_HEADS, 3, BLOCK, BLOCK), lambda b, i: (0, 0, 0, 0)),
            pl.BlockSpec(memory_space=pltpu.SMEM),
            vspec, vspec, vspec, vspec,
            pl.BlockSpec((1, C_V_DIM), lambda b, i: (0, 0)),
        ],
        out_specs=pl.BlockSpec((BLOCK, C_WIDTH), lambda b, i: (b * nb + i, 0)),
        out_shape=jax.ShapeDtypeStruct((n_seq * seq, C_WIDTH), BF16),
        scratch_shapes=[pltpu.VMEM((BLOCK, seq), F32)] * (2 * C_HEADS),
        compiler_params=_cparams(("parallel", "arbitrary")),
        name="mixer_c",
    )(proj, proj, proj, tiles, far, vec(lq1), vec(lk1), vec(lq2), vec(lk2), g.reshape(1, C_V_DIM))


def _out_proj_kernel(x_ref, ya_ref, yb_ref, yc_ref, w_ref, g_ref, x1_ref, h_ref):
    acc = jnp.dot(ya_ref[...], w_ref[0:A_WIDTH, :], preferred_element_type=F32)
    acc += jnp.dot(yb_ref[...], w_ref[A_WIDTH:A_WIDTH + B_WIDTH, :], preferred_element_type=F32)
    acc += jnp.dot(yc_ref[...], w_ref[A_WIDTH + B_WIDTH:, :], preferred_element_type=F32)
    x1 = x_ref[...] + acc
    x1_ref[...] = x1
    ms = jnp.mean(x1 * x1, axis=-1, keepdims=True)
    h_ref[...] = x1 * lax.rsqrt(ms + EPS) * g_ref[...]


def _out_proj(x, ya, yb, yc, w, g, *, tm=256):
    n, d = x.shape
    row = lambda width: pl.BlockSpec((tm, width), lambda i: (i, 0))
    return pl.pallas_call(
        _out_proj_kernel,
        grid=(n // tm,),
        in_specs=[row(d), row(A_WIDTH), row(B_WIDTH), row(C_WIDTH),
                  pl.BlockSpec((d, d), lambda i: (0, 0)),
                  pl.BlockSpec((1, d), lambda i: (0, 0))],
        out_specs=[row(d), row(d)],
        out_shape=[jax.ShapeDtypeStruct((n, d), F32), jax.ShapeDtypeStruct((n, d), F32)],
        compiler_params=_cparams(("parallel",)),
        name="out_proj",
    )(x, ya, yb, yc, w, g.reshape(1, d))


def _top16(x):
    r = float(x.shape[0])
    iota = lax.broadcasted_iota(I32, x.shape, 0).astype(F32)
    vals, idxs = [], []
    for _ in range(PEER_TOPK):
        m = jnp.max(x, axis=0, keepdims=True)
        idx = jnp.min(jnp.where(x == m, iota, r), axis=0, keepdims=True)
        vals.append(m)
        idxs.append(idx)
        x = jnp.where(iota == idx, -jnp.inf, x)
    return vals, idxs


def _peer_candidates(s1, i1, s2, i2):
    sub = 8
    s1a, i1a = jnp.concatenate(s1, axis=0), jnp.concatenate(i1, axis=0)
    s2a, i2a = jnp.concatenate(s2, axis=0), jnp.concatenate(i2, axis=0)
    row = lax.broadcasted_iota(I32, s2a[:sub].shape, 0)
    cand, cidx = [], []
    first_single = PEER_TOPK // 2
    for a in range(first_single):
        nb = PEER_TOPK // (a + 1)
        if nb > sub:
            cand.append(s1[a] + s2a)
            cidx.append(i1[a] * PEER_KEYS + i2a)
        else:
            cand.append(jnp.where(row < nb, s1[a] + s2a[:sub], -jnp.inf))
            cidx.append(i1[a] * PEER_KEYS + i2a[:sub])
    cand.append(s1a[first_single:] + s2[0])
    cidx.append(i1a[first_single:] * PEER_KEYS + i2[0])
    return jnp.concatenate(cand, axis=0), jnp.concatenate(cidx, axis=0)


def _peer_route_kernel(h_ref, wq_ref, keys_ref, eidx_ref, gate_ref):
    q = jnp.dot(h_ref[...].astype(BF16), wq_ref[...], preferred_element_type=F32).astype(BF16)
    tm = q.shape[0]
    for hp in range(PEER_HEADS):
        tops = []
        for c in range(2):
            hc = 2 * hp + c
            qc = q[:, hc * PEER_HALF:(hc + 1) * PEER_HALF]
            sc = lax.dot_general(keys_ref[hc], qc, (((1,), (1,)), ((), ())), preferred_element_type=F32)
            tops.append(_top16(sc))
        (s1, i1), (s2, i2) = tops
        cand, cidx = _peer_candidates(s1, i1, s2, i2)
        top_s, pos = _top16(cand)
        iota = lax.broadcasted_iota(I32, cand.shape, 0).astype(F32)
        eidx = [jnp.sum(jnp.where(iota == p, cidx, 0.0), axis=0, keepdims=True) for p in pos]
        ts = jnp.concatenate(top_s, axis=0)
        e = jnp.exp(ts - top_s[0])
        gate_ref[hp] = e / jnp.sum(e, axis=0, keepdims=True)
        eidx_ref[hp] = jnp.concatenate(eidx, axis=0).astype(I32)


def _peer_route(h, wq, keys, *, tm=256):
    n, d = h.shape
    out_spec = pl.BlockSpec((PEER_HEADS, PEER_TOPK, tm), lambda i: (0, 0, i))
    return pl.pallas_call(
        _peer_route_kernel,
        grid=(n // tm,),
        in_specs=[pl.BlockSpec((tm, d), lambda i: (i, 0)),
                  pl.BlockSpec((d, PEER_HEADS * PEER_QDIM), lambda i: (0, 0)),
                  pl.BlockSpec((2 * PEER_HEADS, PEER_KEYS, PEER_HALF), lambda i: (0, 0, 0))],
        out_specs=[out_spec, out_spec],
        out_shape=[jax.ShapeDtypeStruct((PEER_HEADS, PEER_TOPK, n), I32),
                   jax.ShapeDtypeStruct((PEER_HEADS, PEER_TOPK, n), F32)],
        compiler_params=_cparams(("parallel",)),
        name="peer_route",
    )(h, wq, keys)


def _gelu_exact(x):
    return 0.5 * x * (1.0 + lax.erf(x * (2.0 ** -0.5)))


_PEER_T = 8
_DMA_QUEUES = 2
_PEER_ROWS = _PEER_T * PEER_PICKS


def _pack_expert_tables(u, v):
    half = u.shape[1] // 2
    bits = lambda a: lax.bitcast_convert_type(a.astype(BF16), jnp.uint16).astype(jnp.uint32)
    lo = jnp.concatenate([bits(u[:, :half]), bits(v[:, :half])], axis=1)
    hi = jnp.concatenate([bits(u[:, half:]), bits(v[:, half:])], axis=1)
    return lax.bitcast_convert_type(lo | (hi << 16), I32)


def _peer_expert_kernel(idx_hbm, gate_ref, h_ref, x_ref, gf_ref, uv_hbm, o_ref,
                        idx_smem, buf0, buf1, a_ref, idx_sem, row_sem, *, final_norm):
    i = pl.program_id(0)
    n = pl.num_programs(0)
    bufs = (buf0, buf1)
    half = h_ref.shape[1] // 2

    def idx_copy(step, slot):
        dst = idx_smem.at[pl.ds(slot * _PEER_ROWS, _PEER_ROWS)]
        return pltpu.make_async_copy(idx_hbm.at[step], dst, idx_sem.at[slot])

    def issue_token(slot, t):
        for k in range(PEER_PICKS):
            off = pl.multiple_of(idx_smem[slot * _PEER_ROWS + t * PEER_PICKS + k], D_MODEL)
            copy = pltpu.make_async_copy(uv_hbm.at[pl.ds(off, D_MODEL)], bufs[slot].at[t, k], row_sem.at[slot])
            copy.start(priority=k % _DMA_QUEUES)

    def wait_rows(slot):
        pltpu.make_async_copy(bufs[slot], bufs[slot], row_sem.at[slot]).wait()

    @pl.when(i == 0)
    def _():
        idx_copy(0, 0).start()
        idx_copy(0, 0).wait()
        lax.fori_loop(0, _PEER_T, lambda t, c: (issue_token(0, t), c)[1], 0)
        idx_copy(1, 1).start()

    def step(cur, nxt):
        idx_copy(i + 1, nxt).wait()
        idx_copy(i + 2, cur).start()
        wait_rows(cur)
        buf = bufs[cur]

        h = h_ref[...]
        hh = jnp.concatenate([h[:, :half], h[:, half:]], axis=0).astype(BF16)

        @pl.when(i >= 0)
        def _():
            for t in range(_PEER_T // 2):
                issue_token(nxt, t)
            tok = lax.broadcasted_iota(I32, (_PEER_T, 2 * PEER_PICKS), 0)
            a = jnp.zeros((_PEER_T, 2 * PEER_PICKS), F32)
            for t in range(_PEER_T):
                urows = pltpu.bitcast(buf[t, :, :half], BF16)
                r = lax.dot_general(hh, urows, (((1,), (1,)), ((), ())), preferred_element_type=F32)
                r = r[:_PEER_T] + pltpu.roll(r[_PEER_T:], 2 * PEER_PICKS - 1, axis=1)
                a = jnp.where(tok == t, r, a)
            a_ref[...] = a

        for t in range(_PEER_T // 2, _PEER_T):
            issue_token(nxt, t)
        w_even = gate_ref[...] * _gelu_exact(a_ref[...])
        w = jnp.concatenate([w_even, pltpu.roll(w_even, 1, axis=1)], axis=0).astype(BF16)
        tok = lax.broadcasted_iota(I32, (_PEER_T, half), 0)
        lo = jnp.zeros((_PEER_T, half), F32)
        hi = jnp.zeros((_PEER_T, half), F32)
        for t in range(_PEER_T):
            vrows = pltpu.bitcast(buf[t, :, half:], BF16)
            o = jnp.dot(w, vrows, preferred_element_type=F32)
            lo = jnp.where(tok == t, o[:_PEER_T], lo)
            hi = jnp.where(tok == t, o[_PEER_T:], hi)
        y = x_ref[...] + jnp.concatenate([lo, hi], axis=1)
        if final_norm:
            ms = jnp.mean(y * y, axis=-1, keepdims=True)
            y = y * lax.rsqrt(ms + EPS) * gf_ref[...]
        o_ref[...] = y

        @pl.when(i == n - 1)
        def _():
            wait_rows(nxt)
            idx_copy(i + 2, cur).wait()

    for parity in range(2):
        pl.when(i % 2 == parity)(functools.partial(step, parity, 1 - parity))


def _peer_experts(eidx, gate, h, x, uv, g_final, *, final_norm):
    n, d = x.shape
    steps = n // _PEER_T
    row = lambda width: pl.BlockSpec((_PEER_T, width), lambda i: (i, 0))
    gate2 = jnp.stack([gate, jnp.zeros_like(gate)], axis=-1).reshape(n, 2 * PEER_PICKS)
    return pl.pallas_call(
        functools.partial(_peer_expert_kernel, final_norm=final_norm),
        grid=(steps,),
        in_specs=[pl.BlockSpec(memory_space=pl.ANY), row(2 * PEER_PICKS), row(d), row(d),
                  pl.BlockSpec((1, d), lambda i: (0, 0)),
                  pl.BlockSpec(memory_space=pl.ANY)],
        out_specs=row(d),
        out_shape=jax.ShapeDtypeStruct((n, d), F32),
        scratch_shapes=[pltpu.SMEM((2 * _PEER_ROWS,), I32),
                        pltpu.VMEM((_PEER_T, PEER_PICKS, d), I32),
                        pltpu.VMEM((_PEER_T, PEER_PICKS, d), I32),
                        pltpu.VMEM((_PEER_T, 2 * PEER_PICKS), F32),
                        pltpu.SemaphoreType.DMA((2,)),
                        pltpu.SemaphoreType.DMA((2,))],
        compiler_params=_cparams(("arbitrary",)),
        name="peer_experts",
    )(jnp.pad((eidx * d).reshape(steps, _PEER_ROWS), ((0, 2), (0, 0))), gate2, h, x, g_final.reshape(1, d),
      uv.reshape(-1))


_MACRO = 256
_MACRO_STEPS = _MACRO // _PEER_T
_ROUTE_PHASES = PEER_HEADS + 2


def _route_experts_kernel(h_ref, x_ref, gf_ref, hall_hbm, wq_hbm, keys_hbm, uv_hbm, o_ref,
                          wq_ref, keys_ref, hbuf, q_ref, idx_pk, gate_pk, g2_ref, idx_tok, gate_tok,
                          idx_s0, idx_s1, buf0, buf1, a_ref, w_sem, idx_sem, row_sem, *, final_norm, n_tiles):
    i = pl.program_id(0)
    n = pl.num_programs(0)
    bufs = (buf0, buf1)
    idx_smem = (idx_s0, idx_s1)
    half = h_ref.shape[1] // 2

    def phase_project(mt):
        cp = pltpu.make_async_copy(hall_hbm.at[pl.ds(pl.multiple_of(mt * _MACRO, _MACRO), _MACRO)], hbuf, w_sem)
        cp.start()
        cp.wait()
        q_ref[...] = jnp.dot(hbuf[...].astype(BF16), wq_ref[...], preferred_element_type=F32).astype(BF16)

    def phase_head(hp, mt):
        del mt
        tops = []
        for c in range(2):
            hc = 2 * hp + c
            qc = q_ref[:, hc * PEER_HALF:(hc + 1) * PEER_HALF]
            sc = lax.dot_general(keys_ref[hc], qc, (((1,), (1,)), ((), ())), preferred_element_type=F32)
            tops.append(_top16(sc))
        (s1, i1), (s2, i2) = tops
        cand, cidx = _peer_candidates(s1, i1, s2, i2)
        top_s, pos = _top16(cand)
        iota = lax.broadcasted_iota(I32, cand.shape, 0).astype(F32)
        eidx = [jnp.sum(jnp.where(iota == p, cidx, 0.0), axis=0, keepdims=True) for p in pos]
        ts = jnp.concatenate(top_s, axis=0)
        e = jnp.exp(ts - top_s[0])
        rows = slice(hp * PEER_TOPK, (hp + 1) * PEER_TOPK)
        gate_pk[rows, :] = e / jnp.sum(e, axis=0, keepdims=True)
        idx_pk[rows, :] = jnp.concatenate(eidx, axis=0)

    def phase_finalize(mt):
        slot = mt % 2
        lanes = 128
        for hf in range(_MACRO // lanes):
            toks = slice(hf * lanes, (hf + 1) * lanes)
            g = gate_pk[:, toks]
            g2_ref[hf, pl.ds(0, PEER_PICKS, stride=2), :] = g
            g2_ref[hf, pl.ds(1, PEER_PICKS, stride=2), :] = jnp.zeros_like(g)
            gate_tok[slot, toks, :] = g2_ref[hf].T
            idx_tok[slot, toks, :] = (idx_pk[:, toks].astype(I32) * D_MODEL).T

    phases = ([phase_project] + [functools.partial(phase_head, hp) for hp in range(PEER_HEADS)] + [phase_finalize])

    def run_phase(ph, mt):
        lax.switch(ph, [lambda mt, f=f: f(mt) for f in phases], mt)

    def idx_copy(step, slot):
        src = idx_tok.at[(step // _MACRO_STEPS) % 2, pl.ds(pl.multiple_of((step % _MACRO_STEPS) * _PEER_T, _PEER_T), _PEER_T)]
        return pltpu.make_async_copy(src, idx_smem[slot], idx_sem.at[slot])

    def issue_token(slot, t):
        for k in range(PEER_PICKS):
            off = pl.multiple_of(idx_smem[slot][t, k], D_MODEL)
            copy = pltpu.make_async_copy(uv_hbm.at[pl.ds(off, D_MODEL)], bufs[slot].at[t, k], row_sem.at[slot])
            copy.start(priority=k % _DMA_QUEUES)

    def wait_rows(slot):
        pltpu.make_async_copy(bufs[slot], bufs[slot], row_sem.at[slot]).wait()

    @pl.when(i == 0)
    def _():
        cw = pltpu.make_async_copy(wq_hbm, wq_ref, w_sem)
        cw.start()
        cw.wait()
        ck = pltpu.make_async_copy(keys_hbm, keys_ref, w_sem)
        ck.start()
        ck.wait()

    def routed(r, carry):
        first = (i == 0) & (r < _ROUTE_PHASES)
        ph = jnp.where(first, r, i % _MACRO_STEPS)
        mt = jnp.where(first, 0, i // _MACRO_STEPS + 1)

        @pl.when(first | ((ph < _ROUTE_PHASES) & (mt < n_tiles)))
        def _():
            run_phase(ph, mt)
        return carry
    lax.fori_loop(0, jnp.where(i == 0, _ROUTE_PHASES + 1, 1), routed, 0)

    @pl.when(i == 0)
    def _():
        idx_copy(0, 0).start()
        idx_copy(0, 0).wait()
        lax.fori_loop(0, _PEER_T, lambda t, c: (issue_token(0, t), c)[1], 0)
        idx_copy(1, 1).start()

    def step(cur, nxt):
        idx_copy(i + 1, nxt).wait()
        idx_copy(i + 2, cur).start()
        for t in range(_PEER_T // 2):
            issue_token(nxt, t)
        wait_rows(cur)
        buf = bufs[cur]
        tile_slot = (i // _MACRO_STEPS) % 2
        tile_row = pl.multiple_of((i % _MACRO_STEPS) * _PEER_T, _PEER_T)

        h = h_ref[...]
        hh = jnp.concatenate([h[:, :half], h[:, half:]], axis=0).astype(BF16)

        @pl.when(i >= 0)
        def _():
            tok = lax.broadcasted_iota(I32, (_PEER_T, 2 * PEER_PICKS), 0)
            a = jnp.zeros((_PEER_T, 2 * PEER_PICKS), F32)
            for t in range(_PEER_T):
                urows = pltpu.bitcast(buf[t, :, :half], BF16)
                r = lax.dot_general(hh, urows, (((1,), (1,)), ((), ())), preferred_element_type=F32)
                r = r[:_PEER_T] + pltpu.roll(r[_PEER_T:], 2 * PEER_PICKS - 1, axis=1)
                a = jnp.where(tok == t, r, a)
            a_ref[...] = a

        for t in range(_PEER_T // 2, _PEER_T):
            issue_token(nxt, t)
        w_even = gate_tok[tile_slot, pl.ds(tile_row, _PEER_T), :] * _gelu_exact(a_ref[...])
        w = jnp.concatenate([w_even, pltpu.roll(w_even, 1, axis=1)], axis=0).astype(BF16)
        tok = lax.broadcasted_iota(I32, (_PEER_T, half), 0)
        lo = jnp.zeros((_PEER_T, half), F32)
        hi = jnp.zeros((_PEER_T, half), F32)
        for t in range(_PEER_T):
            vrows = pltpu.bitcast(buf[t, :, half:], BF16)
            o = jnp.dot(w, vrows, preferred_element_type=F32)
            lo = jnp.where(tok == t, o[:_PEER_T], lo)
            hi = jnp.where(tok == t, o[_PEER_T:], hi)
        y = x_ref[...] + jnp.concatenate([lo, hi], axis=1)
        if final_norm:
            ms = jnp.mean(y * y, axis=-1, keepdims=True)
            y = y * lax.rsqrt(ms + EPS) * gf_ref[...]
        o_ref[...] = y

        @pl.when(i == n - 1)
        def _():
            wait_rows(nxt)
            idx_copy(i + 2, cur).wait()

    for parity in range(2):
        pl.when(i % 2 == parity)(functools.partial(step, parity, 1 - parity))


def _peer_route_experts(h, x, wq, keys, uv, g_final, *, final_norm):
    n, d = x.shape
    assert n % _MACRO == 0
    steps = n // _PEER_T
    row = lambda width: pl.BlockSpec((_PEER_T, width), lambda i: (i, 0))
    anyspec = pl.BlockSpec(memory_space=pl.ANY)
    return pl.pallas_call(
        functools.partial(_route_experts_kernel, final_norm=final_norm, n_tiles=n // _MACRO),
        grid=(steps,),
        in_specs=[row(d), row(d), pl.BlockSpec((1, d), lambda i: (0, 0)), anyspec, anyspec, anyspec, anyspec],
        out_specs=row(d),
        out_shape=jax.ShapeDtypeStruct((n, d), F32),
        scratch_shapes=[pltpu.VMEM(wq.shape, BF16),
                        pltpu.VMEM(keys.shape, BF16),
                        pltpu.VMEM((_MACRO, d), F32),
                        pltpu.VMEM((_MACRO, wq.shape[1]), BF16),
                        pltpu.VMEM((PEER_PICKS, _MACRO), F32),
                        pltpu.VMEM((PEER_PICKS, _MACRO), F32),
                        pltpu.VMEM((_MACRO // 128, 2 * PEER_PICKS, 128), F32),
                        pltpu.VMEM((2, _MACRO, PEER_PICKS), I32),
                        pltpu.VMEM((2, _MACRO, 2 * PEER_PICKS), F32),
                        pltpu.SMEM((_PEER_T, PEER_PICKS), I32),
                        pltpu.SMEM((_PEER_T, PEER_PICKS), I32),
                        pltpu.VMEM((_PEER_T, PEER_PICKS, d), I32),
                        pltpu.VMEM((_PEER_T, PEER_PICKS, d), I32),
                        pltpu.VMEM((_PEER_T, 2 * PEER_PICKS), F32),
                        pltpu.SemaphoreType.DMA,
                        pltpu.SemaphoreType.DMA((2,)),
                        pltpu.SemaphoreType.DMA((2,))],
        compiler_params=_cparams(("arbitrary",)),
        name="peer_route_experts",
    )(h, x, g_final.reshape(1, d), h, wq, keys, uv.reshape(-1))


def _encoder(xs, rel_bias, g_attn, w_in, a_sink, g_grp_a, nat_rpb, g_grp_b, lam_q1, lam_k1, lam_q2, lam_k2,
             g_subln, w_out, g_ffn, peer_wq, peer_keys, peer_u, peer_v, g_final):
    depth = w_in.shape[0]
    d = xs[0].shape[-1]
    splits = [int(c) for c in np.cumsum((0,) + IN_SIZES)]
    bias_a = _bias_a_table(rel_bias[:, :A_Q_HEADS])
    tiles_c, far_c = _bias_c_tables(rel_bias[:, A_Q_HEADS:])
    shapes = [x.shape[:2] for x in xs]
    xs = [x.reshape(-1, d) for x in xs]
    for l in range(depth):
        lam_init = 0.8 - 0.6 * math.exp(-0.3 * l)
        w_l = jnp.concatenate([w_in[l][:, splits[s]:splits[s + 1]] for s in _PROJ_ORDER], axis=1).astype(BF16)
        w_o = w_out[l].astype(BF16)
        w_q = peer_wq[l].astype(BF16)
        bias_b = _bias_b_table(nat_rpb[l])
        keys = peer_keys[l].reshape(2 * PEER_HEADS, PEER_KEYS, PEER_HALF).astype(BF16)
        uv = _pack_expert_tables(peer_u[l], peer_v[l])
        for b, (n_seq, seq) in enumerate(shapes):
            x = xs[b]
            n = x.shape[0]
            kw = dict(row0=0, n_seq=n_seq, seq=seq)
            proj = _norm_proj(x, g_attn[l], w_l)
            ya = _mixer_a(proj, bias_a, a_sink[l], g_grp_a[l], **kw)
            yb = _mixer_b(proj, bias_b, g_grp_b[l], **kw)
            yc = _mixer_c(proj, tiles_c, far_c, lam_q1[l], lam_k1[l], lam_q2[l], lam_k2[l], g_subln[l],
                          lam_init=lam_init, **kw)
            x1, h2 = _out_proj(x, ya, yb, yc, w_o, g_ffn[l])
            xs[b] = _peer_route_experts(h2, x1, w_q, keys, uv, g_final, final_norm=(l == depth - 1))
    return tuple(x.reshape(n_seq, seq, d) for x, (n_seq, seq) in zip(xs, shapes))


def kernel(x_prompt, x_sample, rel_bias, g_attn, w_in, a_sink, g_grp_a, nat_rpb, g_grp_b, lam_q1, lam_k1, lam_q2, lam_k2, g_subln, w_out, g_ffn, peer_wq, peer_keys, peer_u, peer_v, g_final):
    return _encoder([x_prompt, x_sample], rel_bias, g_attn, w_in, a_sink, g_grp_a, nat_rpb, g_grp_b, lam_q1, lam_k1,
                    lam_q2, lam_k2, g_subln, w_out, g_ffn, peer_wq, peer_keys, peer_u, peer_v, g_final)
```

```python
import functools
import math

import numpy as np
import jax
import jax.numpy as jnp
from jax import lax
from jax.experimental import pallas as pl
from jax.experimental.pallas import tpu as pltpu

F32 = jnp.float32
BF16 = jnp.bfloat16
I32 = jnp.int32
EPS = 1e-6
NEG = -1e30

D_MODEL = 2048
DEPTH = 2
HEAD_DIM = 128
A_Q_HEADS = 6
A_KV_HEADS = 2
A_GROUP = A_Q_HEADS // A_KV_HEADS
WINDOW = 128
BLOCK = 128
B_HEADS = 6
GRID_W = 64
NB_KH = 8
NB_KW = 16
C_HEADS = 4
C_QK_DIM = 64
C_V_DIM = 128
A_WIDTH = A_Q_HEADS * HEAD_DIM
B_WIDTH = B_HEADS * HEAD_DIM
C_WIDTH = C_HEADS * C_V_DIM
A_KV = A_KV_HEADS * HEAD_DIM
C_QK = C_HEADS * 2 * C_QK_DIM
IN_SIZES = (A_WIDTH, A_KV, A_KV, B_WIDTH, B_WIDTH, B_WIDTH, C_QK, C_QK, C_WIDTH)
IN_WIDTH = sum(IN_SIZES)
REL_BUCKETS = 32
REL_MAX_DIST = 128
PEER_HEADS = 8
PEER_KEYS = 128
PEER_TOPK = 16
PEER_QDIM = 256
PEER_HALF = PEER_QDIM // 2
PEER_PICKS = PEER_HEADS * PEER_TOPK

_PROJ_ORDER = (0, 3, 4, 5, 1, 2, 6, 7, 8)
_COL_AQ, _COL_BQ, _COL_BK, _COL_BV = 0, 1, 2, 3
_COL_AK, _COL_AV = 12, 13
_COL_CQ, _COL_CK, _COL_CV = 7, 8, 9

_VMEM_LIMIT = 56 * 1024 * 1024


def _cparams(sem):
    return pltpu.CompilerParams(dimension_semantics=sem, vmem_limit_bytes=_VMEM_LIMIT)


def _t5_bucket(rel):
    nb = REL_BUCKETS // 2
    max_exact = nb // 2
    ret = jnp.where(rel > 0, nb, 0)
    n = jnp.abs(rel)
    nf = jnp.maximum(n, 1).astype(F32)
    large = max_exact + (jnp.log(nf / max_exact) / math.log(REL_MAX_DIST / max_exact) * (nb - max_exact)).astype(I32)
    large = jnp.minimum(large, nb - 1)
    return ret + jnp.where(n < max_exact, n, large)


def _lookup(table, idx):
    onehot = jax.nn.one_hot(idx, table.shape[0], dtype=F32)
    return jnp.einsum('...b,bc->...c', onehot, table.astype(F32), precision=lax.Precision.HIGHEST)


def _bias_a_table(rel_bias_a):
    rel = np.arange(3 * BLOCK)[None, :] - BLOCK - np.arange(BLOCK)[:, None]
    bias = _lookup(rel_bias_a, _t5_bucket(jnp.asarray(rel)))
    bias = jnp.where(jnp.asarray(np.abs(rel) <= WINDOW)[:, :, None], bias, NEG)
    return bias.transpose(2, 0, 1)


def _bias_b_table(rpb):
    classes = (((0, 1), (0, 0)), ((2, 3), (0, 0)), ((4, 5), (0, 1)), ((5, 6), (1, 1)), ((7, 8), (1, 1)))
    qc = np.arange(GRID_W)[:, None]
    kc = np.arange(GRID_W)[None, :]
    c0 = np.clip(qc - NB_KW // 2, 0, GRID_W - NB_KW)
    in_cols = (kc >= c0) & (kc < c0 + NB_KW)
    n_dc = 2 * NB_KW - 1
    dc_onehot = ((kc - qc + NB_KW - 1)[:, :, None] == np.arange(n_dc)).astype(np.float32)
    blocks = jnp.einsum('qkc,hdc->hdqk', jnp.asarray(dc_onehot), rpb.astype(F32), precision=lax.Precision.HIGHEST)
    blocks = jnp.where(jnp.asarray(in_cols), blocks, NEG)
    neg = jnp.full((rpb.shape[0], GRID_W, GRID_W), NEG, F32)
    tabs = []
    for (rq, r0) in classes:
        rows = []
        for a in range(_B_QROWS):
            parts = [blocks[:, kr - rq[a] + NB_KH - 1] if r0[a] <= kr < r0[a] + NB_KH else neg
                     for kr in range(_B_KROWS)]
            rows.append(jnp.concatenate(parts, axis=-1))
        tabs.append(jnp.concatenate(rows, axis=1))
    return jnp.stack(tabs)


def _bias_c_tables(rel_bias_c):
    q = np.arange(BLOCK)[:, None]
    k = np.arange(BLOCK)[None, :]
    far = _lookup(rel_bias_c, _t5_bucket(jnp.asarray([-2 * BLOCK, 2 * BLOCK])))
    tiles = []
    for d in (-1, 0, 1):
        rel = k + d * BLOCK - q
        t = _lookup(rel_bias_c, _t5_bucket(jnp.asarray(rel))).transpose(2, 0, 1)
        base = far[0] if d < 0 else far[1]
        tiles.append(t - base[:, None, None])
    return jnp.stack(tiles, axis=1), far.T


def _norm_proj_kernel(x_ref, g_ref, w_ref, o_ref, h_ref):
    @pl.when(pl.program_id(1) == 0)
    def _():
        x = x_ref[...]
        ms = jnp.mean(x * x, axis=-1, keepdims=True)
        h_ref[...] = (x * lax.rsqrt(ms + EPS) * g_ref[...]).astype(BF16)

    o_ref[...] = jnp.dot(h_ref[...], w_ref[...], preferred_element_type=F32).astype(o_ref.dtype)


def _norm_proj(x, g, w, *, tm=512, tn=1280):
    n, d = x.shape
    width = w.shape[1]
    return pl.pallas_call(
        _norm_proj_kernel,
        grid=(n // tm, width // tn),
        in_specs=[
            pl.BlockSpec((tm, d), lambda i, j: (i, 0)),
            pl.BlockSpec((1, d), lambda i, j: (0, 0)),
            pl.BlockSpec((d, tn), lambda i, j: (0, j)),
        ],
        out_specs=pl.BlockSpec((tm, tn), lambda i, j: (i, j)),
        out_shape=jax.ShapeDtypeStruct((n, width), BF16),
        scratch_shapes=[pltpu.VMEM((tm, d), BF16)],
        compiler_params=_cparams(("parallel", "arbitrary")),
        name="norm_proj",
    )(x, g.reshape(1, d), w)


def _group_norm_store(o_ref, outs, g_ref):
    o = jnp.concatenate(outs, axis=-1)
    ms = jnp.mean(o * o, axis=-1, keepdims=True)
    o_ref[...] = (o * lax.rsqrt(ms + EPS) * g_ref[...]).astype(o_ref.dtype)


def _mixer_a_kernel(q_ref, kp_ref, kc_ref, kn_ref, vp_ref, vc_ref, vn_ref, bias_ref, sink_ref, g_ref, o_ref):
    i = pl.program_id(1)
    nb = pl.num_programs(1)
    scale = HEAD_DIM ** -0.5
    col = lax.broadcasted_iota(I32, (BLOCK, 3 * BLOCK), 1)
    lo = jnp.where(i == 0, BLOCK, 0)
    hi = jnp.where(i == nb - 1, 2 * BLOCK, 3 * BLOCK)
    in_seq = (col >= lo) & (col < hi)
    ks, vs, scores = [], [], []
    for kv in range(A_KV_HEADS):
        sl = slice(kv * HEAD_DIM, (kv + 1) * HEAD_DIM)
        ks.append(jnp.concatenate([kp_ref[:, sl], kc_ref[:, sl], kn_ref[:, sl]], axis=0))
        vs.append(jnp.concatenate([vp_ref[:, sl], vc_ref[:, sl], vn_ref[:, sl]], axis=0))
    for h in range(A_Q_HEADS):
        q = q_ref[:, h * HEAD_DIM:(h + 1) * HEAD_DIM]
        scores.append(lax.dot_general(q, ks[h // A_GROUP], (((1,), (1,)), ((), ())), preferred_element_type=F32))
    probs, denoms = [], []
    for h in range(A_Q_HEADS):
        s = jnp.where(in_seq, scores[h] * scale + bias_ref[h], NEG)
        sk = sink_ref[h]
        m = jnp.maximum(jnp.max(s, axis=-1, keepdims=True), sk)
        p = jnp.exp(s - m)
        denoms.append(jnp.sum(p, axis=-1, keepdims=True) + jnp.exp(sk - m))
        probs.append(p.astype(BF16))
    outs = [jnp.dot(probs[h], vs[h // A_GROUP], preferred_element_type=F32) / denoms[h] for h in range(A_Q_HEADS)]
    _group_norm_store(o_ref, outs, g_ref)


def _mixer_a(proj, bias, sink, g, *, row0, n_seq, seq):
    nb = seq // BLOCK
    b0 = row0 // BLOCK

    def blk(shift, colblk):
        return lambda b, i: (b0 + b * nb + jnp.clip(i + shift, 0, nb - 1), colblk)

    kv_spec = lambda shift, colblk: pl.BlockSpec((BLOCK, A_KV), blk(shift, colblk))
    return pl.pallas_call(
        _mixer_a_kernel,
        grid=(n_seq, nb),
        in_specs=[
            pl.BlockSpec((BLOCK, A_WIDTH), blk(0, _COL_AQ)),
            kv_spec(-1, _COL_AK), kv_spec(0, _COL_AK), kv_spec(1, _COL_AK),
            kv_spec(-1, _COL_AV), kv_spec(0, _COL_AV), kv_spec(1, _COL_AV),
            pl.BlockSpec((A_Q_HEADS, BLOCK, 3 * BLOCK), lambda b, i: (0, 0, 0)),
            pl.BlockSpec(memory_space=pltpu.SMEM),
            pl.BlockSpec((1, A_WIDTH), lambda b, i: (0, 0)),
        ],
        out_specs=pl.BlockSpec((BLOCK, A_WIDTH), lambda b, i: (b * nb + i, 0)),
        out_shape=jax.ShapeDtypeStruct((n_seq * seq, A_WIDTH), BF16),
        compiler_params=_cparams(("parallel", "parallel")),
        name="mixer_a",
    )(proj, proj, proj, proj, proj, proj, proj, bias, sink, g.reshape(1, A_WIDTH))


_B_QROWS = 2
_B_KROWS = NB_KH + _B_QROWS - 1


def _mixer_b_kernel(q_ref, k_ref, v_ref, bias_ref, g_ref, o_ref, *, rows):
    i = pl.program_id(1)
    scale = HEAD_DIM ** -0.5
    kstart = jnp.clip(_B_QROWS * i - NB_KH // 2, 0, rows - _B_KROWS)
    ks = pl.multiple_of(kstart * GRID_W, GRID_W)
    nk = _B_KROWS * GRID_W
    heads = [slice(h * HEAD_DIM, (h + 1) * HEAD_DIM) for h in range(B_HEADS)]
    scores = [lax.dot_general(q_ref[:, sl], k_ref[pl.ds(ks, nk), sl], (((1,), (1,)), ((), ())),
                              preferred_element_type=F32) for sl in heads]
    probs, denoms = [], []
    for h in range(B_HEADS):
        s = scores[h] * scale + bias_ref[0, h]
        p = jnp.exp(s - jnp.max(s, axis=-1, keepdims=True))
        denoms.append(jnp.sum(p, axis=-1, keepdims=True))
        probs.append(p.astype(BF16))
    outs = [jnp.dot(probs[h], v_ref[pl.ds(ks, nk), sl], preferred_element_type=F32) / denoms[h]
            for h, sl in enumerate(heads)]
    _group_norm_store(o_ref, outs, g_ref)


def _mixer_b(proj, bias, g, *, row0, n_seq, seq):
    rows = seq // GRID_W
    nrp = rows // _B_QROWS
    tq = _B_QROWS * GRID_W
    b0 = row0 // tq
    s0 = row0 // seq

    def cls(i):
        return jnp.where(i == 0, 0, jnp.where(i == 1, 1, jnp.where(i == nrp - 2, 3, jnp.where(i == nrp - 1, 4, 2))))

    return pl.pallas_call(
        functools.partial(_mixer_b_kernel, rows=rows),
        grid=(n_seq, nrp),
        in_specs=[
            pl.BlockSpec((tq, B_WIDTH), lambda b, i: (b0 + b * nrp + i, _COL_BQ)),
            pl.BlockSpec((seq, B_WIDTH), lambda b, i: (s0 + b, _COL_BK)),
            pl.BlockSpec((seq, B_WIDTH), lambda b, i: (s0 + b, _COL_BV)),
            pl.BlockSpec((1, B_HEADS, tq, _B_KROWS * GRID_W), lambda b, i: (cls(i), 0, 0, 0)),
            pl.BlockSpec((1, B_WIDTH), lambda b, i: (0, 0)),
        ],
        out_specs=pl.BlockSpec((tq, B_WIDTH), lambda b, i: (b * nrp + i, 0)),
        out_shape=jax.ShapeDtypeStruct((n_seq * seq, B_WIDTH), BF16),
        compiler_params=_cparams(("parallel", "arbitrary")),
        name="mixer_b",
    )(proj, proj, proj, bias, g.reshape(1, B_WIDTH))


def _mixer_c_kernel(q_ref, k_ref, v_ref, tiles_ref, far_ref, lq1_ref, lk1_ref, lq2_ref, lk2_ref, g_ref, o_ref,
                    *s_refs, seq, lam_init):
    i = pl.program_id(1)
    nb = seq // BLOCK
    scale = C_QK_DIM ** -0.5
    lam = (jnp.exp(jnp.sum(lq1_ref[...] * lk1_ref[...], axis=-1, keepdims=True))
           - jnp.exp(jnp.sum(lq2_ref[...] * lk2_ref[...], axis=-1, keepdims=True)) + lam_init)
    before = lax.broadcasted_iota(I32, (1, seq), 1) < i * BLOCK
    lane = lax.broadcasted_iota(I32, (BLOCK, HEAD_DIM), 1)
    for h in range(C_HEADS):
        sl = slice(h * HEAD_DIM, (h + 1) * HEAD_DIM)
        qh = q_ref[:, sl] * scale
        kh = k_ref[:, sl]
        qs = jnp.concatenate(
            [jnp.where((lane >= c * C_QK_DIM) & (lane < (c + 1) * C_QK_DIM), qh, jnp.zeros_like(qh)) for c in range(2)],
            axis=0)
        s = lax.dot_general(qs, kh, (((1,), (1,)), ((), ())), preferred_element_type=F32)
        for c in range(2):
            hc = 2 * h + c
            s_refs[hc][...] = s[c * BLOCK:(c + 1) * BLOCK] + jnp.where(before, far_ref[hc, 0], far_ref[hc, 1])
    outs = []
    for h in range(C_HEADS):
        sl = slice(h * HEAD_DIM, (h + 1) * HEAD_DIM)
        exps, sums = [], []
        for c in range(2):
            hc = 2 * h + c
            for d in range(3):
                j = i + (d - 1)
                inside = jnp.where((j >= 0) & (j < nb), 1.0, 0.0)
                cols = pl.ds(pl.multiple_of(jnp.clip(j, 0, nb - 1) * BLOCK, BLOCK), BLOCK)
                s_refs[hc][:, cols] += tiles_ref[hc, d] * inside
            s = s_refs[hc][...]
            e = jnp.exp(s - jnp.max(s, axis=-1, keepdims=True))
            exps.append(e)
            sums.append(jnp.sum(e, axis=-1, keepdims=True))
        a = (exps[0] - (lam * sums[0] / sums[1]) * exps[1]).astype(BF16)
        o = jnp.dot(a, v_ref[:, sl], preferred_element_type=F32) / sums[0]
        ms = jnp.mean(o * o, axis=-1, keepdims=True)
        outs.append(o * lax.rsqrt(ms + EPS) * g_ref[...] * (1.0 - lam_init))
    o_ref[...] = jnp.concatenate(outs, axis=-1).astype(o_ref.dtype)


def _mixer_c(proj, tiles, far, lq1, lk1, lq2, lk2, g, *, row0, n_seq, seq, lam_init):
    nb = seq // BLOCK
    b0 = row0 // BLOCK
    s0 = row0 // seq
    vec = lambda a: a.reshape(1, C_QK_DIM)
    vspec = pl.BlockSpec((1, C_QK_DIM), lambda b, i: (0, 0))
    return pl.pallas_call(
        functools.partial(_mixer_c_kernel, seq=seq, lam_init=lam_init),
        grid=(n_seq, nb),
        in_specs=[
            pl.BlockSpec((BLOCK, C_QK), lambda b, i: (b0 + b * nb + i, _COL_CQ)),
            pl.BlockSpec((seq, C_QK), lambda b, i: (s0 + b, _COL_CK)),
            pl.BlockSpec((seq, C_WIDTH), lambda b, i: (s0 + b, _COL_CV)),
            pl.BlockSpec((2 * C_HEADS, 3, BLOCK, BLOCK), lambda b, i: (0, 0, 0, 0)),
            pl.BlockSpec(memory_space=pltpu.SMEM),
            vspec, vspec, vspec, vspec,
            pl.BlockSpec((1, C_V_DIM), lambda b, i: (0, 0)),
        ],
        out_specs=pl.BlockSpec((BLOCK, C_WIDTH), lambda b, i: (b * nb + i, 0)),
        out_shape=jax.ShapeDtypeStruct((n_seq * seq, C_WIDTH), BF16),
        scratch_shapes=[pltpu.VMEM((BLOCK, seq), F32)] * (2 * C_HEADS),
        compiler_params=_cparams(("parallel", "arbitrary")),
        name="mixer_c",
    )(proj, proj, proj, tiles, far, vec(lq1), vec(lk1), vec(lq2), vec(lk2), g.reshape(1, C_V_DIM))


def _out_proj_kernel(x_ref, ya_ref, yb_ref, yc_ref, w_ref, g_ref, x1_ref, h_ref):
    acc = jnp.dot(ya_ref[...], w_ref[0:A_WIDTH, :], preferred_element_type=F32)
    acc += jnp.dot(yb_ref[...], w_ref[A_WIDTH:A_WIDTH + B_WIDTH, :], preferred_element_type=F32)
    acc += jnp.dot(yc_ref[...], w_ref[A_WIDTH + B_WIDTH:, :], preferred_element_type=F32)
    x1 = x_ref[...] + acc
    x1_ref[...] = x1
    ms = jnp.mean(x1 * x1, axis=-1, keepdims=True)
    h_ref[...] = x1 * lax.rsqrt(ms + EPS) * g_ref[...]


def _out_proj(x, ya, yb, yc, w, g, *, tm=256):
    n, d = x.shape
    row = lambda width: pl.BlockSpec((tm, width), lambda i: (i, 0))
    return pl.pallas_call(
        _out_proj_kernel,
        grid=(n // tm,),
        in_specs=[row(d), row(A_WIDTH), row(B_WIDTH), row(C_WIDTH),
                  pl.BlockSpec((d, d), lambda i: (0, 0)),
                  pl.BlockSpec((1, d), lambda i: (0, 0))],
        out_specs=[row(d), row(d)],
        out_shape=[jax.ShapeDtypeStruct((n, d), F32), jax.ShapeDtypeStruct((n, d), F32)],
        compiler_params=_cparams(("parallel",)),
        name="out_proj",
    )(x, ya, yb, yc, w, g.reshape(1, d))


def _top16(x):
    r = float(x.shape[0])
    iota = lax.broadcasted_iota(I32, x.shape, 0).astype(F32)
    vals, idxs = [], []
    for _ in range(PEER_TOPK):
        m = jnp.max(x, axis=0, keepdims=True)
        idx = jnp.min(jnp.where(x == m, iota, r), axis=0, keepdims=True)
        vals.append(m)
        idxs.append(idx)
        x = jnp.where(iota == idx, -jnp.inf, x)
    return vals, idxs


def _peer_candidates(s1, i1, s2, i2):
    sub = 8
    s1a, i1a = jnp.concatenate(s1, axis=0), jnp.concatenate(i1, axis=0)
    s2a, i2a = jnp.concatenate(s2, axis=0), jnp.concatenate(i2, axis=0)
    row = lax.broadcasted_iota(I32, s2a[:sub].shape, 0)
    cand, cidx = [], []
    first_single = PEER_TOPK // 2
    for a in range(first_single):
        nb = PEER_TOPK // (a + 1)
        if nb > sub:
            cand.append(s1[a] + s2a)
            cidx.append(i1[a] * PEER_KEYS + i2a)
        else:
            cand.append(jnp.where(row < nb, s1[a] + s2a[:sub], -jnp.inf))
            cidx.append(i1[a] * PEER_KEYS + i2a[:sub])
    cand.append(s1a[first_single:] + s2[0])
    cidx.append(i1a[first_single:] * PEER_KEYS + i2[0])
    return jnp.concatenate(cand, axis=0), jnp.concatenate(cidx, axis=0)


def _peer_route_kernel(h_ref, wq_ref, keys_ref, eidx_ref, gate_ref):
    q = jnp.dot(h_ref[...].astype(BF16), wq_ref[...], preferred_element_type=F32).astype(BF16)
    tm = q.shape[0]
    for hp in range(PEER_HEADS):
        tops = []
        for c in range(2):
            hc = 2 * hp + c
            qc = q[:, hc * PEER_HALF:(hc + 1) * PEER_HALF]
            sc = lax.dot_general(keys_ref[hc], qc, (((1,), (1,)), ((), ())), preferred_element_type=F32)
            tops.append(_top16(sc))
        (s1, i1), (s2, i2) = tops
        cand, cidx = _peer_candidates(s1, i1, s2, i2)
        top_s, pos = _top16(cand)
        iota = lax.broadcasted_iota(I32, cand.shape, 0).astype(F32)
        eidx = [jnp.sum(jnp.where(iota == p, cidx, 0.0), axis=0, keepdims=True) for p in pos]
        ts = jnp.concatenate(top_s, axis=0)
        e = jnp.exp(ts - top_s[0])
        gate_ref[hp] = e / jnp.sum(e, axis=0, keepdims=True)
        eidx_ref[hp] = jnp.concatenate(eidx, axis=0).astype(I32)


def _peer_route(h, wq, keys, *, tm=256):
    n, d = h.shape
    out_spec = pl.BlockSpec((PEER_HEADS, PEER_TOPK, tm), lambda i: (0, 0, i))
    return pl.pallas_call(
        _peer_route_kernel,
        grid=(n // tm,),
        in_specs=[pl.BlockSpec((tm, d), lambda i: (i, 0)),
                  pl.BlockSpec((d, PEER_HEADS * PEER_QDIM), lambda i: (0, 0)),
                  pl.BlockSpec((2 * PEER_HEADS, PEER_KEYS, PEER_HALF), lambda i: (0, 0, 0))],
        out_specs=[out_spec, out_spec],
        out_shape=[jax.ShapeDtypeStruct((PEER_HEADS, PEER_TOPK, n), I32),
                   jax.ShapeDtypeStruct((PEER_HEADS, PEER_TOPK, n), F32)],
        compiler_params=_cparams(("parallel",)),
        name="peer_route",
    )(h, wq, keys)


def _gelu_exact(x):
    return 0.5 * x * (1.0 + lax.erf(x * (2.0 ** -0.5)))


_PEER_T = 8
_DMA_QUEUES = 2
_PEER_ROWS = _PEER_T * PEER_PICKS


def _pack_expert_tables(u, v):
    half = u.shape[1] // 2
    bits = lambda a: lax.bitcast_convert_type(a.astype(BF16), jnp.uint16).astype(jnp.uint32)
    lo = jnp.concatenate([bits(u[:, :half]), bits(v[:, :half])], axis=1)
    hi = jnp.concatenate([bits(u[:, half:]), bits(v[:, half:])], axis=1)
    return lax.bitcast_convert_type(lo | (hi << 16), I32)


def _peer_expert_kernel(idx_hbm, gate_ref, h_ref, x_ref, gf_ref, uv_hbm, o_ref,
                        idx_smem, buf0, buf1, a_ref, idx_sem, row_sem, *, final_norm):
    i = pl.program_id(0)
    n = pl.num_programs(0)
    bufs = (buf0, buf1)
    half = h_ref.shape[1] // 2

    def idx_copy(step, slot):
        dst = idx_smem.at[pl.ds(slot * _PEER_ROWS, _PEER_ROWS)]
        return pltpu.make_async_copy(idx_hbm.at[step], dst, idx_sem.at[slot])

    def issue_token(slot, t):
        for k in range(PEER_PICKS):
            off = pl.multiple_of(idx_smem[slot * _PEER_ROWS + t * PEER_PICKS + k], D_MODEL)
            copy = pltpu.make_async_copy(uv_hbm.at[pl.ds(off, D_MODEL)], bufs[slot].at[t, k], row_sem.at[slot])
            copy.start(priority=k % _DMA_QUEUES)

    def wait_rows(slot):
        pltpu.make_async_copy(bufs[slot], bufs[slot], row_sem.at[slot]).wait()

    @pl.when(i == 0)
    def _():
        idx_copy(0, 0).start()
        idx_copy(0, 0).wait()
        lax.fori_loop(0, _PEER_T, lambda t, c: (issue_token(0, t), c)[1], 0)
        idx_copy(1, 1).start()

    def step(cur, nxt):
        idx_copy(i + 1, nxt).wait()
        idx_copy(i + 2, cur).start()
        wait_rows(cur)
        buf = bufs[cur]

        h = h_ref[...]
        hh = jnp.concatenate([h[:, :half], h[:, half:]], axis=0).astype(BF16)

        @pl.when(i >= 0)
        def _():
            for t in range(_PEER_T // 2):
                issue_token(nxt, t)
            tok = lax.broadcasted_iota(I32, (_PEER_T, 2 * PEER_PICKS), 0)
            a = jnp.zeros((_PEER_T, 2 * PEER_PICKS), F32)
            for t in range(_PEER_T):
                urows = pltpu.bitcast(buf[t, :, :half], BF16)
                r = lax.dot_general(hh, urows, (((1,), (1,)), ((), ())), preferred_element_type=F32)
                r = r[:_PEER_T] + pltpu.roll(r[_PEER_T:], 2 * PEER_PICKS - 1, axis=1)
                a = jnp.where(tok == t, r, a)
            a_ref[...] = a

        for t in range(_PEER_T // 2, _PEER_T):
            issue_token(nxt, t)
        w_even = gate_ref[...] * _gelu_exact(a_ref[...])
        w = jnp.concatenate([w_even, pltpu.roll(w_even, 1, axis=1)], axis=0).astype(BF16)
        tok = lax.broadcasted_iota(I32, (_PEER_T, half), 0)
        lo = jnp.zeros((_PEER_T, half), F32)
        hi = jnp.zeros((_PEER_T, half), F32)
        for t in range(_PEER_T):
            vrows = pltpu.bitcast(buf[t, :, half:], BF16)
            o = jnp.dot(w, vrows, preferred_element_type=F32)
            lo = jnp.where(tok == t, o[:_PEER_T], lo)
            hi = jnp.where(tok == t, o[_PEER_T:], hi)
        y = x_ref[...] + jnp.concatenate([lo, hi], axis=1)
        if final_norm:
            ms = jnp.mean(y * y, axis=-1, keepdims=True)
            y = y * lax.rsqrt(ms + EPS) * gf_ref[...]
        o_ref[...] = y

        @pl.when(i == n - 1)
        def _():
            wait_rows(nxt)
            idx_copy(i + 2, cur).wait()

    for parity in range(2):
        pl.when(i % 2 == parity)(functools.partial(step, parity, 1 - parity))


def _peer_experts(eidx, gate, h, x, uv, g_final, *, final_norm):
    n, d = x.shape
    steps = n // _PEER_T
    row = lambda width: pl.BlockSpec((_PEER_T, width), lambda i: (i, 0))
    gate2 = jnp.stack([gate, jnp.zeros_like(gate)], axis=-1).reshape(n, 2 * PEER_PICKS)
    return pl.pallas_call(
        functools.partial(_peer_expert_kernel, final_norm=final_norm),
        grid=(steps,),
        in_specs=[pl.BlockSpec(memory_space=pl.ANY), row(2 * PEER_PICKS), row(d), row(d),
                  pl.BlockSpec((1, d), lambda i: (0, 0)),
                  pl.BlockSpec(memory_space=pl.ANY)],
        out_specs=row(d),
        out_shape=jax.ShapeDtypeStruct((n, d), F32),
        scratch_shapes=[pltpu.SMEM((2 * _PEER_ROWS,), I32),
                        pltpu.VMEM((_PEER_T, PEER_PICKS, d), I32),
                        pltpu.VMEM((_PEER_T, PEER_PICKS, d), I32),
                        pltpu.VMEM((_PEER_T, 2 * PEER_PICKS), F32),
                        pltpu.SemaphoreType.DMA((2,)),
                        pltpu.SemaphoreType.DMA((2,))],
        compiler_params=_cparams(("arbitrary",)),
        name="peer_experts",
    )(jnp.pad((eidx * d).reshape(steps, _PEER_ROWS), ((0, 2), (0, 0))), gate2, h, x, g_final.reshape(1, d),
      uv.reshape(-1))


_MACRO = 256
_MACRO_STEPS = _MACRO // _PEER_T
_ROUTE_PHASES = 3 * PEER_HEADS + 2


def _route_experts_kernel(h_ref, x_ref, gf_ref, hall_hbm, wq_hbm, keys_hbm, uv_hbm, o_ref,
                          wq_ref, keys_ref, hbuf, q_ref, sub_s, sub_i, idx_pk, gate_pk, g2_ref, idx_tok, gate_tok,
                          idx_s0, idx_s1, buf0, buf1, a_ref, w_sem, idx_sem, row_sem, *, final_norm, n_tiles):
    i = pl.program_id(0)
    n = pl.num_programs(0)
    bufs = (buf0, buf1)
    idx_smem = (idx_s0, idx_s1)
    half = h_ref.shape[1] // 2

    def phase_project(mt):
        cp = pltpu.make_async_copy(hall_hbm.at[pl.ds(pl.multiple_of(mt * _MACRO, _MACRO), _MACRO)], hbuf, w_sem)
        cp.start()
        cp.wait()
        q_ref[...] = jnp.dot(hbuf[...].astype(BF16), wq_ref[...], preferred_element_type=F32).astype(BF16)

    def phase_half(hc, mt):
        del mt
        qc = q_ref[:, hc * PEER_HALF:(hc + 1) * PEER_HALF]
        sc = lax.dot_general(keys_ref[hc], qc, (((1,), (1,)), ((), ())), preferred_element_type=F32)
        vals, idxs = _top16(sc)
        sub_s[hc % 2] = jnp.concatenate(vals, axis=0)
        sub_i[hc % 2] = jnp.concatenate(idxs, axis=0)

    def phase_head(hp, mt):
        del mt
        rows16 = lambda ref, c: [ref[c, j:j + 1, :] for j in range(PEER_TOPK)]
        s1, i1, s2, i2 = rows16(sub_s, 0), rows16(sub_i, 0), rows16(sub_s, 1), rows16(sub_i, 1)
        cand, cidx = _peer_candidates(s1, i1, s2, i2)
        top_s, pos = _top16(cand)
        iota = lax.broadcasted_iota(I32, cand.shape, 0).astype(F32)
        eidx = [jnp.sum(jnp.where(iota == p, cidx, 0.0), axis=0, keepdims=True) for p in pos]
        ts = jnp.concatenate(top_s, axis=0)
        e = jnp.exp(ts - top_s[0])
        rows = slice(hp * PEER_TOPK, (hp + 1) * PEER_TOPK)
        gate_pk[rows, :] = e / jnp.sum(e, axis=0, keepdims=True)
        idx_pk[rows, :] = jnp.concatenate(eidx, axis=0)

    def phase_finalize(mt):
        slot = mt % 2
        lanes = 128
        for hf in range(_MACRO // lanes):
            toks = slice(hf * lanes, (hf + 1) * lanes)
            g = gate_pk[:, toks]
            g2_ref[hf, pl.ds(0, PEER_PICKS, stride=2), :] = g
            g2_ref[hf, pl.ds(1, PEER_PICKS, stride=2), :] = jnp.zeros_like(g)
            gate_tok[slot, toks, :] = g2_ref[hf].T
            idx_tok[slot, toks, :] = (idx_pk[:, toks].astype(I32) * D_MODEL).T

    phases = [phase_project]
    for hp in range(PEER_HEADS):
        phases += [functools.partial(phase_half, 2 * hp), functools.partial(phase_half, 2 * hp + 1),
                   functools.partial(phase_head, hp)]
    phases.append(phase_finalize)
    assert len(phases) == _ROUTE_PHASES <= _MACRO_STEPS

    def run_phase(ph, mt):
        lax.switch(ph, [lambda mt, f=f: f(mt) for f in phases], mt)

    def idx_copy(step, slot):
        src = idx_tok.at[(step // _MACRO_STEPS) % 2, pl.ds(pl.multiple_of((step % _MACRO_STEPS) * _PEER_T, _PEER_T), _PEER_T)]
        return pltpu.make_async_copy(src, idx_smem[slot], idx_sem.at[slot])

    def issue_token(slot, t):
        for k in range(PEER_PICKS):
            off = pl.multiple_of(idx_smem[slot][t, k], D_MODEL)
            copy = pltpu.make_async_copy(uv_hbm.at[pl.ds(off, D_MODEL)], bufs[slot].at[t, k], row_sem.at[slot])
            copy.start(priority=k % _DMA_QUEUES)

    def wait_rows(slot):
        pltpu.make_async_copy(bufs[slot], bufs[slot], row_sem.at[slot]).wait()

    @pl.when(i == 0)
    def _():
        cw = pltpu.make_async_copy(wq_hbm, wq_ref, w_sem)
        cw.start()
        cw.wait()
        ck = pltpu.make_async_copy(keys_hbm, keys_ref, w_sem)
        ck.start()
        ck.wait()

    def routed(r, carry):
        first = (i == 0) & (r < _ROUTE_PHASES)
        ph = jnp.where(first, r, i % _MACRO_STEPS)
        mt = jnp.where(first, 0, i // _MACRO_STEPS + 1)

        @pl.when(first | ((ph < _ROUTE_PHASES) & (mt < n_tiles)))
        def _():
            run_phase(ph, mt)
        return carry
    lax.fori_loop(0, jnp.where(i == 0, _ROUTE_PHASES + 1, 1), routed, 0)

    @pl.when(i == 0)
    def _():
        idx_copy(0, 0).start()
        idx_copy(0, 0).wait()
        lax.fori_loop(0, _PEER_T, lambda t, c: (issue_token(0, t), c)[1], 0)
        idx_copy(1, 1).start()

    def step(cur, nxt):
        idx_copy(i + 1, nxt).wait()
        idx_copy(i + 2, cur).start()
        wait_rows(cur)
        buf = bufs[cur]
        tile_slot = (i // _MACRO_STEPS) % 2
        tile_row = pl.multiple_of((i % _MACRO_STEPS) * _PEER_T, _PEER_T)

        h = h_ref[...]
        hh = jnp.concatenate([h[:, :half], h[:, half:]], axis=0).astype(BF16)

        @pl.when(i >= 0)
        def _():
            for t in range(_PEER_T // 2):
                issue_token(nxt, t)
            tok = lax.broadcasted_iota(I32, (_PEER_T, 2 * PEER_PICKS), 0)
            a = jnp.zeros((_PEER_T, 2 * PEER_PICKS), F32)
            for t in range(_PEER_T):
                urows = pltpu.bitcast(buf[t, :, :half], BF16)
                r = lax.dot_general(hh, urows, (((1,), (1,)), ((), ())), preferred_element_type=F32)
                r = r[:_PEER_T] + pltpu.roll(r[_PEER_T:], 2 * PEER_PICKS - 1, axis=1)
                a = jnp.where(tok == t, r, a)
            a_ref[...] = a

        for t in range(_PEER_T // 2, _PEER_T):
            issue_token(nxt, t)
        w_even = gate_tok[tile_slot, pl.ds(tile_row, _PEER_T), :] * _gelu_exact(a_ref[...])
        w = jnp.concatenate([w_even, pltpu.roll(w_even, 1, axis=1)], axis=0).astype(BF16)
        tok = lax.broadcasted_iota(I32, (_PEER_T, half), 0)
        lo = jnp.zeros((_PEER_T, half), F32)
        hi = jnp.zeros((_PEER_T, half), F32)
        for t in range(_PEER_T):
            vrows = pltpu.bitcast(buf[t, :, half:], BF16)
            o = jnp.dot(w, vrows, preferred_element_type=F32)
            lo = jnp.where(tok == t, o[:_PEER_T], lo)
            hi = jnp.where(tok == t, o[_PEER_T:], hi)
        y = x_ref[...] + jnp.concatenate([lo, hi], axis=1)
        if final_norm:
            ms = jnp.mean(y * y, axis=-1, keepdims=True)
            y = y * lax.rsqrt(ms + EPS) * gf_ref[...]
        o_ref[...] = y

        @pl.when(i == n - 1)
        def _():
            wait_rows(nxt)
            idx_copy(i + 2, cur).wait()

    for parity in range(2):
        pl.when(i % 2 == parity)(functools.partial(step, parity, 1 - parity))


def _peer_route_experts(h, x, wq, keys, uv, g_final, *, final_norm):
    n, d = x.shape
    assert n % _MACRO == 0
    steps = n // _PEER_T
    row = lambda width: pl.BlockSpec((_PEER_T, width), lambda i: (i, 0))
    anyspec = pl.BlockSpec(memory_space=pl.ANY)
    return pl.pallas_call(
        functools.partial(_route_experts_kernel, final_norm=final_norm, n_tiles=n // _MACRO),
        grid=(steps,),
        in_specs=[row(d), row(d), pl.BlockSpec((1, d), lambda i: (0, 0)), anyspec, anyspec, anyspec, anyspec],
        out_specs=row(d),
        out_shape=jax.ShapeDtypeStruct((n, d), F32),
        scratch_shapes=[pltpu.VMEM(wq.shape, BF16),
                        pltpu.VMEM(keys.shape, BF16),
                        pltpu.VMEM((_MACRO, d), F32),
                        pltpu.VMEM((_MACRO, wq.shape[1]), BF16),
                        pltpu.VMEM((2, PEER_TOPK, _MACRO), F32),
                        pltpu.VMEM((2, PEER_TOPK, _MACRO), F32),
                        pltpu.VMEM((PEER_PICKS, _MACRO), F32),
                        pltpu.VMEM((PEER_PICKS, _MACRO), F32),
                        pltpu.VMEM((_MACRO // 128, 2 * PEER_PICKS, 128), F32),
                        pltpu.VMEM((2, _MACRO, PEER_PICKS), I32),
                        pltpu.VMEM((2, _MACRO, 2 * PEER_PICKS), F32),
                        pltpu.SMEM((_PEER_T, PEER_PICKS), I32),
                        pltpu.SMEM((_PEER_T, PEER_PICKS), I32),
                        pltpu.VMEM((_PEER_T, PEER_PICKS, d), I32),
                        pltpu.VMEM((_PEER_T, PEER_PICKS, d), I32),
                        pltpu.VMEM((_PEER_T, 2 * PEER_PICKS), F32),
                        pltpu.SemaphoreType.DMA,
                        pltpu.SemaphoreType.DMA((2,)),
                        pltpu.SemaphoreType.DMA((2,))],
        compiler_params=_cparams(("arbitrary",)),
        name="peer_route_experts",
    )(h, x, g_final.reshape(1, d), h, wq, keys, uv.reshape(-1))


def _encoder(xs, rel_bias, g_attn, w_in, a_sink, g_grp_a, nat_rpb, g_grp_b, lam_q1, lam_k1, lam_q2, lam_k2,
             g_subln, w_out, g_ffn, peer_wq, peer_keys, peer_u, peer_v, g_final):
    depth = w_in.shape[0]
    d = xs[0].shape[-1]
    splits = [int(c) for c in np.cumsum((0,) + IN_SIZES)]
    bias_a = _bias_a_table(rel_bias[:, :A_Q_HEADS])
    tiles_c, far_c = _bias_c_tables(rel_bias[:, A_Q_HEADS:])
    shapes = [x.shape[:2] for x in xs]
    xs = [x.reshape(-1, d) for x in xs]
    for l in range(depth):
        lam_init = 0.8 - 0.6 * math.exp(-0.3 * l)
        w_l = jnp.concatenate([w_in[l][:, splits[s]:splits[s + 1]] for s in _PROJ_ORDER], axis=1).astype(BF16)
        w_o = w_out[l].astype(BF16)
        w_q = peer_wq[l].astype(BF16)
        bias_b = _bias_b_table(nat_rpb[l])
        keys = peer_keys[l].reshape(2 * PEER_HEADS, PEER_KEYS, PEER_HALF).astype(BF16)
        uv = _pack_expert_tables(peer_u[l], peer_v[l])
        for b, (n_seq, seq) in enumerate(shapes):
            x = xs[b]
            n = x.shape[0]
            kw = dict(row0=0, n_seq=n_seq, seq=seq)
            proj = _norm_proj(x, g_attn[l], w_l)
            ya = _mixer_a(proj, bias_a, a_sink[l], g_grp_a[l], **kw)
            yb = _mixer_b(proj, bias_b, g_grp_b[l], **kw)
            yc = _mixer_c(proj, tiles_c, far_c, lam_q1[l], lam_k1[l], lam_q2[l], lam_k2[l], g_subln[l],
                          lam_init=lam_init, **kw)
            x1, h2 = _out_proj(x, ya, yb, yc, w_o, g_ffn[l])
            xs[b] = _peer_route_experts(h2, x1, w_q, keys, uv, g_final, final_norm=(l == depth - 1))
    return tuple(x.reshape(n_seq, seq, d) for x, (n_seq, seq) in zip(xs, shapes))


def kernel(x_prompt, x_sample, rel_bias, g_attn, w_in, a_sink, g_grp_a, nat_rpb, g_grp_b, lam_q1, lam_k1, lam_q2, lam_k2, g_subln, w_out, g_ffn, peer_wq, peer_keys, peer_u, peer_v, g_final):
    return _encoder([x_prompt, x_sample], rel_bias, g_attn, w_in, a_sink, g_grp_a, nat_rpb, g_grp_b, lam_q1, lam_k1,
                    lam_q2, lam_k2, g_subln, w_out, g_ffn, peer_wq, peer_keys, peer_u, peer_v, g_final)
```

```python
import functools
import math

import numpy as np
import jax
import jax.numpy as jnp
from jax import lax
from jax.experimental import pallas as pl
from jax.experimental.pallas import tpu as pltpu

F32 = jnp.float32
BF16 = jnp.bfloat16
I32 = jnp.int32
EPS = 1e-6
NEG = -1e30

D_MODEL = 2048
DEPTH = 2
HEAD_DIM = 128
A_Q_HEADS = 6
A_KV_HEADS = 2
A_GROUP = A_Q_HEADS // A_KV_HEADS
WINDOW = 128
BLOCK = 128
B_HEADS = 6
GRID_W = 64
NB_KH = 8
NB_KW = 16
C_HEADS = 4
C_QK_DIM = 64
C_V_DIM = 128
A_WIDTH = A_Q_HEADS * HEAD_DIM
B_WIDTH = B_HEADS * HEAD_DIM
C_WIDTH = C_HEADS * C_V_DIM
A_KV = A_KV_HEADS * HEAD_DIM
C_QK = C_HEADS * 2 * C_QK_DIM
IN_SIZES = (A_WIDTH, A_KV, A_KV, B_WIDTH, B_WIDTH, B_WIDTH, C_QK, C_QK, C_WIDTH)
IN_WIDTH = sum(IN_SIZES)
REL_BUCKETS = 32
REL_MAX_DIST = 128
PEER_HEADS = 8
PEER_KEYS = 128
PEER_TOPK = 16
PEER_QDIM = 256
PEER_HALF = PEER_QDIM // 2
PEER_PICKS = PEER_HEADS * PEER_TOPK

_PROJ_ORDER = (0, 3, 4, 5, 1, 2, 6, 7, 8)
_COL_AQ, _COL_BQ, _COL_BK, _COL_BV = 0, 1, 2, 3
_COL_AK, _COL_AV = 12, 13
_COL_CQ, _COL_CK, _COL_CV = 7, 8, 9

_VMEM_LIMIT = 56 * 1024 * 1024


def _cparams(sem):
    return pltpu.CompilerParams(dimension_semantics=sem, vmem_limit_bytes=_VMEM_LIMIT)


def _t5_bucket(rel):
    nb = REL_BUCKETS // 2
    max_exact = nb // 2
    ret = jnp.where(rel > 0, nb, 0)
    n = jnp.abs(rel)
    nf = jnp.maximum(n, 1).astype(F32)
    large = max_exact + (jnp.log(nf / max_exact) / math.log(REL_MAX_DIST / max_exact) * (nb - max_exact)).astype(I32)
    large = jnp.minimum(large, nb - 1)
    return ret + jnp.where(n < max_exact, n, large)


def _lookup(table, idx):
    onehot = jax.nn.one_hot(idx, table.shape[0], dtype=F32)
    return jnp.einsum('...b,bc->...c', onehot, table.astype(F32), precision=lax.Precision.HIGHEST)


def _bias_a_table(rel_bias_a):
    rel = np.arange(3 * BLOCK)[None, :] - BLOCK - np.arange(BLOCK)[:, None]
    bias = _lookup(rel_bias_a, _t5_bucket(jnp.asarray(rel)))
    bias = jnp.where(jnp.asarray(np.abs(rel) <= WINDOW)[:, :, None], bias, NEG)
    return bias.transpose(2, 0, 1)


def _bias_b_table(rpb):
    classes = (((0, 1), (0, 0)), ((2, 3), (0, 0)), ((4, 5), (0, 1)), ((5, 6), (1, 1)), ((7, 8), (1, 1)))
    qc = np.arange(GRID_W)[:, None]
    kc = np.arange(GRID_W)[None, :]
    c0 = np.clip(qc - NB_KW // 2, 0, GRID_W - NB_KW)
    in_cols = (kc >= c0) & (kc < c0 + NB_KW)
    n_dc = 2 * NB_KW - 1
    dc_onehot = ((kc - qc + NB_KW - 1)[:, :, None] == np.arange(n_dc)).astype(np.float32)
    blocks = jnp.einsum('qkc,hdc->hdqk', jnp.asarray(dc_onehot), rpb.astype(F32), precision=lax.Precision.HIGHEST)
    blocks = jnp.where(jnp.asarray(in_cols), blocks, NEG)
    neg = jnp.full((rpb.shape[0], GRID_W, GRID_W), NEG, F32)
    tabs = []
    for (rq, r0) in classes:
        rows = []
        for a in range(_B_QROWS):
            parts = [blocks[:, kr - rq[a] + NB_KH - 1] if r0[a] <= kr < r0[a] + NB_KH else neg
                     for kr in range(_B_KROWS)]
            rows.append(jnp.concatenate(parts, axis=-1))
        tabs.append(jnp.concatenate(rows, axis=1))
    return jnp.stack(tabs)


def _bias_c_tables(rel_bias_c):
    q = np.arange(BLOCK)[:, None]
    k = np.arange(BLOCK)[None, :]
    far = _lookup(rel_bias_c, _t5_bucket(jnp.asarray([-2 * BLOCK, 2 * BLOCK])))
    tiles = []
    for d in (-1, 0, 1):
        rel = k + d * BLOCK - q
        t = _lookup(rel_bias_c, _t5_bucket(jnp.asarray(rel))).transpose(2, 0, 1)
        base = far[0] if d < 0 else far[1]
        tiles.append(t - base[:, None, None])
    return jnp.stack(tiles, axis=1), far.T


def _norm_proj_kernel(x_ref, g_ref, w_ref, o_ref, h_ref):
    @pl.when(pl.program_id(1) == 0)
    def _():
        x = x_ref[...]
        ms = jnp.mean(x * x, axis=-1, keepdims=True)
        h_ref[...] = (x * lax.rsqrt(ms + EPS) * g_ref[...]).astype(BF16)

    o_ref[...] = jnp.dot(h_ref[...], w_ref[...], preferred_element_type=F32).astype(o_ref.dtype)


def _norm_proj(x, g, w, *, tm=512, tn=1280):
    n, d = x.shape
    width = w.shape[1]
    return pl.pallas_call(
        _norm_proj_kernel,
        grid=(n // tm, width // tn),
        in_specs=[
            pl.BlockSpec((tm, d), lambda i, j: (i, 0)),
            pl.BlockSpec((1, d), lambda i, j: (0, 0)),
            pl.BlockSpec((d, tn), lambda i, j: (0, j)),
        ],
        out_specs=pl.BlockSpec((tm, tn), lambda i, j: (i, j)),
        out_shape=jax.ShapeDtypeStruct((n, width), BF16),
        scratch_shapes=[pltpu.VMEM((tm, d), BF16)],
        compiler_params=_cparams(("parallel", "arbitrary")),
        name="norm_proj",
    )(x, g.reshape(1, d), w)


def _group_norm_store(o_ref, outs, g_ref):
    o = jnp.concatenate(outs, axis=-1)
    ms = jnp.mean(o * o, axis=-1, keepdims=True)
    o_ref[...] = (o * lax.rsqrt(ms + EPS) * g_ref[...]).astype(o_ref.dtype)


def _mixer_a_kernel(q_ref, kp_ref, kc_ref, kn_ref, vp_ref, vc_ref, vn_ref, bias_ref, sink_ref, g_ref, o_ref):
    i = pl.program_id(1)
    nb = pl.num_programs(1)
    scale = HEAD_DIM ** -0.5
    col = lax.broadcasted_iota(I32, (BLOCK, 3 * BLOCK), 1)
    lo = jnp.where(i == 0, BLOCK, 0)
    hi = jnp.where(i == nb - 1, 2 * BLOCK, 3 * BLOCK)
    in_seq = (col >= lo) & (col < hi)
    ks, vs, scores = [], [], []
    for kv in range(A_KV_HEADS):
        sl = slice(kv * HEAD_DIM, (kv + 1) * HEAD_DIM)
        ks.append(jnp.concatenate([kp_ref[:, sl], kc_ref[:, sl], kn_ref[:, sl]], axis=0))
        vs.append(jnp.concatenate([vp_ref[:, sl], vc_ref[:, sl], vn_ref[:, sl]], axis=0))
    for h in range(A_Q_HEADS):
        q = q_ref[:, h * HEAD_DIM:(h + 1) * HEAD_DIM]
        scores.append(lax.dot_general(q, ks[h // A_GROUP], (((1,), (1,)), ((), ())), preferred_element_type=F32))
    probs, denoms = [], []
    for h in range(A_Q_HEADS):
        s = jnp.where(in_seq, scores[h] * scale + bias_ref[h], NEG)
        sk = sink_ref[h]
        m = jnp.maximum(jnp.max(s, axis=-1, keepdims=True), sk)
        p = jnp.exp(s - m)
        denoms.append(jnp.sum(p, axis=-1, keepdims=True) + jnp.exp(sk - m))
        probs.append(p.astype(BF16))
    outs = [jnp.dot(probs[h], vs[h // A_GROUP], preferred_element_type=F32) / denoms[h] for h in range(A_Q_HEADS)]
    _group_norm_store(o_ref, outs, g_ref)


def _mixer_a(proj, bias, sink, g, *, row0, n_seq, seq):
    nb = seq // BLOCK
    b0 = row0 // BLOCK

    def blk(shift, colblk):
        return lambda b, i: (b0 + b * nb + jnp.clip(i + shift, 0, nb - 1), colblk)

    kv_spec = lambda shift, colblk: pl.BlockSpec((BLOCK, A_KV), blk(shift, colblk))
    return pl.pallas_call(
        _mixer_a_kernel,
        grid=(n_seq, nb),
        in_specs=[
            pl.BlockSpec((BLOCK, A_WIDTH), blk(0, _COL_AQ)),
            kv_spec(-1, _COL_AK), kv_spec(0, _COL_AK), kv_spec(1, _COL_AK),
            kv_spec(-1, _COL_AV), kv_spec(0, _COL_AV), kv_spec(1, _COL_AV),
            pl.BlockSpec((A_Q_HEADS, BLOCK, 3 * BLOCK), lambda b, i: (0, 0, 0)),
            pl.BlockSpec(memory_space=pltpu.SMEM),
            pl.BlockSpec((1, A_WIDTH), lambda b, i: (0, 0)),
        ],
        out_specs=pl.BlockSpec((BLOCK, A_WIDTH), lambda b, i: (b * nb + i, 0)),
        out_shape=jax.ShapeDtypeStruct((n_seq * seq, A_WIDTH), BF16),
        compiler_params=_cparams(("parallel", "parallel")),
        name="mixer_a",
    )(proj, proj, proj, proj, proj, proj, proj, bias, sink, g.reshape(1, A_WIDTH))


_B_QROWS = 2
_B_KROWS = NB_KH + _B_QROWS - 1


def _mixer_b_kernel(q_ref, k_ref, v_ref, bias_ref, g_ref, o_ref, *, rows):
    i = pl.program_id(1)
    scale = HEAD_DIM ** -0.5
    kstart = jnp.clip(_B_QROWS * i - NB_KH // 2, 0, rows - _B_KROWS)
    ks = pl.multiple_of(kstart * GRID_W, GRID_W)
    nk = _B_KROWS * GRID_W
    heads = [slice(h * HEAD_DIM, (h + 1) * HEAD_DIM) for h in range(B_HEADS)]
    scores = [lax.dot_general(q_ref[:, sl], k_ref[pl.ds(ks, nk), sl], (((1,), (1,)), ((), ())),
                              preferred_element_type=F32) for sl in heads]
    probs, denoms = [], []
    for h in range(B_HEADS):
        s = scores[h] * scale + bias_ref[0, h]
        p = jnp.exp(s - jnp.max(s, axis=-1, keepdims=True))
        denoms.append(jnp.sum(p, axis=-1, keepdims=True))
        probs.append(p.astype(BF16))
    outs = [jnp.dot(probs[h], v_ref[pl.ds(ks, nk), sl], preferred_element_type=F32) / denoms[h]
            for h, sl in enumerate(heads)]
    _group_norm_store(o_ref, outs, g_ref)


def _mixer_b(proj, bias, g, *, row0, n_seq, seq):
    rows = seq // GRID_W
    nrp = rows // _B_QROWS
    tq = _B_QROWS * GRID_W
    b0 = row0 // tq
    s0 = row0 // seq

    def cls(i):
        return jnp.where(i == 0, 0, jnp.where(i == 1, 1, jnp.where(i == nrp - 2, 3, jnp.where(i == nrp - 1, 4, 2))))

    return pl.pallas_call(
        functools.partial(_mixer_b_kernel, rows=rows),
        grid=(n_seq, nrp),
        in_specs=[
            pl.BlockSpec((tq, B_WIDTH), lambda b, i: (b0 + b * nrp + i, _COL_BQ)),
            pl.BlockSpec((seq, B_WIDTH), lambda b, i: (s0 + b, _COL_BK)),
            pl.BlockSpec((seq, B_WIDTH), lambda b, i: (s0 + b, _COL_BV)),
            pl.BlockSpec((1, B_HEADS, tq, _B_KROWS * GRID_W), lambda b, i: (cls(i), 0, 0, 0)),
            pl.BlockSpec((1, B_WIDTH), lambda b, i: (0, 0)),
        ],
        out_specs=pl.BlockSpec((tq, B_WIDTH), lambda b, i: (b * nrp + i, 0)),
        out_shape=jax.ShapeDtypeStruct((n_seq * seq, B_WIDTH), BF16),
        compiler_params=_cparams(("parallel", "arbitrary")),
        name="mixer_b",
    )(proj, proj, proj, bias, g.reshape(1, B_WIDTH))


def _mixer_c_kernel(q_ref, k_ref, v_ref, tiles_ref, far_ref, lq1_ref, lk1_ref, lq2_ref, lk2_ref, g_ref, o_ref,
                    *s_refs, seq, lam_init):
    i = pl.program_id(1)
    nb = seq // BLOCK
    scale = C_QK_DIM ** -0.5
    lam = (jnp.exp(jnp.sum(lq1_ref[...] * lk1_ref[...], axis=-1, keepdims=True))
           - jnp.exp(jnp.sum(lq2_ref[...] * lk2_ref[...], axis=-1, keepdims=True)) + lam_init)
    before = lax.broadcasted_iota(I32, (1, seq), 1) < i * BLOCK
    lane = lax.broadcasted_iota(I32, (BLOCK, HEAD_DIM), 1)
    for h in range(C_HEADS):
        sl = slice(h * HEAD_DIM, (h + 1) * HEAD_DIM)
        qh = q_ref[:, sl] * scale
        kh = k_ref[:, sl]
        qs = jnp.concatenate(
            [jnp.where((lane >= c * C_QK_DIM) & (lane < (c + 1) * C_QK_DIM), qh, jnp.zeros_like(qh)) for c in range(2)],
            axis=0)
        s = lax.dot_general(qs, kh, (((1,), (1,)), ((), ())), preferred_element_type=F32)
        for c in range(2):
            hc = 2 * h + c
            s_refs[hc][...] = s[c * BLOCK:(c + 1) * BLOCK] + jnp.where(before, far_ref[hc, 0], far_ref[hc, 1])
    outs = []
    for h in range(C_HEADS):
        sl = slice(h * HEAD_DIM, (h + 1) * HEAD_DIM)
        exps, sums = [], []
        for c in range(2):
            hc = 2 * h + c
            for d in range(3):
                j = i + (d - 1)
                inside = jnp.where((j >= 0) & (j < nb), 1.0, 0.0)
                cols = pl.ds(pl.multiple_of(jnp.clip(j, 0, nb - 1) * BLOCK, BLOCK), BLOCK)
                s_refs[hc][:, cols] += tiles_ref[hc, d] * inside
            s = s_refs[hc][...]
            e = jnp.exp(s - jnp.max(s, axis=-1, keepdims=True))
            exps.append(e)
            sums.append(jnp.sum(e, axis=-1, keepdims=True))
        a = (exps[0] - (lam * sums[0] / sums[1]) * exps[1]).astype(BF16)
        o = jnp.dot(a, v_ref[:, sl], preferred_element_type=F32) / sums[0]
        ms = jnp.mean(o * o, axis=-1, keepdims=True)
        outs.append(o * lax.rsqrt(ms + EPS) * g_ref[...] * (1.0 - lam_init))
    o_ref[...] = jnp.concatenate(outs, axis=-1).astype(o_ref.dtype)


def _mixer_c(proj, tiles, far, lq1, lk1, lq2, lk2, g, *, row0, n_seq, seq, lam_init):
    nb = seq // BLOCK
    b0 = row0 // BLOCK
    s0 = row0 // seq
    vec = lambda a: a.reshape(1, C_QK_DIM)
    vspec = pl.BlockSpec((1, C_QK_DIM), lambda b, i: (0, 0))
    return pl.pallas_call(
        functools.partial(_mixer_c_kernel, seq=seq, lam_init=lam_init),
        grid=(n_seq, nb),
        in_specs=[
            pl.BlockSpec((BLOCK, C_QK), lambda b, i: (b0 + b * nb + i, _COL_CQ)),
            pl.BlockSpec((seq, C_QK), lambda b, i: (s0 + b, _COL_CK)),
            pl.BlockSpec((seq, C_WIDTH), lambda b, i: (s0 + b, _COL_CV)),
            pl.BlockSpec((2 * C_HEADS, 3, BLOCK, BLOCK), lambda b, i: (0, 0, 0, 0)),
            pl.BlockSpec(memory_space=pltpu.SMEM),
            vspec, vspec, vspec, vspec,
            pl.BlockSpec((1, C_V_DIM), lambda b, i: (0, 0)),
        ],
        out_specs=pl.BlockSpec((BLOCK, C_WIDTH), lambda b, i: (b * nb + i, 0)),
        out_shape=jax.ShapeDtypeStruct((n_seq * seq, C_WIDTH), BF16),
        scratch_shapes=[pltpu.VMEM((BLOCK, seq), F32)] * (2 * C_HEADS),
        compiler_params=_cparams(("parallel", "arbitrary")),
        name="mixer_c",
    )(proj, proj, proj, tiles, far, vec(lq1), vec(lk1), vec(lq2), vec(lk2), g.reshape(1, C_V_DIM))


def _out_proj_kernel(x_ref, ya_ref, yb_ref, yc_ref, w_ref, g_ref, x1_ref, h_ref):
    acc = jnp.dot(ya_ref[...], w_ref[0:A_WIDTH, :], preferred_element_type=F32)
    acc += jnp.dot(yb_ref[...], w_ref[A_WIDTH:A_WIDTH + B_WIDTH, :], preferred_element_type=F32)
    acc += jnp.dot(yc_ref[...], w_ref[A_WIDTH + B_WIDTH:, :], preferred_element_type=F32)
    x1 = x_ref[...] + acc
    x1_ref[...] = x1
    ms = jnp.mean(x1 * x1, axis=-1, keepdims=True)
    h_ref[...] = x1 * lax.rsqrt(ms + EPS) * g_ref[...]


def _out_proj(x, ya, yb, yc, w, g, *, tm=256):
    n, d = x.shape
    row = lambda width: pl.BlockSpec((tm, width), lambda i: (i, 0))
    return pl.pallas_call(
        _out_proj_kernel,
        grid=(n // tm,),
        in_specs=[row(d), row(A_WIDTH), row(B_WIDTH), row(C_WIDTH),
                  pl.BlockSpec((d, d), lambda i: (0, 0)),
                  pl.BlockSpec((1, d), lambda i: (0, 0))],
        out_specs=[row(d), row(d)],
        out_shape=[jax.ShapeDtypeStruct((n, d), F32), jax.ShapeDtypeStruct((n, d), F32)],
        compiler_params=_cparams(("parallel",)),
        name="out_proj",
    )(x, ya, yb, yc, w, g.reshape(1, d))


def _top16(x):
    r = float(x.shape[0])
    iota = lax.broadcasted_iota(I32, x.shape, 0).astype(F32)
    vals, idxs = [], []
    for _ in range(PEER_TOPK):
        m = jnp.max(x, axis=0, keepdims=True)
        idx = jnp.min(jnp.where(x == m, iota, r), axis=0, keepdims=True)
        vals.append(m)
        idxs.append(idx)
        x = jnp.where(iota == idx, -jnp.inf, x)
    return vals, idxs


def _peer_candidates(s1, i1, s2, i2):
    sub = 8
    s1a, i1a = jnp.concatenate(s1, axis=0), jnp.concatenate(i1, axis=0)
    s2a, i2a = jnp.concatenate(s2, axis=0), jnp.concatenate(i2, axis=0)
    row = lax.broadcasted_iota(I32, s2a[:sub].shape, 0)
    cand, cidx = [], []
    first_single = PEER_TOPK // 2
    for a in range(first_single):
        nb = PEER_TOPK // (a + 1)
        if nb > sub:
            cand.append(s1[a] + s2a)
            cidx.append(i1[a] * PEER_KEYS + i2a)
        else:
            cand.append(jnp.where(row < nb, s1[a] + s2a[:sub], -jnp.inf))
            cidx.append(i1[a] * PEER_KEYS + i2a[:sub])
    cand.append(s1a[first_single:] + s2[0])
    cidx.append(i1a[first_single:] * PEER_KEYS + i2[0])
    return jnp.concatenate(cand, axis=0), jnp.concatenate(cidx, axis=0)


def _gelu_exact(x):
    return 0.5 * x * (1.0 + lax.erf(x * (2.0 ** -0.5)))


_PEER_T = 8
_DMA_QUEUES = 2


def _pack_expert_tables(u, v):
    half = u.shape[1] // 2
    bits = lambda a: lax.bitcast_convert_type(a.astype(BF16), jnp.uint16).astype(jnp.uint32)
    lo = jnp.concatenate([bits(u[:, :half]), bits(v[:, :half])], axis=1)
    hi = jnp.concatenate([bits(u[:, half:]), bits(v[:, half:])], axis=1)
    return lax.bitcast_convert_type(lo | (hi << 16), I32)


_MACRO = 256
_MACRO_STEPS = _MACRO // _PEER_T
_ROUTE_PHASES = 3 * PEER_HEADS + 2


def _route_experts_kernel(h_ref, x_ref, gf_ref, hall_hbm, wq_hbm, keys_hbm, uv_hbm, o_ref,
                          wq_ref, keys_ref, hbuf, q_ref, sub_s, sub_i, idx_pk, gate_pk, g2_ref, idx_tok, gate_tok,
                          idx_s0, idx_s1, buf0, buf1, a_ref, w_sem, idx_sem, row_sem, *, final_norm, n_tiles):
    i = pl.program_id(0)
    n = pl.num_programs(0)
    bufs = (buf0, buf1)
    idx_smem = (idx_s0, idx_s1)
    half = h_ref.shape[1] // 2

    def phase_project(mt):
        cp = pltpu.make_async_copy(hall_hbm.at[pl.ds(pl.multiple_of(mt * _MACRO, _MACRO), _MACRO)], hbuf, w_sem)
        cp.start()
        cp.wait()
        q_ref[...] = jnp.dot(hbuf[...].astype(BF16), wq_ref[...], preferred_element_type=F32).astype(BF16)

    def phase_half(hc, mt):
        del mt
        qc = q_ref[:, hc * PEER_HALF:(hc + 1) * PEER_HALF]
        sc = lax.dot_general(keys_ref[hc], qc, (((1,), (1,)), ((), ())), preferred_element_type=F32)
        vals, idxs = _top16(sc)
        sub_s[hc % 2] = jnp.concatenate(vals, axis=0)
        sub_i[hc % 2] = jnp.concatenate(idxs, axis=0)

    def phase_head(hp, mt):
        del mt
        rows16 = lambda ref, c: [ref[c, j:j + 1, :] for j in range(PEER_TOPK)]
        s1, i1, s2, i2 = rows16(sub_s, 0), rows16(sub_i, 0), rows16(sub_s, 1), rows16(sub_i, 1)
        cand, cidx = _peer_candidates(s1, i1, s2, i2)
        top_s, pos = _top16(cand)
        iota = lax.broadcasted_iota(I32, cand.shape, 0).astype(F32)
        eidx = [jnp.sum(jnp.where(iota == p, cidx, 0.0), axis=0, keepdims=True) for p in pos]
        ts = jnp.concatenate(top_s, axis=0)
        e = jnp.exp(ts - top_s[0])
        rows = slice(hp * PEER_TOPK, (hp + 1) * PEER_TOPK)
        gate_pk[rows, :] = e / jnp.sum(e, axis=0, keepdims=True)
        idx_pk[rows, :] = jnp.concatenate(eidx, axis=0)

    def phase_finalize(mt):
        slot = mt % 2
        lanes = 128
        for hf in range(_MACRO // lanes):
            toks = slice(hf * lanes, (hf + 1) * lanes)
            g = gate_pk[:, toks]
            g2_ref[hf, pl.ds(0, PEER_PICKS, stride=2), :] = g
            g2_ref[hf, pl.ds(1, PEER_PICKS, stride=2), :] = jnp.zeros_like(g)
            gate_tok[slot, toks, :] = g2_ref[hf].T
            idx_tok[slot, toks, :] = (idx_pk[:, toks].astype(I32) * D_MODEL).T

    phases = [phase_project]
    for hp in range(PEER_HEADS):
        phases += [functools.partial(phase_half, 2 * hp), functools.partial(phase_half, 2 * hp + 1),
                   functools.partial(phase_head, hp)]
    phases.append(phase_finalize)
    assert len(phases) == _ROUTE_PHASES <= _MACRO_STEPS

    def run_phase(ph, mt):
        lax.switch(ph, [lambda mt, f=f: f(mt) for f in phases], mt)

    def idx_copy(step, slot):
        src = idx_tok.at[(step // _MACRO_STEPS) % 2, pl.ds(pl.multiple_of((step % _MACRO_STEPS) * _PEER_T, _PEER_T), _PEER_T)]
        return pltpu.make_async_copy(src, idx_smem[slot], idx_sem.at[slot])

    def issue_token(slot, t):
        for k in range(PEER_PICKS):
            off = pl.multiple_of(idx_smem[slot][t, k], D_MODEL)
            copy = pltpu.make_async_copy(uv_hbm.at[pl.ds(off, D_MODEL)], bufs[slot].at[t, k], row_sem.at[slot])
            copy.start(priority=k % _DMA_QUEUES)

    def wait_rows(slot):
        pltpu.make_async_copy(bufs[slot], bufs[slot], row_sem.at[slot]).wait()

    @pl.when(i == 0)
    def _():
        cw = pltpu.make_async_copy(wq_hbm, wq_ref, w_sem)
        cw.start()
        cw.wait()
        ck = pltpu.make_async_copy(keys_hbm, keys_ref, w_sem)
        ck.start()
        ck.wait()

    def routed(r, carry):
        first = (i == 0) & (r < _ROUTE_PHASES)
        ph = jnp.where(first, r, i % _MACRO_STEPS)
        mt = jnp.where(first, 0, i // _MACRO_STEPS + 1)

        @pl.when(first | ((ph < _ROUTE_PHASES) & (mt < n_tiles)))
        def _():
            run_phase(ph, mt)
        return carry
    lax.fori_loop(0, jnp.where(i == 0, _ROUTE_PHASES + 1, 1), routed, 0)

    @pl.when(i == 0)
    def _():
        idx_copy(0, 0).start()
        idx_copy(0, 0).wait()
        lax.fori_loop(0, _PEER_T, lambda t, c: (issue_token(0, t), c)[1], 0)
        idx_copy(1, 1).start()

    def step(cur, nxt):
        idx_copy(i + 1, nxt).wait()
        idx_copy(i + 2, cur).start()
        wait_rows(cur)
        buf = bufs[cur]
        tile_slot = (i // _MACRO_STEPS) % 2
        tile_row = pl.multiple_of((i % _MACRO_STEPS) * _PEER_T, _PEER_T)

        h = h_ref[...]
        hh = jnp.concatenate([h[:, :half], h[:, half:]], axis=0).astype(BF16)

        @pl.when(i >= 0)
        def _():
            for t in range(_PEER_T // 2):
                issue_token(nxt, t)
            tok = lax.broadcasted_iota(I32, (_PEER_T, 2 * PEER_PICKS), 0)
            a = jnp.zeros((_PEER_T, 2 * PEER_PICKS), F32)
            for t in range(_PEER_T):
                urows = pltpu.bitcast(buf[t, :, :half], BF16)
                r = lax.dot_general(hh, urows, (((1,), (1,)), ((), ())), preferred_element_type=F32)
                r = r[:_PEER_T] + pltpu.roll(r[_PEER_T:], 2 * PEER_PICKS - 1, axis=1)
                a = jnp.where(tok == t, r, a)
            a_ref[...] = a

        for t in range(_PEER_T // 2, _PEER_T):
            issue_token(nxt, t)
        w_even = gate_tok[tile_slot, pl.ds(tile_row, _PEER_T), :] * _gelu_exact(a_ref[...])
        w = jnp.concatenate([w_even, pltpu.roll(w_even, 1, axis=1)], axis=0).astype(BF16)
        tok = lax.broadcasted_iota(I32, (_PEER_T, half), 0)
        lo = jnp.zeros((_PEER_T, half), F32)
        hi = jnp.zeros((_PEER_T, half), F32)
        for t in range(_PEER_T):
            vrows = pltpu.bitcast(buf[t, :, half:], BF16)
            o = jnp.dot(w, vrows, preferred_element_type=F32)
            lo = jnp.where(tok == t, o[:_PEER_T], lo)
            hi = jnp.where(tok == t, o[_PEER_T:], hi)
        y = x_ref[...] + jnp.concatenate([lo, hi], axis=1)
        if final_norm:
            ms = jnp.mean(y * y, axis=-1, keepdims=True)
            y = y * lax.rsqrt(ms + EPS) * gf_ref[...]
        o_ref[...] = y

        @pl.when(i == n - 1)
        def _():
            wait_rows(nxt)
            idx_copy(i + 2, cur).wait()

    for parity in range(2):
        pl.when(i % 2 == parity)(functools.partial(step, parity, 1 - parity))


def _peer_route_experts(h, x, wq, keys, uv, g_final, *, final_norm):
    n, d = x.shape
    assert n % _MACRO == 0
    steps = n // _PEER_T
    row = lambda width: pl.BlockSpec((_PEER_T, width), lambda i: (i, 0))
    anyspec = pl.BlockSpec(memory_space=pl.ANY)
    return pl.pallas_call(
        functools.partial(_route_experts_kernel, final_norm=final_norm, n_tiles=n // _MACRO),
        grid=(steps,),
        in_specs=[row(d), row(d), pl.BlockSpec((1, d), lambda i: (0, 0)), anyspec, anyspec, anyspec, anyspec],
        out_specs=row(d),
        out_shape=jax.ShapeDtypeStruct((n, d), F32),
        scratch_shapes=[pltpu.VMEM(wq.shape, BF16),
                        pltpu.VMEM(keys.shape, BF16),
                        pltpu.VMEM((_MACRO, d), F32),
                        pltpu.VMEM((_MACRO, wq.shape[1]), BF16),
                        pltpu.VMEM((2, PEER_TOPK, _MACRO), F32),
                        pltpu.VMEM((2, PEER_TOPK, _MACRO), F32),
                        pltpu.VMEM((PEER_PICKS, _MACRO), F32),
                        pltpu.VMEM((PEER_PICKS, _MACRO), F32),
                        pltpu.VMEM((_MACRO // 128, 2 * PEER_PICKS, 128), F32),
                        pltpu.VMEM((2, _MACRO, PEER_PICKS), I32),
                        pltpu.VMEM((2, _MACRO, 2 * PEER_PICKS), F32),
                        pltpu.SMEM((_PEER_T, PEER_PICKS), I32),
                        pltpu.SMEM((_PEER_T, PEER_PICKS), I32),
                        pltpu.VMEM((_PEER_T, PEER_PICKS, d), I32),
                        pltpu.VMEM((_PEER_T, PEER_PICKS, d), I32),
                        pltpu.VMEM((_PEER_T, 2 * PEER_PICKS), F32),
                        pltpu.SemaphoreType.DMA,
                        pltpu.SemaphoreType.DMA((2,)),
                        pltpu.SemaphoreType.DMA((2,))],
        compiler_params=_cparams(("arbitrary",)),
        name="peer_route_experts",
    )(h, x, g_final.reshape(1, d), h, wq, keys, uv.reshape(-1))


def _encoder(xs, rel_bias, g_attn, w_in, a_sink, g_grp_a, nat_rpb, g_grp_b, lam_q1, lam_k1, lam_q2, lam_k2,
             g_subln, w_out, g_ffn, peer_wq, peer_keys, peer_u, peer_v, g_final):
    depth = w_in.shape[0]
    d = xs[0].shape[-1]
    splits = [int(c) for c in np.cumsum((0,) + IN_SIZES)]
    bias_a = _bias_a_table(rel_bias[:, :A_Q_HEADS])
    tiles_c, far_c = _bias_c_tables(rel_bias[:, A_Q_HEADS:])
    shapes = [x.shape[:2] for x in xs]
    xs = [x.reshape(-1, d) for x in xs]
    for l in range(depth):
        lam_init = 0.8 - 0.6 * math.exp(-0.3 * l)
        w_l = jnp.concatenate([w_in[l][:, splits[s]:splits[s + 1]] for s in _PROJ_ORDER], axis=1).astype(BF16)
        w_o = w_out[l].astype(BF16)
        w_q = peer_wq[l].astype(BF16)
        bias_b = _bias_b_table(nat_rpb[l])
        keys = peer_keys[l].reshape(2 * PEER_HEADS, PEER_KEYS, PEER_HALF).astype(BF16)
        uv = _pack_expert_tables(peer_u[l], peer_v[l])
        for b, (n_seq, seq) in enumerate(shapes):
            x = xs[b]
            n = x.shape[0]
            kw = dict(row0=0, n_seq=n_seq, seq=seq)
            proj = _norm_proj(x, g_attn[l], w_l)
            ya = _mixer_a(proj, bias_a, a_sink[l], g_grp_a[l], **kw)
            yb = _mixer_b(proj, bias_b, g_grp_b[l], **kw)
            yc = _mixer_c(proj, tiles_c, far_c, lam_q1[l], lam_k1[l], lam_q2[l], lam_k2[l], g_subln[l],
                          lam_init=lam_init, **kw)
            x1, h2 = _out_proj(x, ya, yb, yc, w_o, g_ffn[l])
            xs[b] = _peer_route_experts(h2, x1, w_q, keys, uv, g_final, final_norm=(l == depth - 1))
    return tuple(x.reshape(n_seq, seq, d) for x, (n_seq, seq) in zip(xs, shapes))


def kernel(x_prompt, x_sample, rel_bias, g_attn, w_in, a_sink, g_grp_a, nat_rpb, g_grp_b, lam_q1, lam_k1, lam_q2, lam_k2, g_subln, w_out, g_ffn, peer_wq, peer_keys, peer_u, peer_v, g_final):
    return _encoder([x_prompt, x_sample], rel_bias, g_attn, w_in, a_sink, g_grp_a, nat_rpb, g_grp_b, lam_q1, lam_k1,
                    lam_q2, lam_k2, g_subln, w_out, g_ffn, peer_wq, peer_keys, peer_u, peer_v, g_final)
```

```python
import functools
import math

import numpy as np
import jax
import jax.numpy as jnp
from jax import lax
from jax.experimental import pallas as pl
from jax.experimental.pallas import tpu as pltpu

F32 = jnp.float32
BF16 = jnp.bfloat16
I32 = jnp.int32
EPS = 1e-6
NEG = -1e30

D_MODEL = 2048
DEPTH = 2
HEAD_DIM = 128
A_Q_HEADS = 6
A_KV_HEADS = 2
A_GROUP = A_Q_HEADS // A_KV_HEADS
WINDOW = 128
BLOCK = 128
B_HEADS = 6
GRID_W = 64
NB_KH = 8
NB_KW = 16
C_HEADS = 4
C_QK_DIM = 64
C_V_DIM = 128
A_WIDTH = A_Q_HEADS * HEAD_DIM
B_WIDTH = B_HEADS * HEAD_DIM
C_WIDTH = C_HEADS * C_V_DIM
A_KV = A_KV_HEADS * HEAD_DIM
C_QK = C_HEADS * 2 * C_QK_DIM
IN_SIZES = (A_WIDTH, A_KV, A_KV, B_WIDTH, B_WIDTH, B_WIDTH, C_QK, C_QK, C_WIDTH)
IN_WIDTH = sum(IN_SIZES)
REL_BUCKETS = 32
REL_MAX_DIST = 128
PEER_HEADS = 8
PEER_KEYS = 128
PEER_TOPK = 16
PEER_QDIM = 256
PEER_HALF = PEER_QDIM // 2
PEER_PICKS = PEER_HEADS * PEER_TOPK

_PROJ_ORDER = (0, 3, 4, 5, 1, 2, 6, 7, 8)
_COL_AQ, _COL_BQ, _COL_BK, _COL_BV = 0, 1, 2, 3
_COL_AK, _COL_AV = 12, 13
_COL_CQ, _COL_CK, _COL_CV = 7, 8, 9

_VMEM_LIMIT = 56 * 1024 * 1024


def _cparams(sem):
    return pltpu.CompilerParams(dimension_semantics=sem, vmem_limit_bytes=_VMEM_LIMIT)


def _t5_bucket(rel):
    nb = REL_BUCKETS // 2
    max_exact = nb // 2
    ret = jnp.where(rel > 0, nb, 0)
    n = jnp.abs(rel)
    nf = jnp.maximum(n, 1).astype(F32)
    large = max_exact + (jnp.log(nf / max_exact) / math.log(REL_MAX_DIST / max_exact) * (nb - max_exact)).astype(I32)
    large = jnp.minimum(large, nb - 1)
    return ret + jnp.where(n < max_exact, n, large)


def _lookup(table, idx):
    onehot = jax.nn.one_hot(idx, table.shape[0], dtype=F32)
    return jnp.einsum('...b,bc->...c', onehot, table.astype(F32), precision=lax.Precision.HIGHEST)


def _bias_a_table(rel_bias_a):
    rel = np.arange(3 * BLOCK)[None, :] - BLOCK - np.arange(BLOCK)[:, None]
    bias = _lookup(rel_bias_a, _t5_bucket(jnp.asarray(rel)))
    bias = jnp.where(jnp.asarray(np.abs(rel) <= WINDOW)[:, :, None], bias, NEG)
    return bias.transpose(2, 0, 1)


def _bias_b_table(rpb):
    classes = (((0, 1), (0, 0)), ((2, 3), (0, 0)), ((4, 5), (0, 1)), ((5, 6), (1, 1)), ((7, 8), (1, 1)))
    qc = np.arange(GRID_W)[:, None]
    kc = np.arange(GRID_W)[None, :]
    c0 = np.clip(qc - NB_KW // 2, 0, GRID_W - NB_KW)
    in_cols = (kc >= c0) & (kc < c0 + NB_KW)
    n_dc = 2 * NB_KW - 1
    dc_onehot = ((kc - qc + NB_KW - 1)[:, :, None] == np.arange(n_dc)).astype(np.float32)
    blocks = jnp.einsum('qkc,hdc->hdqk', jnp.asarray(dc_onehot), rpb.astype(F32), precision=lax.Precision.HIGHEST)
    blocks = jnp.where(jnp.asarray(in_cols), blocks, NEG)
    neg = jnp.full((rpb.shape[0], GRID_W, GRID_W), NEG, F32)
    tabs = []
    for (rq, r0) in classes:
        rows = []
        for a in range(_B_QROWS):
            parts = [blocks[:, kr - rq[a] + NB_KH - 1] if r0[a] <= kr < r0[a] + NB_KH else neg
                     for kr in range(_B_KROWS)]
            rows.append(jnp.concatenate(parts, axis=-1))
        tabs.append(jnp.concatenate(rows, axis=1))
    return jnp.stack(tabs)


def _bias_c_tables(rel_bias_c):
    q = np.arange(BLOCK)[:, None]
    k = np.arange(BLOCK)[None, :]
    far = _lookup(rel_bias_c, _t5_bucket(jnp.asarray([-2 * BLOCK, 2 * BLOCK])))
    tiles = []
    for d in (-1, 0, 1):
        rel = k + d * BLOCK - q
        t = _lookup(rel_bias_c, _t5_bucket(jnp.asarray(rel))).transpose(2, 0, 1)
        base = far[0] if d < 0 else far[1]
        tiles.append(t - base[:, None, None])
    return jnp.stack(tiles, axis=1), far.T


def _norm_proj_kernel(x_ref, g_ref, w_ref, o_ref, h_ref):
    @pl.when(pl.program_id(1) == 0)
    def _():
        x = x_ref[...]
        ms = jnp.mean(x * x, axis=-1, keepdims=True)
        h_ref[...] = (x * lax.rsqrt(ms + EPS) * g_ref[...]).astype(BF16)

    o_ref[...] = jnp.dot(h_ref[...], w_ref[...], preferred_element_type=F32).astype(o_ref.dtype)


def _norm_proj(x, g, w, *, tm=512, tn=1280):
    n, d = x.shape
    width = w.shape[1]
    return pl.pallas_call(
        _norm_proj_kernel,
        grid=(n // tm, width // tn),
        in_specs=[
            pl.BlockSpec((tm, d), lambda i, j: (i, 0)),
            pl.BlockSpec((1, d), lambda i, j: (0, 0)),
            pl.BlockSpec((d, tn), lambda i, j: (0, j)),
        ],
        out_specs=pl.BlockSpec((tm, tn), lambda i, j: (i, j)),
        out_shape=jax.ShapeDtypeStruct((n, width), BF16),
        scratch_shapes=[pltpu.VMEM((tm, d), BF16)],
        compiler_params=_cparams(("parallel", "arbitrary")),
        name="norm_proj",
    )(x, g.reshape(1, d), w)


def _group_norm_store(o_ref, outs, g_ref):
    o = jnp.concatenate(outs, axis=-1)
    ms = jnp.mean(o * o, axis=-1, keepdims=True)
    o_ref[...] = (o * lax.rsqrt(ms + EPS) * g_ref[...]).astype(o_ref.dtype)


def _mixer_a_kernel(q_ref, kp_ref, kc_ref, kn_ref, vp_ref, vc_ref, vn_ref, bias_ref, sink_ref, g_ref, o_ref):
    i = pl.program_id(1)
    nb = pl.num_programs(1)
    scale = HEAD_DIM ** -0.5
    col = lax.broadcasted_iota(I32, (BLOCK, 3 * BLOCK), 1)
    lo = jnp.where(i == 0, BLOCK, 0)
    hi = jnp.where(i == nb - 1, 2 * BLOCK, 3 * BLOCK)
    in_seq = (col >= lo) & (col < hi)
    ks, vs, scores = [], [], []
    for kv in range(A_KV_HEADS):
        sl = slice(kv * HEAD_DIM, (kv + 1) * HEAD_DIM)
        ks.append(jnp.concatenate([kp_ref[:, sl], kc_ref[:, sl], kn_ref[:, sl]], axis=0))
        vs.append(jnp.concatenate([vp_ref[:, sl], vc_ref[:, sl], vn_ref[:, sl]], axis=0))
    for h in range(A_Q_HEADS):
        q = q_ref[:, h * HEAD_DIM:(h + 1) * HEAD_DIM]
        scores.append(lax.dot_general(q, ks[h // A_GROUP], (((1,), (1,)), ((), ())), preferred_element_type=F32))
    probs, denoms = [], []
    for h in range(A_Q_HEADS):
        s = jnp.where(in_seq, scores[h] * scale + bias_ref[h], NEG)
        sk = sink_ref[h]
        m = jnp.maximum(jnp.max(s, axis=-1, keepdims=True), sk)
        p = jnp.exp(s - m)
        denoms.append(jnp.sum(p, axis=-1, keepdims=True) + jnp.exp(sk - m))
        probs.append(p.astype(BF16))
    outs = [jnp.dot(probs[h], vs[h // A_GROUP], preferred_element_type=F32) / denoms[h] for h in range(A_Q_HEADS)]
    _group_norm_store(o_ref, outs, g_ref)


def _mixer_a(proj, bias, sink, g, *, row0, n_seq, seq):
    nb = seq // BLOCK
    b0 = row0 // BLOCK

    def blk(shift, colblk):
        return lambda b, i: (b0 + b * nb + jnp.clip(i + shift, 0, nb - 1), colblk)

    kv_spec = lambda shift, colblk: pl.BlockSpec((BLOCK, A_KV), blk(shift, colblk))
    return pl.pallas_call(
        _mixer_a_kernel,
        grid=(n_seq, nb),
        in_specs=[
            pl.BlockSpec((BLOCK, A_WIDTH), blk(0, _COL_AQ)),
            kv_spec(-1, _COL_AK), kv_spec(0, _COL_AK), kv_spec(1, _COL_AK),
            kv_spec(-1, _COL_AV), kv_spec(0, _COL_AV), kv_spec(1, _COL_AV),
            pl.BlockSpec((A_Q_HEADS, BLOCK, 3 * BLOCK), lambda b, i: (0, 0, 0)),
            pl.BlockSpec(memory_space=pltpu.SMEM),
            pl.BlockSpec((1, A_WIDTH), lambda b, i: (0, 0)),
        ],
        out_specs=pl.BlockSpec((BLOCK, A_WIDTH), lambda b, i: (b * nb + i, 0)),
        out_shape=jax.ShapeDtypeStruct((n_seq * seq, A_WIDTH), BF16),
        compiler_params=_cparams(("parallel", "parallel")),
        name="mixer_a",
    )(proj, proj, proj, proj, proj, proj, proj, bias, sink, g.reshape(1, A_WIDTH))


_B_QROWS = 2
_B_KROWS = NB_KH + _B_QROWS - 1


def _mixer_b_kernel(q_ref, k_ref, v_ref, bias_ref, g_ref, o_ref, *, rows):
    i = pl.program_id(1)
    scale = HEAD_DIM ** -0.5
    kstart = jnp.clip(_B_QROWS * i - NB_KH // 2, 0, rows - _B_KROWS)
    ks = pl.multiple_of(kstart * GRID_W, GRID_W)
    nk = _B_KROWS * GRID_W
    heads = [slice(h * HEAD_DIM, (h + 1) * HEAD_DIM) for h in range(B_HEADS)]
    scores = [lax.dot_general(q_ref[:, sl], k_ref[pl.ds(ks, nk), sl], (((1,), (1,)), ((), ())),
                              preferred_element_type=F32) for sl in heads]
    probs, denoms = [], []
    for h in range(B_HEADS):
        s = scores[h] * scale + bias_ref[0, h]
        p = jnp.exp(s - jnp.max(s, axis=-1, keepdims=True))
        denoms.append(jnp.sum(p, axis=-1, keepdims=True))
        probs.append(p.astype(BF16))
    outs = [jnp.dot(probs[h], v_ref[pl.ds(ks, nk), sl], preferred_element_type=F32) / denoms[h]
            for h, sl in enumerate(heads)]
    _group_norm_store(o_ref, outs, g_ref)


def _mixer_b(proj, bias, g, *, row0, n_seq, seq):
    rows = seq // GRID_W
    nrp = rows // _B_QROWS
    tq = _B_QROWS * GRID_W
    b0 = row0 // tq
    s0 = row0 // seq

    def cls(i):
        return jnp.where(i == 0, 0, jnp.where(i == 1, 1, jnp.where(i == nrp - 2, 3, jnp.where(i == nrp - 1, 4, 2))))

    return pl.pallas_call(
        functools.partial(_mixer_b_kernel, rows=rows),
        grid=(n_seq, nrp),
        in_specs=[
            pl.BlockSpec((tq, B_WIDTH), lambda b, i: (b0 + b * nrp + i, _COL_BQ)),
            pl.BlockSpec((seq, B_WIDTH), lambda b, i: (s0 + b, _COL_BK)),
            pl.BlockSpec((seq, B_WIDTH), lambda b, i: (s0 + b, _COL_BV)),
            pl.BlockSpec((1, B_HEADS, tq, _B_KROWS * GRID_W), lambda b, i: (cls(i), 0, 0, 0)),
            pl.BlockSpec((1, B_WIDTH), lambda b, i: (0, 0)),
        ],
        out_specs=pl.BlockSpec((tq, B_WIDTH), lambda b, i: (b * nrp + i, 0)),
        out_shape=jax.ShapeDtypeStruct((n_seq * seq, B_WIDTH), BF16),
        compiler_params=_cparams(("parallel", "arbitrary")),
        name="mixer_b",
    )(proj, proj, proj, bias, g.reshape(1, B_WIDTH))


def _mixer_c_kernel(q_ref, k_ref, v_ref, tiles_ref, far_ref, lq1_ref, lk1_ref, lq2_ref, lk2_ref, g_ref, o_ref,
                    *s_refs, seq, lam_init):
    i = pl.program_id(1)
    nb = seq // BLOCK
    scale = C_QK_DIM ** -0.5
    lam = (jnp.exp(jnp.sum(lq1_ref[...] * lk1_ref[...], axis=-1, keepdims=True))
           - jnp.exp(jnp.sum(lq2_ref[...] * lk2_ref[...], axis=-1, keepdims=True)) + lam_init)
    before = lax.broadcasted_iota(I32, (1, seq), 1) < i * BLOCK
    lane = lax.broadcasted_iota(I32, (BLOCK, HEAD_DIM), 1)
    for h in range(C_HEADS):
        sl = slice(h * HEAD_DIM, (h + 1) * HEAD_DIM)
        qh = q_ref[:, sl] * scale
        kh = k_ref[:, sl]
        qs = jnp.concatenate(
            [jnp.where((lane >= c * C_QK_DIM) & (lane < (c + 1) * C_QK_DIM), qh, jnp.zeros_like(qh)) for c in range(2)],
            axis=0)
        s = lax.dot_general(qs, kh, (((1,), (1,)), ((), ())), preferred_element_type=F32)
        for c in range(2):
            hc = 2 * h + c
            s_refs[hc][...] = s[c * BLOCK:(c + 1) * BLOCK] + jnp.where(before, far_ref[hc, 0], far_ref[hc, 1])
    outs = []
    for h in range(C_HEADS):
        sl = slice(h * HEAD_DIM, (h + 1) * HEAD_DIM)
        exps, sums = [], []
        for c in range(2):
            hc = 2 * h + c
            for d in range(3):
                j = i + (d - 1)
                inside = jnp.where((j >= 0) & (j < nb), 1.0, 0.0)
                cols = pl.ds(pl.multiple_of(jnp.clip(j, 0, nb - 1) * BLOCK, BLOCK), BLOCK)
                s_refs[hc][:, cols] += tiles_ref[hc, d] * inside
            s = s_refs[hc][...]
            e = jnp.exp(s - jnp.max(s, axis=-1, keepdims=True))
            exps.append(e)
            sums.append(jnp.sum(e, axis=-1, keepdims=True))
        a = (exps[0] - (lam * sums[0] / sums[1]) * exps[1]).astype(BF16)
        o = jnp.dot(a, v_ref[:, sl], preferred_element_type=F32) / sums[0]
        ms = jnp.mean(o * o, axis=-1, keepdims=True)
        outs.append(o * lax.rsqrt(ms + EPS) * g_ref[...] * (1.0 - lam_init))
    o_ref[...] = jnp.concatenate(outs, axis=-1).astype(o_ref.dtype)


def _mixer_c(proj, tiles, far, lq1, lk1, lq2, lk2, g, *, row0, n_seq, seq, lam_init):
    nb = seq // BLOCK
    b0 = row0 // BLOCK
    s0 = row0 // seq
    vec = lambda a: a.reshape(1, C_QK_DIM)
    vspec = pl.BlockSpec((1, C_QK_DIM), lambda b, i: (0, 0))
    return pl.pallas_call(
        functools.partial(_mixer_c_kernel, seq=seq, lam_init=lam_init),
        grid=(n_seq, nb),
        in_specs=[
            pl.BlockSpec((BLOCK, C_QK), lambda b, i: (b0 + b * nb + i, _COL_CQ)),
            pl.BlockSpec((seq, C_QK), lambda b, i: (s0 + b, _COL_CK)),
            pl.BlockSpec((seq, C_WIDTH), lambda b, i: (s0 + b, _COL_CV)),
            pl.BlockSpec((2 * C_HEADS, 3, BLOCK, BLOCK), lambda b, i: (0, 0, 0, 0)),
            pl.BlockSpec(memory_space=pltpu.SMEM),
            vspec, vspec, vspec, vspec,
            pl.BlockSpec((1, C_V_DIM), lambda b, i: (0, 0)),
        ],
        out_specs=pl.BlockSpec((BLOCK, C_WIDTH), lambda b, i: (b * nb + i, 0)),
        out_shape=jax.ShapeDtypeStruct((n_seq * seq, C_WIDTH), BF16),
        scratch_shapes=[pltpu.VMEM((BLOCK, seq), F32)] * (2 * C_HEADS),
        compiler_params=_cparams(("parallel", "arbitrary")),
        name="mixer_c",
    )(proj, proj, proj, tiles, far, vec(lq1), vec(lk1), vec(lq2), vec(lk2), g.reshape(1, C_V_DIM))


def _out_proj_kernel(x_ref, ya_ref, yb_ref, yc_ref, w_ref, g_ref, x1_ref, h_ref):
    acc = jnp.dot(ya_ref[...], w_ref[0:A_WIDTH, :], preferred_element_type=F32)
    acc += jnp.dot(yb_ref[...], w_ref[A_WIDTH:A_WIDTH + B_WIDTH, :], preferred_element_type=F32)
    acc += jnp.dot(yc_ref[...], w_ref[A_WIDTH + B_WIDTH:, :], preferred_element_type=F32)
    x1 = x_ref[...] + acc
    x1_ref[...] = x1
    ms = jnp.mean(x1 * x1, axis=-1, keepdims=True)
    h_ref[...] = x1 * lax.rsqrt(ms + EPS) * g_ref[...]


def _out_proj(x, ya, yb, yc, w, g, *, tm=256):
    n, d = x.shape
    row = lambda width: pl.BlockSpec((tm, width), lambda i: (i, 0))
    return pl.pallas_call(
        _out_proj_kernel,
        grid=(n // tm,),
        in_specs=[row(d), row(A_WIDTH), row(B_WIDTH), row(C_WIDTH),
                  pl.BlockSpec((d, d), lambda i: (0, 0)),
                  pl.BlockSpec((1, d), lambda i: (0, 0))],
        out_specs=[row(d), row(d)],
        out_shape=[jax.ShapeDtypeStruct((n, d), F32), jax.ShapeDtypeStruct((n, d), F32)],
        compiler_params=_cparams(("parallel",)),
        name="out_proj",
    )(x, ya, yb, yc, w, g.reshape(1, d))


def _top16(x):
    r = float(x.shape[0])
    iota = lax.broadcasted_iota(I32, x.shape, 0).astype(F32)
    vals, idxs = [], []
    for _ in range(PEER_TOPK):
        m = jnp.max(x, axis=0, keepdims=True)
        idx = jnp.min(jnp.where(x == m, iota, r), axis=0, keepdims=True)
        vals.append(m)
        idxs.append(idx)
        x = jnp.where(iota == idx, -jnp.inf, x)
    return vals, idxs


def _peer_candidates(s1, i1, s2, i2):
    sub = 8
    s1a, i1a = jnp.concatenate(s1, axis=0), jnp.concatenate(i1, axis=0)
    s2a, i2a = jnp.concatenate(s2, axis=0), jnp.concatenate(i2, axis=0)
    row = lax.broadcasted_iota(I32, s2a[:sub].shape, 0)
    cand, cidx = [], []
    first_single = PEER_TOPK // 2
    for a in range(first_single):
        nb = PEER_TOPK // (a + 1)
        if nb > sub:
            cand.append(s1[a] + s2a)
            cidx.append(i1[a] * PEER_KEYS + i2a)
        else:
            cand.append(jnp.where(row < nb, s1[a] + s2a[:sub], -jnp.inf))
            cidx.append(i1[a] * PEER_KEYS + i2a[:sub])
    cand.append(s1a[first_single:] + s2[0])
    cidx.append(i1a[first_single:] * PEER_KEYS + i2[0])
    return jnp.concatenate(cand, axis=0), jnp.concatenate(cidx, axis=0)


def _gelu_exact(x):
    return 0.5 * x * (1.0 + lax.erf(x * (2.0 ** -0.5)))


_PEER_T = 8
_DMA_QUEUES = 2


def _pack_expert_tables(u, v):
    half = u.shape[1] // 2
    bits = lambda a: lax.bitcast_convert_type(a.astype(BF16), jnp.uint16).astype(jnp.uint32)
    lo = jnp.concatenate([bits(u[:, :half]), bits(v[:, :half])], axis=1)
    hi = jnp.concatenate([bits(u[:, half:]), bits(v[:, half:])], axis=1)
    return lax.bitcast_convert_type(lo | (hi << 16), I32)


_MACRO = 256
_MACRO_STEPS = _MACRO // _PEER_T
_ROUTE_PHASES = 3 * PEER_HEADS + 2


def _route_experts_kernel(h_ref, x_ref, gf_ref, hall_hbm, wq_hbm, keys_hbm, uv_hbm, o_ref,
                          wq_ref, keys_ref, hbuf, q_ref, sub_s, sub_i, idx_pk, gate_pk, g2_ref, idx_tok, gate_tok,
                          idx_s0, idx_s1, buf0, buf1, a_ref, w_sem, idx_sem, row_sem, *, final_norm, n_tiles):
    i = pl.program_id(0)
    n = pl.num_programs(0)
    bufs = (buf0, buf1)
    idx_smem = (idx_s0, idx_s1)
    half = h_ref.shape[1] // 2

    def h_copy(mt):
        return pltpu.make_async_copy(hall_hbm.at[pl.ds(pl.multiple_of(mt * _MACRO, _MACRO), _MACRO)], hbuf, w_sem)

    def phase_project(mt):
        h_copy(mt).wait()
        q_ref[...] = jnp.dot(hbuf[...].astype(BF16), wq_ref[...], preferred_element_type=F32).astype(BF16)

    def phase_half(hc, mt):
        del mt
        qc = q_ref[:, hc * PEER_HALF:(hc + 1) * PEER_HALF]
        sc = lax.dot_general(keys_ref[hc], qc, (((1,), (1,)), ((), ())), preferred_element_type=F32)
        vals, idxs = _top16(sc)
        sub_s[hc % 2] = jnp.concatenate(vals, axis=0)
        sub_i[hc % 2] = jnp.concatenate(idxs, axis=0)

    def phase_head(hp, mt):
        del mt
        rows16 = lambda ref, c: [ref[c, j:j + 1, :] for j in range(PEER_TOPK)]
        s1, i1, s2, i2 = rows16(sub_s, 0), rows16(sub_i, 0), rows16(sub_s, 1), rows16(sub_i, 1)
        cand, cidx = _peer_candidates(s1, i1, s2, i2)
        top_s, pos = _top16(cand)
        iota = lax.broadcasted_iota(I32, cand.shape, 0).astype(F32)
        eidx = [jnp.sum(jnp.where(iota == p, cidx, 0.0), axis=0, keepdims=True) for p in pos]
        ts = jnp.concatenate(top_s, axis=0)
        e = jnp.exp(ts - top_s[0])
        rows = slice(hp * PEER_TOPK, (hp + 1) * PEER_TOPK)
        gate_pk[rows, :] = e / jnp.sum(e, axis=0, keepdims=True)
        idx_pk[rows, :] = jnp.concatenate(eidx, axis=0)

    def phase_finalize(mt):
        slot = mt % 2
        lanes = 128
        for hf in range(_MACRO // lanes):
            toks = slice(hf * lanes, (hf + 1) * lanes)
            g = gate_pk[:, toks]
            g2_ref[hf, pl.ds(0, PEER_PICKS, stride=2), :] = g
            g2_ref[hf, pl.ds(1, PEER_PICKS, stride=2), :] = jnp.zeros_like(g)
            gate_tok[slot, toks, :] = g2_ref[hf].T
            idx_tok[slot, toks, :] = (idx_pk[:, toks].astype(I32) * D_MODEL).T

        @pl.when(mt + 1 < n_tiles)
        def _():
            h_copy(mt + 1).start()

    phases = [phase_project]
    for hp in range(PEER_HEADS):
        phases += [functools.partial(phase_half, 2 * hp), functools.partial(phase_half, 2 * hp + 1),
                   functools.partial(phase_head, hp)]
    phases.append(phase_finalize)
    assert len(phases) == _ROUTE_PHASES <= _MACRO_STEPS

    def run_phase(ph, mt):
        lax.switch(ph, [lambda mt, f=f: f(mt) for f in phases], mt)

    def idx_copy(step, slot):
        src = idx_tok.at[(step // _MACRO_STEPS) % 2, pl.ds(pl.multiple_of((step % _MACRO_STEPS) * _PEER_T, _PEER_T), _PEER_T)]
        return pltpu.make_async_copy(src, idx_smem[slot], idx_sem.at[slot])

    def issue_token(slot, t):
        for k in range(PEER_PICKS):
            off = pl.multiple_of(idx_smem[slot][t, k], D_MODEL)
            copy = pltpu.make_async_copy(uv_hbm.at[pl.ds(off, D_MODEL)], bufs[slot].at[t, k], row_sem.at[slot])
            copy.start(priority=k % _DMA_QUEUES)

    def wait_rows(slot):
        pltpu.make_async_copy(bufs[slot], bufs[slot], row_sem.at[slot]).wait()

    @pl.when(i == 0)
    def _():
        cw = pltpu.make_async_copy(wq_hbm, wq_ref, w_sem)
        cw.start()
        cw.wait()
        ck = pltpu.make_async_copy(keys_hbm, keys_ref, w_sem)
        ck.start()
        ck.wait()
        h_copy(0).start()

    def routed(r, carry):
        first = (i == 0) & (r < _ROUTE_PHASES)
        ph = jnp.where(first, r, i % _MACRO_STEPS)
        mt = jnp.where(first, 0, i // _MACRO_STEPS + 1)

        @pl.when(first | ((ph < _ROUTE_PHASES) & (mt < n_tiles)))
        def _():
            run_phase(ph, mt)
        return carry
    lax.fori_loop(0, jnp.where(i == 0, _ROUTE_PHASES + 1, 1), routed, 0)

    @pl.when(i == 0)
    def _():
        idx_copy(0, 0).start()
        idx_copy(0, 0).wait()
        lax.fori_loop(0, _PEER_T, lambda t, c: (issue_token(0, t), c)[1], 0)
        idx_copy(1, 1).start()

    def step(cur, nxt):
        idx_copy(i + 1, nxt).wait()
        idx_copy(i + 2, cur).start()
        wait_rows(cur)
        buf = bufs[cur]
        tile_slot = (i // _MACRO_STEPS) % 2
        tile_row = pl.multiple_of((i % _MACRO_STEPS) * _PEER_T, _PEER_T)

        h = h_ref[...]
        hh = jnp.concatenate([h[:, :half], h[:, half:]], axis=0).astype(BF16)

        @pl.when(i >= 0)
        def _():
            for t in range(_PEER_T // 2):
                issue_token(nxt, t)
            tok = lax.broadcasted_iota(I32, (_PEER_T, 2 * PEER_PICKS), 0)
            a = jnp.zeros((_PEER_T, 2 * PEER_PICKS), F32)
            for t in range(_PEER_T):
                urows = pltpu.bitcast(buf[t, :, :half], BF16)
                r = lax.dot_general(hh, urows, (((1,), (1,)), ((), ())), preferred_element_type=F32)
                r = r[:_PEER_T] + pltpu.roll(r[_PEER_T:], 2 * PEER_PICKS - 1, axis=1)
                a = jnp.where(tok == t, r, a)
            a_ref[...] = a

        for t in range(_PEER_T // 2, _PEER_T):
            issue_token(nxt, t)
        w_even = gate_tok[tile_slot, pl.ds(tile_row, _PEER_T), :] * _gelu_exact(a_ref[...])
        w = jnp.concatenate([w_even, pltpu.roll(w_even, 1, axis=1)], axis=0).astype(BF16)
        tok = lax.broadcasted_iota(I32, (_PEER_T, half), 0)
        lo = jnp.zeros((_PEER_T, half), F32)
        hi = jnp.zeros((_PEER_T, half), F32)
        for t in range(_PEER_T):
            vrows = pltpu.bitcast(buf[t, :, half:], BF16)
            o = jnp.dot(w, vrows, preferred_element_type=F32)
            lo = jnp.where(tok == t, o[:_PEER_T], lo)
            hi = jnp.where(tok == t, o[_PEER_T:], hi)
        y = x_ref[...] + jnp.concatenate([lo, hi], axis=1)
        if final_norm:
            ms = jnp.mean(y * y, axis=-1, keepdims=True)
            y = y * lax.rsqrt(ms + EPS) * gf_ref[...]
        o_ref[...] = y

        @pl.when(i == n - 1)
        def _():
            wait_rows(nxt)
            idx_copy(i + 2, cur).wait()

    for parity in range(2):
        pl.when(i % 2 == parity)(functools.partial(step, parity, 1 - parity))


def _peer_route_experts(h, x, wq, keys, uv, g_final, *, final_norm):
    n, d = x.shape
    assert n % _MACRO == 0
    steps = n // _PEER_T
    row = lambda width: pl.BlockSpec((_PEER_T, width), lambda i: (i, 0))
    anyspec = pl.BlockSpec(memory_space=pl.ANY)
    return pl.pallas_call(
        functools.partial(_route_experts_kernel, final_norm=final_norm, n_tiles=n // _MACRO),
        grid=(steps,),
        in_specs=[row(d), row(d), pl.BlockSpec((1, d), lambda i: (0, 0)), anyspec, anyspec, anyspec, anyspec],
        out_specs=row(d),
        out_shape=jax.ShapeDtypeStruct((n, d), F32),
        scratch_shapes=[pltpu.VMEM(wq.shape, BF16),
                        pltpu.VMEM(keys.shape, BF16),
                        pltpu.VMEM((_MACRO, d), F32),
                        pltpu.VMEM((_MACRO, wq.shape[1]), BF16),
                        pltpu.VMEM((2, PEER_TOPK, _MACRO), F32),
                        pltpu.VMEM((2, PEER_TOPK, _MACRO), F32),
                        pltpu.VMEM((PEER_PICKS, _MACRO), F32),
                        pltpu.VMEM((PEER_PICKS, _MACRO), F32),
                        pltpu.VMEM((_MACRO // 128, 2 * PEER_PICKS, 128), F32),
                        pltpu.VMEM((2, _MACRO, PEER_PICKS), I32),
                        pltpu.VMEM((2, _MACRO, 2 * PEER_PICKS), F32),
                        pltpu.SMEM((_PEER_T, PEER_PICKS), I32),
                        pltpu.SMEM((_PEER_T, PEER_PICKS), I32),
                        pltpu.VMEM((_PEER_T, PEER_PICKS, d), I32),
                        pltpu.VMEM((_PEER_T, PEER_PICKS, d), I32),
                        pltpu.VMEM((_PEER_T, 2 * PEER_PICKS), F32),
                        pltpu.SemaphoreType.DMA,
                        pltpu.SemaphoreType.DMA((2,)),
                        pltpu.SemaphoreType.DMA((2,))],
        compiler_params=_cparams(("arbitrary",)),
        name="peer_route_experts",
    )(h, x, g_final.reshape(1, d), h, wq, keys, uv.reshape(-1))


def _encoder(xs, rel_bias, g_attn, w_in, a_sink, g_grp_a, nat_rpb, g_grp_b, lam_q1, lam_k1, lam_q2, lam_k2,
             g_subln, w_out, g_ffn, peer_wq, peer_keys, peer_u, peer_v, g_final):
    depth = w_in.shape[0]
    d = xs[0].shape[-1]
    splits = [int(c) for c in np.cumsum((0,) + IN_SIZES)]
    bias_a = _bias_a_table(rel_bias[:, :A_Q_HEADS])
    tiles_c, far_c = _bias_c_tables(rel_bias[:, A_Q_HEADS:])
    shapes = [x.shape[:2] for x in xs]
    xs = [x.reshape(-1, d) for x in xs]
    for l in range(depth):
        lam_init = 0.8 - 0.6 * math.exp(-0.3 * l)
        w_l = jnp.concatenate([w_in[l][:, splits[s]:splits[s + 1]] for s in _PROJ_ORDER], axis=1).astype(BF16)
        w_o = w_out[l].astype(BF16)
        w_q = peer_wq[l].astype(BF16)
        bias_b = _bias_b_table(nat_rpb[l])
        keys = peer_keys[l].reshape(2 * PEER_HEADS, PEER_KEYS, PEER_HALF).astype(BF16)
        uv = _pack_expert_tables(peer_u[l], peer_v[l])
        for b, (n_seq, seq) in enumerate(shapes):
            x = xs[b]
            n = x.shape[0]
            kw = dict(row0=0, n_seq=n_seq, seq=seq)
            proj = _norm_proj(x, g_attn[l], w_l)
            ya = _mixer_a(proj, bias_a, a_sink[l], g_grp_a[l], **kw)
            yb = _mixer_b(proj, bias_b, g_grp_b[l], **kw)
            yc = _mixer_c(proj, tiles_c, far_c, lam_q1[l], lam_k1[l], lam_q2[l], lam_k2[l], g_subln[l],
                          lam_init=lam_init, **kw)
            x1, h2 = _out_proj(x, ya, yb, yc, w_o, g_ffn[l])
            xs[b] = _peer_route_experts(h2, x1, w_q, keys, uv, g_final, final_norm=(l == depth - 1))
    return tuple(x.reshape(n_seq, seq, d) for x, (n_seq, seq) in zip(xs, shapes))


def kernel(x_prompt, x_sample, rel_bias, g_attn, w_in, a_sink, g_grp_a, nat_rpb, g_grp_b, lam_q1, lam_k1, lam_q2, lam_k2, g_subln, w_out, g_ffn, peer_wq, peer_keys, peer_u, peer_v, g_final):
    return _encoder([x_prompt, x_sample], rel_bias, g_attn, w_in, a_sink, g_grp_a, nat_rpb, g_grp_b, lam_q1, lam_k1,
                    lam_q2, lam_k2, g_subln, w_out, g_ffn, peer_wq, peer_keys, peer_u, peer_v, g_final)
```

```python
import functools
import math

import numpy as np
import jax
import jax.numpy as jnp
from jax import lax
from jax.experimental import pallas as pl
from jax.experimental.pallas import tpu as pltpu

F32 = jnp.float32
BF16 = jnp.bfloat16
I32 = jnp.int32
EPS = 1e-6
NEG = -1e30

D_MODEL = 2048
DEPTH = 2
HEAD_DIM = 128
A_Q_HEADS = 6
A_KV_HEADS = 2
A_GROUP = A_Q_HEADS // A_KV_HEADS
WINDOW = 128
BLOCK = 128
B_HEADS = 6
GRID_W = 64
NB_KH = 8
NB_KW = 16
C_HEADS = 4
C_QK_DIM = 64
C_V_DIM = 128
A_WIDTH = A_Q_HEADS * HEAD_DIM
B_WIDTH = B_HEADS * HEAD_DIM
C_WIDTH = C_HEADS * C_V_DIM
A_KV = A_KV_HEADS * HEAD_DIM
C_QK = C_HEADS * 2 * C_QK_DIM
IN_SIZES = (A_WIDTH, A_KV, A_KV, B_WIDTH, B_WIDTH, B_WIDTH, C_QK, C_QK, C_WIDTH)
IN_WIDTH = sum(IN_SIZES)
REL_BUCKETS = 32
REL_MAX_DIST = 128
PEER_HEADS = 8
PEER_KEYS = 128
PEER_TOPK = 16
PEER_QDIM = 256
PEER_HALF = PEER_QDIM // 2
PEER_PICKS = PEER_HEADS * PEER_TOPK

_PROJ_ORDER = (0, 3, 4, 5, 1, 2, 6, 7, 8)
_COL_AQ, _COL_BQ, _COL_BK, _COL_BV = 0, 1, 2, 3
_COL_AK, _COL_AV = 12, 13
_COL_CQ, _COL_CK, _COL_CV = 7, 8, 9

_VMEM_LIMIT = 56 * 1024 * 1024


def _cparams(sem):
    return pltpu.CompilerParams(dimension_semantics=sem, vmem_limit_bytes=_VMEM_LIMIT)


def _t5_bucket(rel):
    nb = REL_BUCKETS // 2
    max_exact = nb // 2
    ret = jnp.where(rel > 0, nb, 0)
    n = jnp.abs(rel)
    nf = jnp.maximum(n, 1).astype(F32)
    large = max_exact + (jnp.log(nf / max_exact) / math.log(REL_MAX_DIST / max_exact) * (nb - max_exact)).astype(I32)
    large = jnp.minimum(large, nb - 1)
    return ret + jnp.where(n < max_exact, n, large)


def _lookup(table, idx):
    onehot = jax.nn.one_hot(idx, table.shape[0], dtype=F32)
    return jnp.einsum('...b,bc->...c', onehot, table.astype(F32), precision=lax.Precision.HIGHEST)


def _bias_a_table(rel_bias_a):
    rel = np.arange(3 * BLOCK)[None, :] - BLOCK - np.arange(BLOCK)[:, None]
    bias = _lookup(rel_bias_a, _t5_bucket(jnp.asarray(rel)))
    bias = jnp.where(jnp.asarray(np.abs(rel) <= WINDOW)[:, :, None], bias, NEG)
    return bias.transpose(2, 0, 1)


def _bias_b_table(rpb):
    classes = (((0, 1), (0, 0)), ((2, 3), (0, 0)), ((4, 5), (0, 1)), ((5, 6), (1, 1)), ((7, 8), (1, 1)))
    qc = np.arange(GRID_W)[:, None]
    kc = np.arange(GRID_W)[None, :]
    c0 = np.clip(qc - NB_KW // 2, 0, GRID_W - NB_KW)
    in_cols = (kc >= c0) & (kc < c0 + NB_KW)
    n_dc = 2 * NB_KW - 1
    dc_onehot = ((kc - qc + NB_KW - 1)[:, :, None] == np.arange(n_dc)).astype(np.float32)
    blocks = jnp.einsum('qkc,hdc->hdqk', jnp.asarray(dc_onehot), rpb.astype(F32), precision=lax.Precision.HIGHEST)
    blocks = jnp.where(jnp.asarray(in_cols), blocks, NEG)
    neg = jnp.full((rpb.shape[0], GRID_W, GRID_W), NEG, F32)
    tabs = []
    for (rq, r0) in classes:
        rows = []
        for a in range(_B_QROWS):
            parts = [blocks[:, kr - rq[a] + NB_KH - 1] if r0[a] <= kr < r0[a] + NB_KH else neg
                     for kr in range(_B_KROWS)]
            rows.append(jnp.concatenate(parts, axis=-1))
        tabs.append(jnp.concatenate(rows, axis=1))
    return jnp.stack(tabs)


def _bias_c_tables(rel_bias_c):
    q = np.arange(BLOCK)[:, None]
    k = np.arange(BLOCK)[None, :]
    far = _lookup(rel_bias_c, _t5_bucket(jnp.asarray([-2 * BLOCK, 2 * BLOCK])))
    tiles = []
    for d in (-1, 0, 1):
        rel = k + d * BLOCK - q
        t = _lookup(rel_bias_c, _t5_bucket(jnp.asarray(rel))).transpose(2, 0, 1)
        base = far[0] if d < 0 else far[1]
        tiles.append(t - base[:, None, None])
    return jnp.stack(tiles, axis=1), far.T


def _norm_proj_kernel(x_ref, g_ref, w_ref, o_ref, h_ref):
    @pl.when(pl.program_id(1) == 0)
    def _():
        x = x_ref[...]
        ms = jnp.mean(x * x, axis=-1, keepdims=True)
        h_ref[...] = (x * lax.rsqrt(ms + EPS) * g_ref[...]).astype(BF16)

    o_ref[...] = jnp.dot(h_ref[...], w_ref[...], preferred_element_type=F32).astype(o_ref.dtype)


def _norm_proj(x, g, w, *, tm=1024, tn=1280):
    n, d = x.shape
    width = w.shape[1]
    return pl.pallas_call(
        _norm_proj_kernel,
        grid=(n // tm, width // tn),
        in_specs=[
            pl.BlockSpec((tm, d), lambda i, j: (i, 0)),
            pl.BlockSpec((1, d), lambda i, j: (0, 0)),
            pl.BlockSpec((d, tn), lambda i, j: (0, j)),
        ],
        out_specs=pl.BlockSpec((tm, tn), lambda i, j: (i, j)),
        out_shape=jax.ShapeDtypeStruct((n, width), BF16),
        scratch_shapes=[pltpu.VMEM((tm, d), BF16)],
        compiler_params=_cparams(("parallel", "arbitrary")),
        name="norm_proj",
    )(x, g.reshape(1, d), w)


def _group_norm_store(o_ref, outs, g_ref):
    o = jnp.concatenate(outs, axis=-1)
    ms = jnp.mean(o * o, axis=-1, keepdims=True)
    o_ref[...] = (o * lax.rsqrt(ms + EPS) * g_ref[...]).astype(o_ref.dtype)


def _mixer_a_kernel(q_ref, kp_ref, kc_ref, kn_ref, vp_ref, vc_ref, vn_ref, bias_ref, sink_ref, g_ref, o_ref):
    i = pl.program_id(1)
    nb = pl.num_programs(1)
    scale = HEAD_DIM ** -0.5
    col = lax.broadcasted_iota(I32, (BLOCK, 3 * BLOCK), 1)
    lo = jnp.where(i == 0, BLOCK, 0)
    hi = jnp.where(i == nb - 1, 2 * BLOCK, 3 * BLOCK)
    in_seq = (col >= lo) & (col < hi)
    ks, vs, scores = [], [], []
    for kv in range(A_KV_HEADS):
        sl = slice(kv * HEAD_DIM, (kv + 1) * HEAD_DIM)
        ks.append(jnp.concatenate([kp_ref[:, sl], kc_ref[:, sl], kn_ref[:, sl]], axis=0))
        vs.append(jnp.concatenate([vp_ref[:, sl], vc_ref[:, sl], vn_ref[:, sl]], axis=0))
    for h in range(A_Q_HEADS):
        q = q_ref[:, h * HEAD_DIM:(h + 1) * HEAD_DIM]
        scores.append(lax.dot_general(q, ks[h // A_GROUP], (((1,), (1,)), ((), ())), preferred_element_type=F32))
    probs, denoms = [], []
    for h in range(A_Q_HEADS):
        s = jnp.where(in_seq, scores[h] * scale + bias_ref[h], NEG)
        sk = sink_ref[h]
        m = jnp.maximum(jnp.max(s, axis=-1, keepdims=True), sk)
        p = jnp.exp(s - m)
        denoms.append(jnp.sum(p, axis=-1, keepdims=True) + jnp.exp(sk - m))
        probs.append(p.astype(BF16))
    outs = [jnp.dot(probs[h], vs[h // A_GROUP], preferred_element_type=F32) / denoms[h] for h in range(A_Q_HEADS)]
    _group_norm_store(o_ref, outs, g_ref)


def _mixer_a(proj, bias, sink, g, *, row0, n_seq, seq):
    nb = seq // BLOCK
    b0 = row0 // BLOCK

    def blk(shift, colblk):
        return lambda b, i: (b0 + b * nb + jnp.clip(i + shift, 0, nb - 1), colblk)

    kv_spec = lambda shift, colblk: pl.BlockSpec((BLOCK, A_KV), blk(shift, colblk))
    return pl.pallas_call(
        _mixer_a_kernel,
        grid=(n_seq, nb),
        in_specs=[
            pl.BlockSpec((BLOCK, A_WIDTH), blk(0, _COL_AQ)),
            kv_spec(-1, _COL_AK), kv_spec(0, _COL_AK), kv_spec(1, _COL_AK),
            kv_spec(-1, _COL_AV), kv_spec(0, _COL_AV), kv_spec(1, _COL_AV),
            pl.BlockSpec((A_Q_HEADS, BLOCK, 3 * BLOCK), lambda b, i: (0, 0, 0)),
            pl.BlockSpec(memory_space=pltpu.SMEM),
            pl.BlockSpec((1, A_WIDTH), lambda b, i: (0, 0)),
        ],
        out_specs=pl.BlockSpec((BLOCK, A_WIDTH), lambda b, i: (b * nb + i, 0)),
        out_shape=jax.ShapeDtypeStruct((n_seq * seq, A_WIDTH), BF16),
        compiler_params=_cparams(("parallel", "parallel")),
        name="mixer_a",
    )(proj, proj, proj, proj, proj, proj, proj, bias, sink, g.reshape(1, A_WIDTH))


_B_QROWS = 2
_B_KROWS = NB_KH + _B_QROWS - 1


def _mixer_b_kernel(q_ref, k_ref, v_ref, bias_ref, g_ref, o_ref, *, rows):
    i = pl.program_id(1)
    scale = HEAD_DIM ** -0.5
    kstart = jnp.clip(_B_QROWS * i - NB_KH // 2, 0, rows - _B_KROWS)
    ks = pl.multiple_of(kstart * GRID_W, GRID_W)
    nk = _B_KROWS * GRID_W
    heads = [slice(h * HEAD_DIM, (h + 1) * HEAD_DIM) for h in range(B_HEADS)]
    scores = [lax.dot_general(q_ref[:, sl], k_ref[pl.ds(ks, nk), sl], (((1,), (1,)), ((), ())),
                              preferred_element_type=F32) for sl in heads]
    probs, denoms = [], []
    for h in range(B_HEADS):
        s = scores[h] * scale + bias_ref[0, h]
        p = jnp.exp(s - jnp.max(s, axis=-1, keepdims=True))
        denoms.append(jnp.sum(p, axis=-1, keepdims=True))
        probs.append(p.astype(BF16))
    outs = [jnp.dot(probs[h], v_ref[pl.ds(ks, nk), sl], preferred_element_type=F32) / denoms[h]
            for h, sl in enumerate(heads)]
    _group_norm_store(o_ref, outs, g_ref)


def _mixer_b(proj, bias, g, *, row0, n_seq, seq):
    rows = seq // GRID_W
    nrp = rows // _B_QROWS
    tq = _B_QROWS * GRID_W
    b0 = row0 // tq
    s0 = row0 // seq

    def cls(i):
        return jnp.where(i == 0, 0, jnp.where(i == 1, 1, jnp.where(i == nrp - 2, 3, jnp.where(i == nrp - 1, 4, 2))))

    return pl.pallas_call(
        functools.partial(_mixer_b_kernel, rows=rows),
        grid=(n_seq, nrp),
        in_specs=[
            pl.BlockSpec((tq, B_WIDTH), lambda b, i: (b0 + b * nrp + i, _COL_BQ)),
            pl.BlockSpec((seq, B_WIDTH), lambda b, i: (s0 + b, _COL_BK)),
            pl.BlockSpec((seq, B_WIDTH), lambda b, i: (s0 + b, _COL_BV)),
            pl.BlockSpec((1, B_HEADS, tq, _B_KROWS * GRID_W), lambda b, i: (cls(i), 0, 0, 0)),
            pl.BlockSpec((1, B_WIDTH), lambda b, i: (0, 0)),
        ],
        out_specs=pl.BlockSpec((tq, B_WIDTH), lambda b, i: (b * nrp + i, 0)),
        out_shape=jax.ShapeDtypeStruct((n_seq * seq, B_WIDTH), BF16),
        compiler_params=_cparams(("parallel", "arbitrary")),
        name="mixer_b",
    )(proj, proj, proj, bias, g.reshape(1, B_WIDTH))


def _mixer_c_kernel(q_ref, k_ref, v_ref, tiles_ref, far_ref, lq1_ref, lk1_ref, lq2_ref, lk2_ref, g_ref, o_ref,
                    *s_refs, seq, lam_init):
    i = pl.program_id(1)
    nb = seq // BLOCK
    scale = C_QK_DIM ** -0.5
    lam = (jnp.exp(jnp.sum(lq1_ref[...] * lk1_ref[...], axis=-1, keepdims=True))
           - jnp.exp(jnp.sum(lq2_ref[...] * lk2_ref[...], axis=-1, keepdims=True)) + lam_init)
    before = lax.broadcasted_iota(I32, (1, seq), 1) < i * BLOCK
    lane = lax.broadcasted_iota(I32, (BLOCK, HEAD_DIM), 1)
    for h in range(C_HEADS):
        sl = slice(h * HEAD_DIM, (h + 1) * HEAD_DIM)
        qh = q_ref[:, sl] * scale
        kh = k_ref[:, sl]
        qs = jnp.concatenate(
            [jnp.where((lane >= c * C_QK_DIM) & (lane < (c + 1) * C_QK_DIM), qh, jnp.zeros_like(qh)) for c in range(2)],
            axis=0)
        s = lax.dot_general(qs, kh, (((1,), (1,)), ((), ())), preferred_element_type=F32)
        for c in range(2):
            hc = 2 * h + c
            s_refs[hc][...] = s[c * BLOCK:(c + 1) * BLOCK] + jnp.where(before, far_ref[hc, 0], far_ref[hc, 1])
    outs = []
    for h in range(C_HEADS):
        sl = slice(h * HEAD_DIM, (h + 1) * HEAD_DIM)
        exps, sums = [], []
        for c in range(2):
            hc = 2 * h + c
            for d in range(3):
                j = i + (d - 1)
                inside = jnp.where((j >= 0) & (j < nb), 1.0, 0.0)
                cols = pl.ds(pl.multiple_of(jnp.clip(j, 0, nb - 1) * BLOCK, BLOCK), BLOCK)
                s_refs[hc][:, cols] += tiles_ref[hc, d] * inside
            s = s_refs[hc][...]
            e = jnp.exp(s - jnp.max(s, axis=-1, keepdims=True))
            exps.append(e)
            sums.append(jnp.sum(e, axis=-1, keepdims=True))
        a = (exps[0] - (lam * sums[0] / sums[1]) * exps[1]).astype(BF16)
        o = jnp.dot(a, v_ref[:, sl], preferred_element_type=F32) / sums[0]
        ms = jnp.mean(o * o, axis=-1, keepdims=True)
        outs.append(o * lax.rsqrt(ms + EPS) * g_ref[...] * (1.0 - lam_init))
    o_ref[...] = jnp.concatenate(outs, axis=-1).astype(o_ref.dtype)


def _mixer_c(proj, tiles, far, lq1, lk1, lq2, lk2, g, *, row0, n_seq, seq, lam_init):
    nb = seq // BLOCK
    b0 = row0 // BLOCK
    s0 = row0 // seq
    vec = lambda a: a.reshape(1, C_QK_DIM)
    vspec = pl.BlockSpec((1, C_QK_DIM), lambda b, i: (0, 0))
    return pl.pallas_call(
        functools.partial(_mixer_c_kernel, seq=seq, lam_init=lam_init),
        grid=(n_seq, nb),
        in_specs=[
            pl.BlockSpec((BLOCK, C_QK), lambda b, i: (b0 + b * nb + i, _COL_CQ)),
            pl.BlockSpec((seq, C_QK), lambda b, i: (s0 + b, _COL_CK)),
            pl.BlockSpec((seq, C_WIDTH), lambda b, i: (s0 + b, _COL_CV)),
            pl.BlockSpec((2 * C_HEADS, 3, BLOCK, BLOCK), lambda b, i: (0, 0, 0, 0)),
            pl.BlockSpec(memory_space=pltpu.SMEM),
            vspec, vspec, vspec, vspec,
            pl.BlockSpec((1, C_V_DIM), lambda b, i: (0, 0)),
        ],
        out_specs=pl.BlockSpec((BLOCK, C_WIDTH), lambda b, i: (b * nb + i, 0)),
        out_shape=jax.ShapeDtypeStruct((n_seq * seq, C_WIDTH), BF16),
        scratch_shapes=[pltpu.VMEM((BLOCK, seq), F32)] * (2 * C_HEADS),
        compiler_params=_cparams(("parallel", "arbitrary")),
        name="mixer_c",
    )(proj, proj, proj, tiles, far, vec(lq1), vec(lk1), vec(lq2), vec(lk2), g.reshape(1, C_V_DIM))


def _out_proj_kernel(x_ref, ya_ref, yb_ref, yc_ref, w_ref, g_ref, x1_ref, h_ref):
    acc = jnp.dot(ya_ref[...], w_ref[0:A_WIDTH, :], preferred_element_type=F32)
    acc += jnp.dot(yb_ref[...], w_ref[A_WIDTH:A_WIDTH + B_WIDTH, :], preferred_element_type=F32)
    acc += jnp.dot(yc_ref[...], w_ref[A_WIDTH + B_WIDTH:, :], preferred_element_type=F32)
    x1 = x_ref[...] + acc
    x1_ref[...] = x1
    ms = jnp.mean(x1 * x1, axis=-1, keepdims=True)
    h_ref[...] = x1 * lax.rsqrt(ms + EPS) * g_ref[...]


def _out_proj(x, ya, yb, yc, w, g, *, tm=512):
    n, d = x.shape
    row = lambda width: pl.BlockSpec((tm, width), lambda i: (i, 0))
    return pl.pallas_call(
        _out_proj_kernel,
        grid=(n // tm,),
        in_specs=[row(d), row(A_WIDTH), row(B_WIDTH), row(C_WIDTH),
                  pl.BlockSpec((d, d), lambda i: (0, 0)),
                  pl.BlockSpec((1, d), lambda i: (0, 0))],
        out_specs=[row(d), row(d)],
        out_shape=[jax.ShapeDtypeStruct((n, d), F32), jax.ShapeDtypeStruct((n, d), F32)],
        compiler_params=_cparams(("parallel",)),
        name="out_proj",
    )(x, ya, yb, yc, w, g.reshape(1, d))


def _top16(x):
    r = float(x.shape[0])
    iota = lax.broadcasted_iota(I32, x.shape, 0).astype(F32)
    vals, idxs = [], []
    for _ in range(PEER_TOPK):
        m = jnp.max(x, axis=0, keepdims=True)
        idx = jnp.min(jnp.where(x == m, iota, r), axis=0, keepdims=True)
        vals.append(m)
        idxs.append(idx)
        x = jnp.where(iota == idx, -jnp.inf, x)
    return vals, idxs


def _peer_candidates(s1, i1, s2, i2):
    sub = 8
    s1a, i1a = jnp.concatenate(s1, axis=0), jnp.concatenate(i1, axis=0)
    s2a, i2a = jnp.concatenate(s2, axis=0), jnp.concatenate(i2, axis=0)
    row = lax.broadcasted_iota(I32, s2a[:sub].shape, 0)
    cand, cidx = [], []
    first_single = PEER_TOPK // 2
    for a in range(first_single):
        nb = PEER_TOPK // (a + 1)
        if nb > sub:
            cand.append(s1[a] + s2a)
            cidx.append(i1[a] * PEER_KEYS + i2a)
        else:
            cand.append(jnp.where(row < nb, s1[a] + s2a[:sub], -jnp.inf))
            cidx.append(i1[a] * PEER_KEYS + i2a[:sub])
    cand.append(s1a[first_single:] + s2[0])
    cidx.append(i1a[first_single:] * PEER_KEYS + i2[0])
    return jnp.concatenate(cand, axis=0), jnp.concatenate(cidx, axis=0)


def _gelu_exact(x):
    return 0.5 * x * (1.0 + lax.erf(x * (2.0 ** -0.5)))


_PEER_T = 8
_DMA_QUEUES = 2


def _pack_expert_tables(u, v):
    half = u.shape[1] // 2
    bits = lambda a: lax.bitcast_convert_type(a.astype(BF16), jnp.uint16).astype(jnp.uint32)
    lo = jnp.concatenate([bits(u[:, :half]), bits(v[:, :half])], axis=1)
    hi = jnp.concatenate([bits(u[:, half:]), bits(v[:, half:])], axis=1)
    return lax.bitcast_convert_type(lo | (hi << 16), I32)


_MACRO = 256
_MACRO_STEPS = _MACRO // _PEER_T
_ROUTE_PHASES = 3 * PEER_HEADS + 2


def _route_experts_kernel(h_ref, x_ref, gf_ref, hall_hbm, wq_hbm, keys_hbm, uv_hbm, o_ref,
                          wq_ref, keys_ref, hbuf, q_ref, sub_s, sub_i, idx_pk, gate_pk, g2_ref, idx_tok, gate_tok,
                          idx_s0, idx_s1, buf0, buf1, a_ref, w_sem, idx_sem, row_sem, *, final_norm, n_tiles):
    i = pl.program_id(0)
    n = pl.num_programs(0)
    bufs = (buf0, buf1)
    idx_smem = (idx_s0, idx_s1)
    half = h_ref.shape[1] // 2

    def h_copy(mt):
        return pltpu.make_async_copy(hall_hbm.at[pl.ds(pl.multiple_of(mt * _MACRO, _MACRO), _MACRO)], hbuf, w_sem)

    def phase_project(mt):
        h_copy(mt).wait()
        q_ref[...] = jnp.dot(hbuf[...].astype(BF16), wq_ref[...], preferred_element_type=F32).astype(BF16)

    def phase_half(hc, mt):
        del mt
        qc = q_ref[:, hc * PEER_HALF:(hc + 1) * PEER_HALF]
        sc = lax.dot_general(keys_ref[hc], qc, (((1,), (1,)), ((), ())), preferred_element_type=F32)
        vals, idxs = _top16(sc)
        sub_s[hc % 2] = jnp.concatenate(vals, axis=0)
        sub_i[hc % 2] = jnp.concatenate(idxs, axis=0)

    def phase_head(hp, mt):
        del mt
        rows16 = lambda ref, c: [ref[c, j:j + 1, :] for j in range(PEER_TOPK)]
        s1, i1, s2, i2 = rows16(sub_s, 0), rows16(sub_i, 0), rows16(sub_s, 1), rows16(sub_i, 1)
        cand, cidx = _peer_candidates(s1, i1, s2, i2)
        top_s, pos = _top16(cand)
        iota = lax.broadcasted_iota(I32, cand.shape, 0).astype(F32)
        eidx = [jnp.sum(jnp.where(iota == p, cidx, 0.0), axis=0, keepdims=True) for p in pos]
        ts = jnp.concatenate(top_s, axis=0)
        e = jnp.exp(ts - top_s[0])
        rows = slice(hp * PEER_TOPK, (hp + 1) * PEER_TOPK)
        gate_pk[rows, :] = e / jnp.sum(e, axis=0, keepdims=True)
        idx_pk[rows, :] = jnp.concatenate(eidx, axis=0)

    def phase_finalize(mt):
        slot = mt % 2
        lanes = 128
        for hf in range(_MACRO // lanes):
            toks = slice(hf * lanes, (hf + 1) * lanes)
            g = gate_pk[:, toks]
            g2_ref[hf, pl.ds(0, PEER_PICKS, stride=2), :] = g
            g2_ref[hf, pl.ds(1, PEER_PICKS, stride=2), :] = jnp.zeros_like(g)
            gate_tok[slot, toks, :] = g2_ref[hf].T
            idx_tok[slot, toks, :] = (idx_pk[:, toks].astype(I32) * D_MODEL).T

        @pl.when(mt + 1 < n_tiles)
        def _():
            h_copy(mt + 1).start()

    phases = [phase_project]
    for hp in range(PEER_HEADS):
        phases += [functools.partial(phase_half, 2 * hp), functools.partial(phase_half, 2 * hp + 1),
                   functools.partial(phase_head, hp)]
    phases.append(phase_finalize)
    assert len(phases) == _ROUTE_PHASES <= _MACRO_STEPS

    def run_phase(ph, mt):
        lax.switch(ph, [lambda mt, f=f: f(mt) for f in phases], mt)

    def idx_copy(step, slot):
        src = idx_tok.at[(step // _MACRO_STEPS) % 2, pl.ds(pl.multiple_of((step % _MACRO_STEPS) * _PEER_T, _PEER_T), _PEER_T)]
        return pltpu.make_async_copy(src, idx_smem[slot], idx_sem.at[slot])

    def issue_token(slot, t):
        for k in range(PEER_PICKS):
            off = pl.multiple_of(idx_smem[slot][t, k], D_MODEL)
            copy = pltpu.make_async_copy(uv_hbm.at[pl.ds(off, D_MODEL)], bufs[slot].at[t, k], row_sem.at[slot])
            copy.start(priority=k % _DMA_QUEUES)

    def wait_rows(slot):
        pltpu.make_async_copy(bufs[slot], bufs[slot], row_sem.at[slot]).wait()

    @pl.when(i == 0)
    def _():
        cw = pltpu.make_async_copy(wq_hbm, wq_ref, w_sem)
        cw.start()
        cw.wait()
        ck = pltpu.make_async_copy(keys_hbm, keys_ref, w_sem)
        ck.start()
        ck.wait()
        h_copy(0).start()

    def routed(r, carry):
        first = (i == 0) & (r < _ROUTE_PHASES)
        ph = jnp.where(first, r, i % _MACRO_STEPS)
        mt = jnp.where(first, 0, i // _MACRO_STEPS + 1)

        @pl.when(first | ((ph < _ROUTE_PHASES) & (mt < n_tiles)))
        def _():
            run_phase(ph, mt)
        return carry
    lax.fori_loop(0, jnp.where(i == 0, _ROUTE_PHASES + 1, 1), routed, 0)

    @pl.when(i == 0)
    def _():
        idx_copy(0, 0).start()
        idx_copy(0, 0).wait()
        lax.fori_loop(0, _PEER_T, lambda t, c: (issue_token(0, t), c)[1], 0)
        idx_copy(1, 1).start()

    def step(cur, nxt):
        idx_copy(i + 1, nxt).wait()
        idx_copy(i + 2, cur).start()
        wait_rows(cur)
        buf = bufs[cur]
        tile_slot = (i // _MACRO_STEPS) % 2
        tile_row = pl.multiple_of((i % _MACRO_STEPS) * _PEER_T, _PEER_T)

        h = h_ref[...]
        hh = jnp.concatenate([h[:, :half], h[:, half:]], axis=0).astype(BF16)

        @pl.when(i >= 0)
        def _():
            for t in range(_PEER_T // 2):
                issue_token(nxt, t)
            tok = lax.broadcasted_iota(I32, (_PEER_T, 2 * PEER_PICKS), 0)
            a = jnp.zeros((_PEER_T, 2 * PEER_PICKS), F32)
            for t in range(_PEER_T):
                urows = pltpu.bitcast(buf[t, :, :half], BF16)
                r = lax.dot_general(hh, urows, (((1,), (1,)), ((), ())), preferred_element_type=F32)
                r = r[:_PEER_T] + pltpu.roll(r[_PEER_T:], 2 * PEER_PICKS - 1, axis=1)
                a = jnp.where(tok == t, r, a)
            a_ref[...] = a

        for t in range(_PEER_T // 2, _PEER_T):
            issue_token(nxt, t)
        w_even = gate_tok[tile_slot, pl.ds(tile_row, _PEER_T), :] * _gelu_exact(a_ref[...])
        w = jnp.concatenate([w_even, pltpu.roll(w_even, 1, axis=1)], axis=0).astype(BF16)
        tok = lax.broadcasted_iota(I32, (_PEER_T, half), 0)
        lo = jnp.zeros((_PEER_T, half), F32)
        hi = jnp.zeros((_PEER_T, half), F32)
        for t in range(_PEER_T):
            vrows = pltpu.bitcast(buf[t, :, half:], BF16)
            o = jnp.dot(w, vrows, preferred_element_type=F32)
            lo = jnp.where(tok == t, o[:_PEER_T], lo)
            hi = jnp.where(tok == t, o[_PEER_T:], hi)
        y = x_ref[...] + jnp.concatenate([lo, hi], axis=1)
        if final_norm:
            ms = jnp.mean(y * y, axis=-1, keepdims=True)
            y = y * lax.rsqrt(ms + EPS) * gf_ref[...]
        o_ref[...] = y

        @pl.when(i == n - 1)
        def _():
            wait_rows(nxt)
            idx_copy(i + 2, cur).wait()

    for parity in range(2):
        pl.when(i % 2 == parity)(functools.partial(step, parity, 1 - parity))


def _peer_route_experts(h, x, wq, keys, uv, g_final, *, final_norm):
    n, d = x.shape
    assert n % _MACRO == 0
    steps = n // _PEER_T
    row = lambda width: pl.BlockSpec((_PEER_T, width), lambda i: (i, 0))
    anyspec = pl.BlockSpec(memory_space=pl.ANY)
    return pl.pallas_call(
        functools.partial(_route_experts_kernel, final_norm=final_norm, n_tiles=n // _MACRO),
        grid=(steps,),
        in_specs=[row(d), row(d), pl.BlockSpec((1, d), lambda i: (0, 0)), anyspec, anyspec, anyspec, anyspec],
        out_specs=row(d),
        out_shape=jax.ShapeDtypeStruct((n, d), F32),
        scratch_shapes=[pltpu.VMEM(wq.shape, BF16),
                        pltpu.VMEM(keys.shape, BF16),
                        pltpu.VMEM((_MACRO, d), F32),
                        pltpu.VMEM((_MACRO, wq.shape[1]), BF16),
                        pltpu.VMEM((2, PEER_TOPK, _MACRO), F32),
                        pltpu.VMEM((2, PEER_TOPK, _MACRO), F32),
                        pltpu.VMEM((PEER_PICKS, _MACRO), F32),
                        pltpu.VMEM((PEER_PICKS, _MACRO), F32),
                        pltpu.VMEM((_MACRO // 128, 2 * PEER_PICKS, 128), F32),
                        pltpu.VMEM((2, _MACRO, PEER_PICKS), I32),
                        pltpu.VMEM((2, _MACRO, 2 * PEER_PICKS), F32),
                        pltpu.SMEM((_PEER_T, PEER_PICKS), I32),
                        pltpu.SMEM((_PEER_T, PEER_PICKS), I32),
                        pltpu.VMEM((_PEER_T, PEER_PICKS, d), I32),
                        pltpu.VMEM((_PEER_T, PEER_PICKS, d), I32),
                        pltpu.VMEM((_PEER_T, 2 * PEER_PICKS), F32),
                        pltpu.SemaphoreType.DMA,
                        pltpu.SemaphoreType.DMA((2,)),
                        pltpu.SemaphoreType.DMA((2,))],
        compiler_params=_cparams(("arbitrary",)),
        name="peer_route_experts",
    )(h, x, g_final.reshape(1, d), h, wq, keys, uv.reshape(-1))


def _encoder(xs, rel_bias, g_attn, w_in, a_sink, g_grp_a, nat_rpb, g_grp_b, lam_q1, lam_k1, lam_q2, lam_k2,
             g_subln, w_out, g_ffn, peer_wq, peer_keys, peer_u, peer_v, g_final):
    depth = w_in.shape[0]
    d = xs[0].shape[-1]
    splits = [int(c) for c in np.cumsum((0,) + IN_SIZES)]
    bias_a = _bias_a_table(rel_bias[:, :A_Q_HEADS])
    tiles_c, far_c = _bias_c_tables(rel_bias[:, A_Q_HEADS:])
    shapes = [x.shape[:2] for x in xs]
    xs = [x.reshape(-1, d) for x in xs]
    for l in range(depth):
        lam_init = 0.8 - 0.6 * math.exp(-0.3 * l)
        w_l = jnp.concatenate([w_in[l][:, splits[s]:splits[s + 1]] for s in _PROJ_ORDER], axis=1).astype(BF16)
        w_o = w_out[l].astype(BF16)
        w_q = peer_wq[l].astype(BF16)
        bias_b = _bias_b_table(nat_rpb[l])
        keys = peer_keys[l].reshape(2 * PEER_HEADS, PEER_KEYS, PEER_HALF).astype(BF16)
        uv = _pack_expert_tables(peer_u[l], peer_v[l])
        for b, (n_seq, seq) in enumerate(shapes):
            x = xs[b]
            n = x.shape[0]
            kw = dict(row0=0, n_seq=n_seq, seq=seq)
            proj = _norm_proj(x, g_attn[l], w_l)
            ya = _mixer_a(proj, bias_a, a_sink[l], g_grp_a[l], **kw)
            yb = _mixer_b(proj, bias_b, g_grp_b[l], **kw)
            yc = _mixer_c(proj, tiles_c, far_c, lam_q1[l], lam_k1[l], lam_q2[l], lam_k2[l], g_subln[l],
                          lam_init=lam_init, **kw)
            x1, h2 = _out_proj(x, ya, yb, yc, w_o, g_ffn[l])
            xs[b] = _peer_route_experts(h2, x1, w_q, keys, uv, g_final, final_norm=(l == depth - 1))
    return tuple(x.reshape(n_seq, seq, d) for x, (n_seq, seq) in zip(xs, shapes))


def kernel(x_prompt, x_sample, rel_bias, g_attn, w_in, a_sink, g_grp_a, nat_rpb, g_grp_b, lam_q1, lam_k1, lam_q2, lam_k2, g_subln, w_out, g_ffn, peer_wq, peer_keys, peer_u, peer_v, g_final):
    return _encoder([x_prompt, x_sample], rel_bias, g_attn, w_in, a_sink, g_grp_a, nat_rpb, g_grp_b, lam_q1, lam_k1,
                    lam_q2, lam_k2, g_subln, w_out, g_ffn, peer_wq, peer_keys, peer_u, peer_v, g_final)
```

```python
import functools
import math

import numpy as np
import jax
import jax.numpy as jnp
from jax import lax
from jax.experimental import pallas as pl
from jax.experimental.pallas import tpu as pltpu

F32 = jnp.float32
BF16 = jnp.bfloat16
I32 = jnp.int32
EPS = 1e-6
NEG = -1e30

D_MODEL = 2048
DEPTH = 2
HEAD_DIM = 128
A_Q_HEADS = 6
A_KV_HEADS = 2
A_GROUP = A_Q_HEADS // A_KV_HEADS
WINDOW = 128
BLOCK = 128
B_HEADS = 6
GRID_W = 64
NB_KH = 8
NB_KW = 16
C_HEADS = 4
C_QK_DIM = 64
C_V_DIM = 128
A_WIDTH = A_Q_HEADS * HEAD_DIM
B_WIDTH = B_HEADS * HEAD_DIM
C_WIDTH = C_HEADS * C_V_DIM
A_KV = A_KV_HEADS * HEAD_DIM
C_QK = C_HEADS * 2 * C_QK_DIM
IN_SIZES = (A_WIDTH, A_KV, A_KV, B_WIDTH, B_WIDTH, B_WIDTH, C_QK, C_QK, C_WIDTH)
IN_WIDTH = sum(IN_SIZES)
REL_BUCKETS = 32
REL_MAX_DIST = 128
PEER_HEADS = 8
PEER_KEYS = 128
PEER_TOPK = 16
PEER_QDIM = 256
PEER_HALF = PEER_QDIM // 2
PEER_PICKS = PEER_HEADS * PEER_TOPK

_PROJ_ORDER = (0, 3, 4, 5, 1, 2, 6, 7, 8)
_COL_AQ, _COL_BQ, _COL_BK, _COL_BV = 0, 1, 2, 3
_COL_AK, _COL_AV = 12, 13
_COL_CQ, _COL_CK, _COL_CV = 7, 8, 9

_VMEM_LIMIT = 56 * 1024 * 1024


def _cparams(sem):
    return pltpu.CompilerParams(dimension_semantics=sem, vmem_limit_bytes=_VMEM_LIMIT)


def _t5_bucket(rel):
    nb = REL_BUCKETS // 2
    max_exact = nb // 2
    ret = jnp.where(rel > 0, nb, 0)
    n = jnp.abs(rel)
    nf = jnp.maximum(n, 1).astype(F32)
    large = max_exact + (jnp.log(nf / max_exact) / math.log(REL_MAX_DIST / max_exact) * (nb - max_exact)).astype(I32)
    large = jnp.minimum(large, nb - 1)
    return ret + jnp.where(n < max_exact, n, large)


def _lookup(table, idx):
    onehot = jax.nn.one_hot(idx, table.shape[0], dtype=F32)
    return jnp.einsum('...b,bc->...c', onehot, table.astype(F32), precision=lax.Precision.HIGHEST)


def _bias_a_table(rel_bias_a):
    rel = np.arange(3 * BLOCK)[None, :] - BLOCK - np.arange(BLOCK)[:, None]
    bias = _lookup(rel_bias_a, _t5_bucket(jnp.asarray(rel)))
    bias = jnp.where(jnp.asarray(np.abs(rel) <= WINDOW)[:, :, None], bias, NEG)
    return bias.transpose(2, 0, 1)


def _bias_b_table(rpb):
    classes = (((0, 1), (0, 0)), ((2, 3), (0, 0)), ((4, 5), (0, 1)), ((5, 6), (1, 1)), ((7, 8), (1, 1)))
    qc = np.arange(GRID_W)[:, None]
    kc = np.arange(GRID_W)[None, :]
    c0 = np.clip(qc - NB_KW // 2, 0, GRID_W - NB_KW)
    in_cols = (kc >= c0) & (kc < c0 + NB_KW)
    n_dc = 2 * NB_KW - 1
    dc_onehot = ((kc - qc + NB_KW - 1)[:, :, None] == np.arange(n_dc)).astype(np.float32)
    blocks = jnp.einsum('qkc,hdc->hdqk', jnp.asarray(dc_onehot), rpb.astype(F32), precision=lax.Precision.HIGHEST)
    blocks = jnp.where(jnp.asarray(in_cols), blocks, NEG)
    neg = jnp.full((rpb.shape[0], GRID_W, GRID_W), NEG, F32)
    tabs = []
    for (rq, r0) in classes:
        rows = []
        for a in range(_B_QROWS):
            parts = [blocks[:, kr - rq[a] + NB_KH - 1] if r0[a] <= kr < r0[a] + NB_KH else neg
                     for kr in range(_B_KROWS)]
            rows.append(jnp.concatenate(parts, axis=-1))
        tabs.append(jnp.concatenate(rows, axis=1))
    return jnp.stack(tabs)


def _bias_c_tables(rel_bias_c):
    q = np.arange(BLOCK)[:, None]
    k = np.arange(BLOCK)[None, :]
    far = _lookup(rel_bias_c, _t5_bucket(jnp.asarray([-2 * BLOCK, 2 * BLOCK])))
    tiles = []
    for d in (-1, 0, 1):
        rel = k + d * BLOCK - q
        t = _lookup(rel_bias_c, _t5_bucket(jnp.asarray(rel))).transpose(2, 0, 1)
        base = far[0] if d < 0 else far[1]
        tiles.append(t - base[:, None, None])
    return jnp.stack(tiles, axis=1), far.T


def _norm_proj_kernel(x_ref, g_ref, w_ref, o_ref, h_ref):
    @pl.when(pl.program_id(1) == 0)
    def _():
        x = x_ref[...]
        ms = jnp.mean(x * x, axis=-1, keepdims=True)
        h_ref[...] = (x * lax.rsqrt(ms + EPS) * g_ref[...]).astype(BF16)

    o_ref[...] = jnp.dot(h_ref[...], w_ref[...], preferred_element_type=F32).astype(o_ref.dtype)


def _norm_proj(x, g, w, *, tm=1024, tn=1280):
    n, d = x.shape
    width = w.shape[1]
    return pl.pallas_call(
        _norm_proj_kernel,
        grid=(n // tm, width // tn),
        in_specs=[
            pl.BlockSpec((tm, d), lambda i, j: (i, 0)),
            pl.BlockSpec((1, d), lambda i, j: (0, 0)),
            pl.BlockSpec((d, tn), lambda i, j: (0, j)),
        ],
        out_specs=pl.BlockSpec((tm, tn), lambda i, j: (i, j)),
        out_shape=jax.ShapeDtypeStruct((n, width), BF16),
        scratch_shapes=[pltpu.VMEM((tm, d), BF16)],
        compiler_params=_cparams(("parallel", "arbitrary")),
        name="norm_proj",
    )(x, g.reshape(1, d), w)


def _group_norm_store(o_ref, outs, g_ref):
    o = jnp.concatenate(outs, axis=-1)
    ms = jnp.mean(o * o, axis=-1, keepdims=True)
    o_ref[...] = (o * lax.rsqrt(ms + EPS) * g_ref[...]).astype(o_ref.dtype)


def _mixer_a_kernel(q_ref, kp_ref, kc_ref, kn_ref, vp_ref, vc_ref, vn_ref, bias_ref, sink_ref, g_ref, o_ref):
    i = pl.program_id(1)
    nb = pl.num_programs(1)
    scale = HEAD_DIM ** -0.5
    col = lax.broadcasted_iota(I32, (BLOCK, 3 * BLOCK), 1)
    lo = jnp.where(i == 0, BLOCK, 0)
    hi = jnp.where(i == nb - 1, 2 * BLOCK, 3 * BLOCK)
    in_seq = (col >= lo) & (col < hi)
    ks, vs, scores = [], [], []
    for kv in range(A_KV_HEADS):
        sl = slice(kv * HEAD_DIM, (kv + 1) * HEAD_DIM)
        ks.append(jnp.concatenate([kp_ref[:, sl], kc_ref[:, sl], kn_ref[:, sl]], axis=0))
        vs.append(jnp.concatenate([vp_ref[:, sl], vc_ref[:, sl], vn_ref[:, sl]], axis=0))
    for h in range(A_Q_HEADS):
        q = q_ref[:, h * HEAD_DIM:(h + 1) * HEAD_DIM]
        scores.append(lax.dot_general(q, ks[h // A_GROUP], (((1,), (1,)), ((), ())), preferred_element_type=F32))
    probs, denoms = [], []
    for h in range(A_Q_HEADS):
        s = jnp.where(in_seq, scores[h] * scale + bias_ref[h], NEG)
        sk = sink_ref[h]
        m = jnp.maximum(jnp.max(s, axis=-1, keepdims=True), sk)
        p = jnp.exp(s - m)
        denoms.append(jnp.sum(p, axis=-1, keepdims=True) + jnp.exp(sk - m))
        probs.append(p.astype(BF16))
    outs = [jnp.dot(probs[h], vs[h // A_GROUP], preferred_element_type=F32) / denoms[h] for h in range(A_Q_HEADS)]
    _group_norm_store(o_ref, outs, g_ref)


def _mixer_a(proj, bias, sink, g, *, row0, n_seq, seq):
    nb = seq // BLOCK
    b0 = row0 // BLOCK

    def blk(shift, colblk):
        return lambda b, i: (b0 + b * nb + jnp.clip(i + shift, 0, nb - 1), colblk)

    kv_spec = lambda shift, colblk: pl.BlockSpec((BLOCK, A_KV), blk(shift, colblk))
    return pl.pallas_call(
        _mixer_a_kernel,
        grid=(n_seq, nb),
        in_specs=[
            pl.BlockSpec((BLOCK, A_WIDTH), blk(0, _COL_AQ)),
            kv_spec(-1, _COL_AK), kv_spec(0, _COL_AK), kv_spec(1, _COL_AK),
            kv_spec(-1, _COL_AV), kv_spec(0, _COL_AV), kv_spec(1, _COL_AV),
            pl.BlockSpec((A_Q_HEADS, BLOCK, 3 * BLOCK), lambda b, i: (0, 0, 0)),
            pl.BlockSpec(memory_space=pltpu.SMEM),
            pl.BlockSpec((1, A_WIDTH), lambda b, i: (0, 0)),
        ],
        out_specs=pl.BlockSpec((BLOCK, A_WIDTH), lambda b, i: (b * nb + i, 0)),
        out_shape=jax.ShapeDtypeStruct((n_seq * seq, A_WIDTH), BF16),
        compiler_params=_cparams(("parallel", "parallel")),
        name="mixer_a",
    )(proj, proj, proj, proj, proj, proj, proj, bias, sink, g.reshape(1, A_WIDTH))


_B_QROWS = 2
_B_KROWS = NB_KH + _B_QROWS - 1


def _mixer_b_kernel(q_ref, k_ref, v_ref, bias_ref, g_ref, o_ref, *, rows):
    i = pl.program_id(1)
    scale = HEAD_DIM ** -0.5
    kstart = jnp.clip(_B_QROWS * i - NB_KH // 2, 0, rows - _B_KROWS)
    ks = pl.multiple_of(kstart * GRID_W, GRID_W)
    nk = _B_KROWS * GRID_W
    heads = [slice(h * HEAD_DIM, (h + 1) * HEAD_DIM) for h in range(B_HEADS)]
    scores = [lax.dot_general(q_ref[:, sl], k_ref[pl.ds(ks, nk), sl], (((1,), (1,)), ((), ())),
                              preferred_element_type=F32) for sl in heads]
    probs, denoms = [], []
    for h in range(B_HEADS):
        s = scores[h] * scale + bias_ref[0, h]
        p = jnp.exp(s - jnp.max(s, axis=-1, keepdims=True))
        denoms.append(jnp.sum(p, axis=-1, keepdims=True))
        probs.append(p.astype(BF16))
    outs = [jnp.dot(probs[h], v_ref[pl.ds(ks, nk), sl], preferred_element_type=F32) / denoms[h]
            for h, sl in enumerate(heads)]
    _group_norm_store(o_ref, outs, g_ref)


def _mixer_b(proj, bias, g, *, row0, n_seq, seq):
    rows = seq // GRID_W
    nrp = rows // _B_QROWS
    tq = _B_QROWS * GRID_W
    b0 = row0 // tq
    s0 = row0 // seq

    def cls(i):
        return jnp.where(i == 0, 0, jnp.where(i == 1, 1, jnp.where(i == nrp - 2, 3, jnp.where(i == nrp - 1, 4, 2))))

    return pl.pallas_call(
        functools.partial(_mixer_b_kernel, rows=rows),
        grid=(n_seq, nrp),
        in_specs=[
            pl.BlockSpec((tq, B_WIDTH), lambda b, i: (b0 + b * nrp + i, _COL_BQ)),
            pl.BlockSpec((seq, B_WIDTH), lambda b, i: (s0 + b, _COL_BK)),
            pl.BlockSpec((seq, B_WIDTH), lambda b, i: (s0 + b, _COL_BV)),
            pl.BlockSpec((1, B_HEADS, tq, _B_KROWS * GRID_W), lambda b, i: (cls(i), 0, 0, 0)),
            pl.BlockSpec((1, B_WIDTH), lambda b, i: (0, 0)),
        ],
        out_specs=pl.BlockSpec((tq, B_WIDTH), lambda b, i: (b * nrp + i, 0)),
        out_shape=jax.ShapeDtypeStruct((n_seq * seq, B_WIDTH), BF16),
        compiler_params=_cparams(("parallel", "arbitrary")),
        name="mixer_b",
    )(proj, proj, proj, bias, g.reshape(1, B_WIDTH))


def _mixer_c_kernel(q_ref, k_ref, v_ref, tiles_ref, far_ref, lq1_ref, lk1_ref, lq2_ref, lk2_ref, g_ref, o_ref,
                    *s_refs, seq, lam_init):
    i = pl.program_id(1)
    nb = seq // BLOCK
    scale = C_QK_DIM ** -0.5
    lam = (jnp.exp(jnp.sum(lq1_ref[...] * lk1_ref[...], axis=-1, keepdims=True))
           - jnp.exp(jnp.sum(lq2_ref[...] * lk2_ref[...], axis=-1, keepdims=True)) + lam_init)
    before = lax.broadcasted_iota(I32, (1, seq), 1) < i * BLOCK
    lane = lax.broadcasted_iota(I32, (BLOCK, HEAD_DIM), 1)
    for h in range(C_HEADS):
        sl = slice(h * HEAD_DIM, (h + 1) * HEAD_DIM)
        qh = q_ref[:, sl] * scale
        kh = k_ref[:, sl]
        qs = jnp.concatenate(
            [jnp.where((lane >= c * C_QK_DIM) & (lane < (c + 1) * C_QK_DIM), qh, jnp.zeros_like(qh)) for c in range(2)],
            axis=0)
        s = lax.dot_general(qs, kh, (((1,), (1,)), ((), ())), preferred_element_type=F32)
        for c in range(2):
            hc = 2 * h + c
            s_refs[hc][...] = s[c * BLOCK:(c + 1) * BLOCK] + jnp.where(before, far_ref[hc, 0], far_ref[hc, 1])
    outs = []
    for h in range(C_HEADS):
        sl = slice(h * HEAD_DIM, (h + 1) * HEAD_DIM)
        exps, sums = [], []
        for c in range(2):
            hc = 2 * h + c
            for d in range(3):
                j = i + (d - 1)
                inside = jnp.where((j >= 0) & (j < nb), 1.0, 0.0)
                cols = pl.ds(pl.multiple_of(jnp.clip(j, 0, nb - 1) * BLOCK, BLOCK), BLOCK)
                s_refs[hc][:, cols] += tiles_ref[hc, d] * inside
            s = s_refs[hc][...]
            e = jnp.exp(s - jnp.max(s, axis=-1, keepdims=True))
            exps.append(e)
            sums.append(jnp.sum(e, axis=-1, keepdims=True))
        a = (exps[0] - (lam * sums[0] / sums[1]) * exps[1]).astype(BF16)
        o = jnp.dot(a, v_ref[:, sl], preferred_element_type=F32) / sums[0]
        ms = jnp.mean(o * o, axis=-1, keepdims=True)
        outs.append(o * lax.rsqrt(ms + EPS) * g_ref[...] * (1.0 - lam_init))
    o_ref[...] = jnp.concatenate(outs, axis=-1).astype(o_ref.dtype)


def _mixer_c(proj, tiles, far, lq1, lk1, lq2, lk2, g, *, row0, n_seq, seq, lam_init):
    nb = seq // BLOCK
    b0 = row0 // BLOCK
    s0 = row0 // seq
    vec = lambda a: a.reshape(1, C_QK_DIM)
    vspec = pl.BlockSpec((1, C_QK_DIM), lambda b, i: (0, 0))
    return pl.pallas_call(
        functools.partial(_mixer_c_kernel, seq=seq, lam_init=lam_init),
        grid=(n_seq, nb),
        in_specs=[
            pl.BlockSpec((BLOCK, C_QK), lambda b, i: (b0 + b * nb + i, _COL_CQ)),
            pl.BlockSpec((seq, C_QK), lambda b, i: (s0 + b, _COL_CK)),
            pl.BlockSpec((seq, C_WIDTH), lambda b, i: (s0 + b, _COL_CV)),
            pl.BlockSpec((2 * C_HEADS, 3, BLOCK, BLOCK), lambda b, i: (0, 0, 0, 0)),
            pl.BlockSpec(memory_space=pltpu.SMEM),
            vspec, vspec, vspec, vspec,
            pl.BlockSpec((1, C_V_DIM), lambda b, i: (0, 0)),
        ],
        out_specs=pl.BlockSpec((BLOCK, C_WIDTH), lambda b, i: (b * nb + i, 0)),
        out_shape=jax.ShapeDtypeStruct((n_seq * seq, C_WIDTH), BF16),
        scratch_shapes=[pltpu.VMEM((BLOCK, seq), F32)] * (2 * C_HEADS),
        compiler_params=_cparams(("parallel", "arbitrary")),
        name="mixer_c",
    )(proj, proj, proj, tiles, far, vec(lq1), vec(lk1), vec(lq2), vec(lk2), g.reshape(1, C_V_DIM))


def _out_proj_kernel(x_ref, ya_ref, yb_ref, yc_ref, w_ref, g_ref, x1_ref, h_ref):
    acc = jnp.dot(ya_ref[...], w_ref[0:A_WIDTH, :], preferred_element_type=F32)
    acc += jnp.dot(yb_ref[...], w_ref[A_WIDTH:A_WIDTH + B_WIDTH, :], preferred_element_type=F32)
    acc += jnp.dot(yc_ref[...], w_ref[A_WIDTH + B_WIDTH:, :], preferred_element_type=F32)
    x1 = x_ref[...] + acc
    x1_ref[...] = x1
    ms = jnp.mean(x1 * x1, axis=-1, keepdims=True)
    h_ref[...] = x1 * lax.rsqrt(ms + EPS) * g_ref[...]


def _out_proj(x, ya, yb, yc, w, g, *, tm=512):
    n, d = x.shape
    row = lambda width: pl.BlockSpec((tm, width), lambda i: (i, 0))
    return pl.pallas_call(
        _out_proj_kernel,
        grid=(n // tm,),
        in_specs=[row(d), row(A_WIDTH), row(B_WIDTH), row(C_WIDTH),
                  pl.BlockSpec((d, d), lambda i: (0, 0)),
                  pl.BlockSpec((1, d), lambda i: (0, 0))],
        out_specs=[row(d), row(d)],
        out_shape=[jax.ShapeDtypeStruct((n, d), F32), jax.ShapeDtypeStruct((n, d), F32)],
        compiler_params=_cparams(("parallel",)),
        name="out_proj",
    )(x, ya, yb, yc, w, g.reshape(1, d))


def _top16(x):
    r = float(x.shape[0])
    iota = lax.broadcasted_iota(I32, x.shape, 0).astype(F32)
    vals, idxs = [], []
    for _ in range(PEER_TOPK):
        m = jnp.max(x, axis=0, keepdims=True)
        idx = jnp.min(jnp.where(x == m, iota, r), axis=0, keepdims=True)
        vals.append(m)
        idxs.append(idx)
        x = jnp.where(iota == idx, -jnp.inf, x)
    return vals, idxs


def _peer_candidates(s1, i1, s2, i2):
    sub = 8
    s1a, i1a = jnp.concatenate(s1, axis=0), jnp.concatenate(i1, axis=0)
    s2a, i2a = jnp.concatenate(s2, axis=0), jnp.concatenate(i2, axis=0)
    row = lax.broadcasted_iota(I32, s2a[:sub].shape, 0)
    cand, cidx = [], []
    first_single = PEER_TOPK // 2
    for a in range(first_single):
        nb = PEER_TOPK // (a + 1)
        if nb > sub:
            cand.append(s1[a] + s2a)
            cidx.append(i1[a] * PEER_KEYS + i2a)
        else:
            cand.append(jnp.where(row < nb, s1[a] + s2a[:sub], -jnp.inf))
            cidx.append(i1[a] * PEER_KEYS + i2a[:sub])
    cand.append(s1a[first_single:] + s2[0])
    cidx.append(i1a[first_single:] * PEER_KEYS + i2[0])
    return jnp.concatenate(cand, axis=0), jnp.concatenate(cidx, axis=0)


def _gelu_exact(x):
    return 0.5 * x * (1.0 + lax.erf(x * (2.0 ** -0.5)))


_PEER_T = 8
_DMA_QUEUES = 2


def _pack_expert_tables(u, v):
    half = u.shape[1] // 2
    bits = lambda a: lax.bitcast_convert_type(a.astype(BF16), jnp.uint16).astype(jnp.uint32)
    lo = jnp.concatenate([bits(u[:, :half]), bits(v[:, :half])], axis=1)
    hi = jnp.concatenate([bits(u[:, half:]), bits(v[:, half:])], axis=1)
    return lax.bitcast_convert_type(lo | (hi << 16), I32)


_MACRO = 256
_MACRO_STEPS = _MACRO // _PEER_T
_ROUTE_PHASES = 3 * PEER_HEADS + 2


def _route_experts_kernel(h_ref, x_ref, gf_ref, hall_hbm, wq_hbm, keys_hbm, uv_hbm, o_ref,
                          wq_ref, keys_ref, hbuf, q_ref, sub_s, sub_i, idx_pk, gate_pk, g2_ref, idx_tok, gate_tok,
                          idx_s0, idx_s1, buf0, buf1, a_ref, w_sem, idx_sem, row_sem, *, final_norm, n_tiles):
    i = pl.program_id(0)
    n = pl.num_programs(0)
    bufs = (buf0, buf1)
    idx_smem = (idx_s0, idx_s1)
    half = h_ref.shape[1] // 2

    def h_copy(mt):
        return pltpu.make_async_copy(hall_hbm.at[pl.ds(pl.multiple_of(mt * _MACRO, _MACRO), _MACRO)], hbuf, w_sem)

    def phase_project(mt):
        h_copy(mt).wait()
        q_ref[...] = jnp.dot(hbuf[...].astype(BF16), wq_ref[...], preferred_element_type=F32).astype(BF16)

    def phase_half(hc, mt):
        del mt
        qc = q_ref[:, hc * PEER_HALF:(hc + 1) * PEER_HALF]
        sc = lax.dot_general(keys_ref[hc], qc, (((1,), (1,)), ((), ())), preferred_element_type=F32)
        vals, idxs = _top16(sc)
        sub_s[hc % 2] = jnp.concatenate(vals, axis=0)
        sub_i[hc % 2] = jnp.concatenate(idxs, axis=0)

    def phase_head(hp, mt):
        del mt
        rows16 = lambda ref, c: [ref[c, j:j + 1, :] for j in range(PEER_TOPK)]
        s1, i1, s2, i2 = rows16(sub_s, 0), rows16(sub_i, 0), rows16(sub_s, 1), rows16(sub_i, 1)
        cand, cidx = _peer_candidates(s1, i1, s2, i2)
        top_s, pos = _top16(cand)
        iota = lax.broadcasted_iota(I32, cand.shape, 0).astype(F32)
        eidx = [jnp.sum(jnp.where(iota == p, cidx, 0.0), axis=0, keepdims=True) for p in pos]
        ts = jnp.concatenate(top_s, axis=0)
        e = jnp.exp(ts - top_s[0])
        rows = slice(hp * PEER_TOPK, (hp + 1) * PEER_TOPK)
        gate_pk[rows, :] = e / jnp.sum(e, axis=0, keepdims=True)
        idx_pk[rows, :] = jnp.concatenate(eidx, axis=0)

    def phase_finalize(mt):
        slot = mt % 2
        lanes = 128
        for hf in range(_MACRO // lanes):
            toks = slice(hf * lanes, (hf + 1) * lanes)
            g = gate_pk[:, toks]
            g2_ref[hf, pl.ds(0, PEER_PICKS, stride=2), :] = g
            g2_ref[hf, pl.ds(1, PEER_PICKS, stride=2), :] = jnp.zeros_like(g)
            gate_tok[slot, toks, :] = g2_ref[hf].T
            idx_tok[slot, toks, :] = (idx_pk[:, toks].astype(I32) * D_MODEL).T

        @pl.when(mt + 1 < n_tiles)
        def _():
            h_copy(mt + 1).start()

    phases = [phase_project]
    for hp in range(PEER_HEADS):
        phases += [functools.partial(phase_half, 2 * hp), functools.partial(phase_half, 2 * hp + 1),
                   functools.partial(phase_head, hp)]
    phases.append(phase_finalize)
    assert len(phases) == _ROUTE_PHASES <= _MACRO_STEPS

    def run_phase(ph, mt):
        lax.switch(ph, [lambda mt, f=f: f(mt) for f in phases], mt)

    def idx_copy(step, slot):
        src = idx_tok.at[(step // _MACRO_STEPS) % 2, pl.ds(pl.multiple_of((step % _MACRO_STEPS) * _PEER_T, _PEER_T), _PEER_T)]
        return pltpu.make_async_copy(src, idx_smem[slot], idx_sem.at[slot])

    def issue_token(slot, t):
        for k in range(PEER_PICKS):
            off = pl.multiple_of(idx_smem[slot][t, k], D_MODEL)
            copy = pltpu.make_async_copy(uv_hbm.at[pl.ds(off, D_MODEL)], bufs[slot].at[t, k], row_sem.at[slot])
            copy.start(priority=k % _DMA_QUEUES)

    def wait_rows(slot):
        pltpu.make_async_copy(bufs[slot], bufs[slot], row_sem.at[slot]).wait()

    @pl.when(i == 0)
    def _():
        cw = pltpu.make_async_copy(wq_hbm, wq_ref, w_sem)
        cw.start()
        cw.wait()
        ck = pltpu.make_async_copy(keys_hbm, keys_ref, w_sem)
        ck.start()
        ck.wait()
        h_copy(0).start()

    def routed(r, carry):
        first = (i == 0) & (r < _ROUTE_PHASES)
        ph = jnp.where(first, r, i % _MACRO_STEPS)
        mt = jnp.where(first, 0, i // _MACRO_STEPS + 1)

        @pl.when(first | ((ph < _ROUTE_PHASES) & (mt < n_tiles)))
        def _():
            run_phase(ph, mt)
        return carry
    lax.fori_loop(0, jnp.where(i == 0, _ROUTE_PHASES + 1, 1), routed, 0)

    @pl.when(i == 0)
    def _():
        idx_copy(0, 0).start()
        idx_copy(0, 0).wait()
        lax.fori_loop(0, _PEER_T, lambda t, c: (issue_token(0, t), c)[1], 0)
        idx_copy(1, 1).start()

    def step(cur, nxt):
        idx_copy(i + 1, nxt).wait()
        idx_copy(i + 2, cur).start()
        wait_rows(cur)
        buf = bufs[cur]
        tile_slot = (i // _MACRO_STEPS) % 2
        tile_row = pl.multiple_of((i % _MACRO_STEPS) * _PEER_T, _PEER_T)

        h = h_ref[...]
        hh = jnp.concatenate([h[:, :half], h[:, half:]], axis=0).astype(BF16)

        @pl.when(i >= 0)
        def _():
            for t in range(_PEER_T // 2):
                issue_token(nxt, t)
            tok = lax.broadcasted_iota(I32, (_PEER_T, 2 * PEER_PICKS), 0)
            a = jnp.zeros((_PEER_T, 2 * PEER_PICKS), F32)
            for t in range(_PEER_T):
                urows = pltpu.bitcast(buf[t, :, :half], BF16)
                r = lax.dot_general(hh, urows, (((1,), (1,)), ((), ())), preferred_element_type=F32)
                r = r[:_PEER_T] + pltpu.roll(r[_PEER_T:], 2 * PEER_PICKS - 1, axis=1)
                a = jnp.where(tok == t, r, a)
            a_ref[...] = a

        for t in range(_PEER_T // 2, _PEER_T):
            issue_token(nxt, t)
        w_even = gate_tok[tile_slot, pl.ds(tile_row, _PEER_T), :] * _gelu_exact(a_ref[...])
        w = jnp.concatenate([w_even, pltpu.roll(w_even, 1, axis=1)], axis=0).astype(BF16)
        tok = lax.broadcasted_iota(I32, (_PEER_T, half), 0)
        lo = jnp.zeros((_PEER_T, half), F32)
        hi = jnp.zeros((_PEER_T, half), F32)
        for t in range(_PEER_T):
            vrows = pltpu.bitcast(buf[t, :, half:], BF16)
            o = jnp.dot(w, vrows, preferred_element_type=F32)
            lo = jnp.where(tok == t, o[:_PEER_T], lo)
            hi = jnp.where(tok == t, o[_PEER_T:], hi)
        y = x_ref[...] + jnp.concatenate([lo, hi], axis=1)
        if final_norm:
            ms = jnp.mean(y * y, axis=-1, keepdims=True)
            y = y * lax.rsqrt(ms + EPS) * gf_ref[...]
        o_ref[...] = y

        @pl.when(i == n - 1)
        def _():
            wait_rows(nxt)
            idx_copy(i + 2, cur).wait()

    for parity in range(2):
        pl.when(i % 2 == parity)(functools.partial(step, parity, 1 - parity))


def _peer_route_experts(h, x, wq, keys, uv, g_final, *, final_norm):
    n, d = x.shape
    assert n % _MACRO == 0 and n // _MACRO >= 2
    steps = n // _PEER_T
    row = lambda width: pl.BlockSpec((_PEER_T, width), lambda i: (i, 0))
    anyspec = pl.BlockSpec(memory_space=pl.ANY)
    return pl.pallas_call(
        functools.partial(_route_experts_kernel, final_norm=final_norm, n_tiles=n // _MACRO),
        grid=(steps,),
        in_specs=[row(d), row(d), pl.BlockSpec((1, d), lambda i: (0, 0)), anyspec, anyspec, anyspec, anyspec],
        out_specs=row(d),
        out_shape=jax.ShapeDtypeStruct((n, d), F32),
        scratch_shapes=[pltpu.VMEM(wq.shape, BF16),
                        pltpu.VMEM(keys.shape, BF16),
                        pltpu.VMEM((_MACRO, d), F32),
                        pltpu.VMEM((_MACRO, wq.shape[1]), BF16),
                        pltpu.VMEM((2, PEER_TOPK, _MACRO), F32),
                        pltpu.VMEM((2, PEER_TOPK, _MACRO), F32),
                        pltpu.VMEM((PEER_PICKS, _MACRO), F32),
                        pltpu.VMEM((PEER_PICKS, _MACRO), F32),
                        pltpu.VMEM((_MACRO // 128, 2 * PEER_PICKS, 128), F32),
                        pltpu.VMEM((2, _MACRO, PEER_PICKS), I32),
                        pltpu.VMEM((2, _MACRO, 2 * PEER_PICKS), F32),
                        pltpu.SMEM((_PEER_T, PEER_PICKS), I32),
                        pltpu.SMEM((_PEER_T, PEER_PICKS), I32),
                        pltpu.VMEM((_PEER_T, PEER_PICKS, d), I32),
                        pltpu.VMEM((_PEER_T, PEER_PICKS, d), I32),
                        pltpu.VMEM((_PEER_T, 2 * PEER_PICKS), F32),
                        pltpu.SemaphoreType.DMA,
                        pltpu.SemaphoreType.DMA((2,)),
                        pltpu.SemaphoreType.DMA((2,))],
        compiler_params=_cparams(("arbitrary",)),
        name="peer_route_experts",
    )(h, x, g_final.reshape(1, d), h, wq, keys, uv.reshape(-1))


def _encoder(xs, rel_bias, g_attn, w_in, a_sink, g_grp_a, nat_rpb, g_grp_b, lam_q1, lam_k1, lam_q2, lam_k2,
             g_subln, w_out, g_ffn, peer_wq, peer_keys, peer_u, peer_v, g_final):
    depth = w_in.shape[0]
    d = xs[0].shape[-1]
    splits = [int(c) for c in np.cumsum((0,) + IN_SIZES)]
    bias_a = _bias_a_table(rel_bias[:, :A_Q_HEADS])
    tiles_c, far_c = _bias_c_tables(rel_bias[:, A_Q_HEADS:])
    shapes = [x.shape[:2] for x in xs]
    xs = [x.reshape(-1, d) for x in xs]
    for l in range(depth):
        lam_init = 0.8 - 0.6 * math.exp(-0.3 * l)
        w_l = jnp.concatenate([w_in[l][:, splits[s]:splits[s + 1]] for s in _PROJ_ORDER], axis=1).astype(BF16)
        w_o = w_out[l].astype(BF16)
        w_q = peer_wq[l].astype(BF16)
        bias_b = _bias_b_table(nat_rpb[l])
        keys = peer_keys[l].reshape(2 * PEER_HEADS, PEER_KEYS, PEER_HALF).astype(BF16)
        uv = _pack_expert_tables(peer_u[l], peer_v[l])
        for b, (n_seq, seq) in enumerate(shapes):
            x = xs[b]
            n = x.shape[0]
            kw = dict(row0=0, n_seq=n_seq, seq=seq)
            proj = _norm_proj(x, g_attn[l], w_l)
            ya = _mixer_a(proj, bias_a, a_sink[l], g_grp_a[l], **kw)
            yb = _mixer_b(proj, bias_b, g_grp_b[l], **kw)
            yc = _mixer_c(proj, tiles_c, far_c, lam_q1[l], lam_k1[l], lam_q2[l], lam_k2[l], g_subln[l],
                          lam_init=lam_init, **kw)
            x1, h2 = _out_proj(x, ya, yb, yc, w_o, g_ffn[l])
            xs[b] = _peer_route_experts(h2, x1, w_q, keys, uv, g_final, final_norm=(l == depth - 1))
    return tuple(x.reshape(n_seq, seq, d) for x, (n_seq, seq) in zip(xs, shapes))


def kernel(x_prompt, x_sample, rel_bias, g_attn, w_in, a_sink, g_grp_a, nat_rpb, g_grp_b, lam_q1, lam_k1, lam_q2, lam_k2, g_subln, w_out, g_ffn, peer_wq, peer_keys, peer_u, peer_v, g_final):
    return _encoder([x_prompt, x_sample], rel_bias, g_attn, w_in, a_sink, g_grp_a, nat_rpb, g_grp_b, lam_q1, lam_k1,
                    lam_q2, lam_k2, g_subln, w_out, g_ffn, peer_wq, peer_keys, peer_u, peer_v, g_final)
```

```python
import functools
import math

import numpy as np
import jax
import jax.numpy as jnp
from jax import lax
from jax.experimental import pallas as pl
from jax.experimental.pallas import tpu as pltpu

F32 = jnp.float32
BF16 = jnp.bfloat16
I32 = jnp.int32
EPS = 1e-6
NEG = -1e30

D_MODEL = 2048
DEPTH = 2
HEAD_DIM = 128
A_Q_HEADS = 6
A_KV_HEADS = 2
A_GROUP = A_Q_HEADS // A_KV_HEADS
WINDOW = 128
BLOCK = 128
B_HEADS = 6
GRID_W = 64
NB_KH = 8
NB_KW = 16
C_HEADS = 4
C_QK_DIM = 64
C_V_DIM = 128
A_WIDTH = A_Q_HEADS * HEAD_DIM
B_WIDTH = B_HEADS * HEAD_DIM
C_WIDTH = C_HEADS * C_V_DIM
A_KV = A_KV_HEADS * HEAD_DIM
C_QK = C_HEADS * 2 * C_QK_DIM
IN_SIZES = (A_WIDTH, A_KV, A_KV, B_WIDTH, B_WIDTH, B_WIDTH, C_QK, C_QK, C_WIDTH)
IN_WIDTH = sum(IN_SIZES)
REL_BUCKETS = 32
REL_MAX_DIST = 128
PEER_HEADS = 8
PEER_KEYS = 128
PEER_TOPK = 16
PEER_QDIM = 256
PEER_HALF = PEER_QDIM // 2
PEER_PICKS = PEER_HEADS * PEER_TOPK

_PROJ_ORDER = (0, 3, 4, 5, 1, 2, 6, 7, 8)
_COL_AQ, _COL_BQ, _COL_BK, _COL_BV = 0, 1, 2, 3
_COL_AK, _COL_AV = 12, 13
_COL_CQ, _COL_CK, _COL_CV = 7, 8, 9

_VMEM_LIMIT = 56 * 1024 * 1024


def _cparams(sem):
    return pltpu.CompilerParams(dimension_semantics=sem, vmem_limit_bytes=_VMEM_LIMIT)


def _t5_bucket(rel):
    nb = REL_BUCKETS // 2
    max_exact = nb // 2
    ret = jnp.where(rel > 0, nb, 0)
    n = jnp.abs(rel)
    nf = jnp.maximum(n, 1).astype(F32)
    large = max_exact + (jnp.log(nf / max_exact) / math.log(REL_MAX_DIST / max_exact) * (nb - max_exact)).astype(I32)
    large = jnp.minimum(large, nb - 1)
    return ret + jnp.where(n < max_exact, n, large)


def _lookup(table, idx):
    onehot = jax.nn.one_hot(idx, table.shape[0], dtype=F32)
    return jnp.einsum('...b,bc->...c', onehot, table.astype(F32), precision=lax.Precision.HIGHEST)


def _bias_a_table(rel_bias_a):
    rel = np.arange(3 * BLOCK)[None, :] - BLOCK - np.arange(BLOCK)[:, None]
    bias = _lookup(rel_bias_a, _t5_bucket(jnp.asarray(rel)))
    bias = jnp.where(jnp.asarray(np.abs(rel) <= WINDOW)[:, :, None], bias, NEG)
    return bias.transpose(2, 0, 1)


def _bias_b_table(rpb):
    classes = (((0, 1), (0, 0)), ((2, 3), (0, 0)), ((4, 5), (0, 1)), ((5, 6), (1, 1)), ((7, 8), (1, 1)))
    qc = np.arange(GRID_W)[:, None]
    kc = np.arange(GRID_W)[None, :]
    c0 = np.clip(qc - NB_KW // 2, 0, GRID_W - NB_KW)
    in_cols = (kc >= c0) & (kc < c0 + NB_KW)
    n_dc = 2 * NB_KW - 1
    dc_onehot = ((kc - qc + NB_KW - 1)[:, :, None] == np.arange(n_dc)).astype(np.float32)
    blocks = jnp.einsum('qkc,hdc->hdqk', jnp.asarray(dc_onehot), rpb.astype(F32), precision=lax.Precision.HIGHEST)
    blocks = jnp.where(jnp.asarray(in_cols), blocks, NEG)
    neg = jnp.full((rpb.shape[0], GRID_W, GRID_W), NEG, F32)
    tabs = []
    for (rq, r0) in classes:
        rows = []
        for a in range(_B_QROWS):
            parts = [blocks[:, kr - rq[a] + NB_KH - 1] if r0[a] <= kr < r0[a] + NB_KH else neg
                     for kr in range(_B_KROWS)]
            rows.append(jnp.concatenate(parts, axis=-1))
        tabs.append(jnp.concatenate(rows, axis=1))
    return jnp.stack(tabs)


def _bias_c_tables(rel_bias_c):
    q = np.arange(BLOCK)[:, None]
    k = np.arange(BLOCK)[None, :]
    far = _lookup(rel_bias_c, _t5_bucket(jnp.asarray([-2 * BLOCK, 2 * BLOCK])))
    tiles = []
    for d in (-1, 0, 1):
        rel = k + d * BLOCK - q
        t = _lookup(rel_bias_c, _t5_bucket(jnp.asarray(rel))).transpose(2, 0, 1)
        base = far[0] if d < 0 else far[1]
        tiles.append(t - base[:, None, None])
    return jnp.stack(tiles, axis=1), far.T


def _norm_proj_kernel(x_ref, g_ref, w_ref, o_ref, h_ref):
    @pl.when(pl.program_id(1) == 0)
    def _():
        x = x_ref[...]
        ms = jnp.mean(x * x, axis=-1, keepdims=True)
        h_ref[...] = (x * lax.rsqrt(ms + EPS) * g_ref[...]).astype(BF16)

    o_ref[...] = jnp.dot(h_ref[...], w_ref[...], preferred_element_type=F32).astype(o_ref.dtype)


def _norm_proj(x, g, w, *, tm=1024, tn=1280):
    n, d = x.shape
    width = w.shape[1]
    return pl.pallas_call(
        _norm_proj_kernel,
        grid=(n // tm, width // tn),
        in_specs=[
            pl.BlockSpec((tm, d), lambda i, j: (i, 0)),
            pl.BlockSpec((1, d), lambda i, j: (0, 0)),
            pl.BlockSpec((d, tn), lambda i, j: (0, j)),
        ],
        out_specs=pl.BlockSpec((tm, tn), lambda i, j: (i, j)),
        out_shape=jax.ShapeDtypeStruct((n, width), BF16),
        scratch_shapes=[pltpu.VMEM((tm, d), BF16)],
        compiler_params=_cparams(("parallel", "arbitrary")),
        name="norm_proj",
    )(x, g.reshape(1, d), w)


def _group_norm_store(o_ref, outs, g_ref):
    o = jnp.concatenate(outs, axis=-1)
    ms = jnp.mean(o * o, axis=-1, keepdims=True)
    o_ref[...] = (o * lax.rsqrt(ms + EPS) * g_ref[...]).astype(o_ref.dtype)


def _mixer_a_kernel(q_ref, kp_ref, kc_ref, kn_ref, vp_ref, vc_ref, vn_ref, bias_ref, sink_ref, g_ref, o_ref):
    i = pl.program_id(1)
    nb = pl.num_programs(1)
    scale = HEAD_DIM ** -0.5
    col = lax.broadcasted_iota(I32, (BLOCK, 3 * BLOCK), 1)
    lo = jnp.where(i == 0, BLOCK, 0)
    hi = jnp.where(i == nb - 1, 2 * BLOCK, 3 * BLOCK)
    in_seq = (col >= lo) & (col < hi)
    ks, vs, scores = [], [], []
    for kv in range(A_KV_HEADS):
        sl = slice(kv * HEAD_DIM, (kv + 1) * HEAD_DIM)
        ks.append(jnp.concatenate([kp_ref[:, sl], kc_ref[:, sl], kn_ref[:, sl]], axis=0))
        vs.append(jnp.concatenate([vp_ref[:, sl], vc_ref[:, sl], vn_ref[:, sl]], axis=0))
    for h in range(A_Q_HEADS):
        q = q_ref[:, h * HEAD_DIM:(h + 1) * HEAD_DIM]
        scores.append(lax.dot_general(q, ks[h // A_GROUP], (((1,), (1,)), ((), ())), preferred_element_type=F32))
    probs, denoms = [], []
    for h in range(A_Q_HEADS):
        s = jnp.where(in_seq, scores[h] * scale + bias_ref[h], NEG)
        sk = sink_ref[h]
        m = jnp.maximum(jnp.max(s, axis=-1, keepdims=True), sk)
        p = jnp.exp(s - m)
        denoms.append(jnp.sum(p, axis=-1, keepdims=True) + jnp.exp(sk - m))
        probs.append(p.astype(BF16))
    outs = [jnp.dot(probs[h], vs[h // A_GROUP], preferred_element_type=F32) / denoms[h] for h in range(A_Q_HEADS)]
    _group_norm_store(o_ref, outs, g_ref)


def _mixer_a(proj, bias, sink, g, *, row0, n_seq, seq):
    nb = seq // BLOCK
    b0 = row0 // BLOCK

    def blk(shift, colblk):
        return lambda b, i: (b0 + b * nb + jnp.clip(i + shift, 0, nb - 1), colblk)

    kv_spec = lambda shift, colblk: pl.BlockSpec((BLOCK, A_KV), blk(shift, colblk))
    return pl.pallas_call(
        _mixer_a_kernel,
        grid=(n_seq, nb),
        in_specs=[
            pl.BlockSpec((BLOCK, A_WIDTH), blk(0, _COL_AQ)),
            kv_spec(-1, _COL_AK), kv_spec(0, _COL_AK), kv_spec(1, _COL_AK),
            kv_spec(-1, _COL_AV), kv_spec(0, _COL_AV), kv_spec(1, _COL_AV),
            pl.BlockSpec((A_Q_HEADS, BLOCK, 3 * BLOCK), lambda b, i: (0, 0, 0)),
            pl.BlockSpec(memory_space=pltpu.SMEM),
            pl.BlockSpec((1, A_WIDTH), lambda b, i: (0, 0)),
        ],
        out_specs=pl.BlockSpec((BLOCK, A_WIDTH), lambda b, i: (b * nb + i, 0)),
        out_shape=jax.ShapeDtypeStruct((n_seq * seq, A_WIDTH), BF16),
        compiler_params=_cparams(("parallel", "parallel")),
        name="mixer_a",
    )(proj, proj, proj, proj, proj, proj, proj, bias, sink, g.reshape(1, A_WIDTH))


_B_QROWS = 2
_B_KROWS = NB_KH + _B_QROWS - 1


def _mixer_b_kernel(q_ref, k_ref, v_ref, bias_ref, g_ref, o_ref, *, rows):
    i = pl.program_id(1)
    scale = HEAD_DIM ** -0.5
    kstart = jnp.clip(_B_QROWS * i - NB_KH // 2, 0, rows - _B_KROWS)
    ks = pl.multiple_of(kstart * GRID_W, GRID_W)
    nk = _B_KROWS * GRID_W
    heads = [slice(h * HEAD_DIM, (h + 1) * HEAD_DIM) for h in range(B_HEADS)]
    scores = [lax.dot_general(q_ref[:, sl], k_ref[pl.ds(ks, nk), sl], (((1,), (1,)), ((), ())),
                              preferred_element_type=F32) for sl in heads]
    probs, denoms = [], []
    for h in range(B_HEADS):
        s = scores[h] * scale + bias_ref[0, h]
        p = jnp.exp(s - jnp.max(s, axis=-1, keepdims=True))
        denoms.append(jnp.sum(p, axis=-1, keepdims=True))
        probs.append(p.astype(BF16))
    outs = [jnp.dot(probs[h], v_ref[pl.ds(ks, nk), sl], preferred_element_type=F32) / denoms[h]
            for h, sl in enumerate(heads)]
    _group_norm_store(o_ref, outs, g_ref)


def _mixer_b(proj, bias, g, *, row0, n_seq, seq):
    rows = seq // GRID_W
    nrp = rows // _B_QROWS
    tq = _B_QROWS * GRID_W
    b0 = row0 // tq
    s0 = row0 // seq

    def cls(i):
        return jnp.where(i == 0, 0, jnp.where(i == 1, 1, jnp.where(i == nrp - 2, 3, jnp.where(i == nrp - 1, 4, 2))))

    return pl.pallas_call(
        functools.partial(_mixer_b_kernel, rows=rows),
        grid=(n_seq, nrp),
        in_specs=[
            pl.BlockSpec((tq, B_WIDTH), lambda b, i: (b0 + b * nrp + i, _COL_BQ)),
            pl.BlockSpec((seq, B_WIDTH), lambda b, i: (s0 + b, _COL_BK)),
            pl.BlockSpec((seq, B_WIDTH), lambda b, i: (s0 + b, _COL_BV)),
            pl.BlockSpec((1, B_HEADS, tq, _B_KROWS * GRID_W), lambda b, i: (cls(i), 0, 0, 0)),
            pl.BlockSpec((1, B_WIDTH), lambda b, i: (0, 0)),
        ],
        out_specs=pl.BlockSpec((tq, B_WIDTH), lambda b, i: (b * nrp + i, 0)),
        out_shape=jax.ShapeDtypeStruct((n_seq * seq, B_WIDTH), BF16),
        compiler_params=_cparams(("parallel", "arbitrary")),
        name="mixer_b",
    )(proj, proj, proj, bias, g.reshape(1, B_WIDTH))


def _mixer_c_kernel(q_ref, k_ref, v_ref, tiles_ref, far_ref, lq1_ref, lk1_ref, lq2_ref, lk2_ref, g_ref, o_ref,
                    *s_refs, seq, lam_init):
    i = pl.program_id(1)
    nb = seq // BLOCK
    scale = C_QK_DIM ** -0.5
    lam = (jnp.exp(jnp.sum(lq1_ref[...] * lk1_ref[...], axis=-1, keepdims=True))
           - jnp.exp(jnp.sum(lq2_ref[...] * lk2_ref[...], axis=-1, keepdims=True)) + lam_init)
    before = lax.broadcasted_iota(I32, (1, seq), 1) < i * BLOCK
    lane = lax.broadcasted_iota(I32, (BLOCK, HEAD_DIM), 1)
    for h in range(C_HEADS):
        sl = slice(h * HEAD_DIM, (h + 1) * HEAD_DIM)
        qh = q_ref[:, sl] * scale
        kh = k_ref[:, sl]
        qs = jnp.concatenate(
            [jnp.where((lane >= c * C_QK_DIM) & (lane < (c + 1) * C_QK_DIM), qh, jnp.zeros_like(qh)) for c in range(2)],
            axis=0)
        s = lax.dot_general(qs, kh, (((1,), (1,)), ((), ())), preferred_element_type=F32)
        for c in range(2):
            hc = 2 * h + c
            s_refs[hc][...] = s[c * BLOCK:(c + 1) * BLOCK] + jnp.where(before, far_ref[hc, 0], far_ref[hc, 1])
    outs = []
    for h in range(C_HEADS):
        sl = slice(h * HEAD_DIM, (h + 1) * HEAD_DIM)
        exps, sums = [], []
        for c in range(2):
            hc = 2 * h + c
            for d in range(3):
                j = i + (d - 1)
                inside = jnp.where((j >= 0) & (j < nb), 1.0, 0.0)
                cols = pl.ds(pl.multiple_of(jnp.clip(j, 0, nb - 1) * BLOCK, BLOCK), BLOCK)
                s_refs[hc][:, cols] += tiles_ref[hc, d] * inside
            s = s_refs[hc][...]
            e = jnp.exp(s - jnp.max(s, axis=-1, keepdims=True))
            exps.append(e)
            sums.append(jnp.sum(e, axis=-1, keepdims=True))
        a = (exps[0] - (lam * sums[0] / sums[1]) * exps[1]).astype(BF16)
        o = jnp.dot(a, v_ref[:, sl], preferred_element_type=F32) / sums[0]
        ms = jnp.mean(o * o, axis=-1, keepdims=True)
        outs.append(o * lax.rsqrt(ms + EPS) * g_ref[...] * (1.0 - lam_init))
    o_ref[...] = jnp.concatenate(outs, axis=-1).astype(o_ref.dtype)


def _mixer_c(proj, tiles, far, lq1, lk1, lq2, lk2, g, *, row0, n_seq, seq, lam_init):
    nb = seq // BLOCK
    b0 = row0 // BLOCK
    s0 = row0 // seq
    vec = lambda a: a.reshape(1, C_QK_DIM)
    vspec = pl.BlockSpec((1, C_QK_DIM), lambda b, i: (0, 0))
    return pl.pallas_call(
        functools.partial(_mixer_c_kernel, seq=seq, lam_init=lam_init),
        grid=(n_seq, nb),
        in_specs=[
            pl.BlockSpec((BLOCK, C_QK), lambda b, i: (b0 + b * nb + i, _COL_CQ)),
            pl.BlockSpec((seq, C_QK), lambda b, i: (s0 + b, _COL_CK)),
            pl.BlockSpec((seq, C_WIDTH), lambda b, i: (s0 + b, _COL_CV)),
            pl.BlockSpec((2 * C_HEADS, 3, BLOCK, BLOCK), lambda b, i: (0, 0, 0, 0)),
            pl.BlockSpec(memory_space=pltpu.SMEM),
            vspec, vspec, vspec, vspec,
            pl.BlockSpec((1, C_V_DIM), lambda b, i: (0, 0)),
        ],
        out_specs=pl.BlockSpec((BLOCK, C_WIDTH), lambda b, i: (b * nb + i, 0)),
        out_shape=jax.ShapeDtypeStruct((n_seq * seq, C_WIDTH), BF16),
        scratch_shapes=[pltpu.VMEM((BLOCK, seq), F32)] * (2 * C_HEADS),
        compiler_params=_cparams(("parallel", "arbitrary")),
        name="mixer_c",
    )(proj, proj, proj, tiles, far, vec(lq1), vec(lk1), vec(lq2), vec(lk2), g.reshape(1, C_V_DIM))


def _out_proj_kernel(x_ref, ya_ref, yb_ref, yc_ref, w_ref, g_ref, x1_ref, h_ref):
    acc = jnp.dot(ya_ref[...], w_ref[0:A_WIDTH, :], preferred_element_type=F32)
    acc += jnp.dot(yb_ref[...], w_ref[A_WIDTH:A_WIDTH + B_WIDTH, :], preferred_element_type=F32)
    acc += jnp.dot(yc_ref[...], w_ref[A_WIDTH + B_WIDTH:, :], preferred_element_type=F32)
    x1 = x_ref[...] + acc
    x1_ref[...] = x1
    ms = jnp.mean(x1 * x1, axis=-1, keepdims=True)
    h_ref[...] = x1 * lax.rsqrt(ms + EPS) * g_ref[...]


def _out_proj(x, ya, yb, yc, w, g, *, tm=512):
    n, d = x.shape
    row = lambda width: pl.BlockSpec((tm, width), lambda i: (i, 0))
    return pl.pallas_call(
        _out_proj_kernel,
        grid=(n // tm,),
        in_specs=[row(d), row(A_WIDTH), row(B_WIDTH), row(C_WIDTH),
                  pl.BlockSpec((d, d), lambda i: (0, 0)),
                  pl.BlockSpec((1, d), lambda i: (0, 0))],
        out_specs=[row(d), row(d)],
        out_shape=[jax.ShapeDtypeStruct((n, d), F32), jax.ShapeDtypeStruct((n, d), F32)],
        compiler_params=_cparams(("parallel",)),
        name="out_proj",
    )(x, ya, yb, yc, w, g.reshape(1, d))


def _top16(x):
    r = float(x.shape[0])
    iota = lax.broadcasted_iota(I32, x.shape, 0).astype(F32)
    vals, idxs = [], []
    for _ in range(PEER_TOPK):
        m = jnp.max(x, axis=0, keepdims=True)
        idx = jnp.min(jnp.where(x == m, iota, r), axis=0, keepdims=True)
        vals.append(m)
        idxs.append(idx)
        x = jnp.where(iota == idx, -jnp.inf, x)
    return vals, idxs


def _peer_candidates(s1, i1, s2, i2):
    sub = 8
    s1a, i1a = jnp.concatenate(s1, axis=0), jnp.concatenate(i1, axis=0)
    s2a, i2a = jnp.concatenate(s2, axis=0), jnp.concatenate(i2, axis=0)
    row = lax.broadcasted_iota(I32, s2a[:sub].shape, 0)
    cand, cidx = [], []
    first_single = PEER_TOPK // 2
    for a in range(first_single):
        nb = PEER_TOPK // (a + 1)
        if nb > sub:
            cand.append(s1[a] + s2a)
            cidx.append(i1[a] * PEER_KEYS + i2a)
        else:
            cand.append(jnp.where(row < nb, s1[a] + s2a[:sub], -jnp.inf))
            cidx.append(i1[a] * PEER_KEYS + i2a[:sub])
    cand.append(s1a[first_single:] + s2[0])
    cidx.append(i1a[first_single:] * PEER_KEYS + i2[0])
    return jnp.concatenate(cand, axis=0), jnp.concatenate(cidx, axis=0)


def _gelu_exact(x):
    return 0.5 * x * (1.0 + lax.erf(x * (2.0 ** -0.5)))


_PEER_T = 8
_DMA_QUEUES = 2
_RING = 3


def _pack_expert_tables(u, v):
    half = u.shape[1] // 2
    bits = lambda a: lax.bitcast_convert_type(a.astype(BF16), jnp.uint16).astype(jnp.uint32)
    lo = jnp.concatenate([bits(u[:, :half]), bits(v[:, :half])], axis=1)
    hi = jnp.concatenate([bits(u[:, half:]), bits(v[:, half:])], axis=1)
    return lax.bitcast_convert_type(lo | (hi << 16), I32)


_MACRO = 256
_MACRO_STEPS = _MACRO // _PEER_T
_ROUTE_PHASES = 3 * PEER_HEADS + 2


def _route_experts_kernel(h_ref, x_ref, gf_ref, hall_hbm, wq_hbm, keys_hbm, uv_hbm, o_ref,
                          wq_ref, keys_ref, hbuf, q_ref, sub_s, sub_i, idx_pk, gate_pk, g2_ref, idx_tok, gate_tok,
                          idx_s0, idx_s1, idx_s2, buf0, buf1, buf2, a_ref, w_sem, idx_sem, row_sem,
                          *, final_norm, n_tiles):
    i = pl.program_id(0)
    n = pl.num_programs(0)
    bufs = (buf0, buf1, buf2)
    idx_smem = (idx_s0, idx_s1, idx_s2)
    half = h_ref.shape[1] // 2

    def h_copy(mt):
        return pltpu.make_async_copy(hall_hbm.at[pl.ds(pl.multiple_of(mt * _MACRO, _MACRO), _MACRO)], hbuf, w_sem)

    def phase_project(mt):
        h_copy(mt).wait()
        q_ref[...] = jnp.dot(hbuf[...].astype(BF16), wq_ref[...], preferred_element_type=F32).astype(BF16)

    def phase_half(hc, mt):
        del mt
        qc = q_ref[:, hc * PEER_HALF:(hc + 1) * PEER_HALF]
        sc = lax.dot_general(keys_ref[hc], qc, (((1,), (1,)), ((), ())), preferred_element_type=F32)
        vals, idxs = _top16(sc)
        sub_s[hc % 2] = jnp.concatenate(vals, axis=0)
        sub_i[hc % 2] = jnp.concatenate(idxs, axis=0)

    def phase_head(hp, mt):
        del mt
        rows16 = lambda ref, c: [ref[c, j:j + 1, :] for j in range(PEER_TOPK)]
        s1, i1, s2, i2 = rows16(sub_s, 0), rows16(sub_i, 0), rows16(sub_s, 1), rows16(sub_i, 1)
        cand, cidx = _peer_candidates(s1, i1, s2, i2)
        top_s, pos = _top16(cand)
        iota = lax.broadcasted_iota(I32, cand.shape, 0).astype(F32)
        eidx = [jnp.sum(jnp.where(iota == p, cidx, 0.0), axis=0, keepdims=True) for p in pos]
        ts = jnp.concatenate(top_s, axis=0)
        e = jnp.exp(ts - top_s[0])
        rows = slice(hp * PEER_TOPK, (hp + 1) * PEER_TOPK)
        gate_pk[rows, :] = e / jnp.sum(e, axis=0, keepdims=True)
        idx_pk[rows, :] = jnp.concatenate(eidx, axis=0)

    def phase_finalize(mt):
        slot = mt % 2
        lanes = 128
        for hf in range(_MACRO // lanes):
            toks = slice(hf * lanes, (hf + 1) * lanes)
            g = gate_pk[:, toks]
            g2_ref[hf, pl.ds(0, PEER_PICKS, stride=2), :] = g
            g2_ref[hf, pl.ds(1, PEER_PICKS, stride=2), :] = jnp.zeros_like(g)
            gate_tok[slot, toks, :] = g2_ref[hf].T
            idx_tok[slot, toks, :] = (idx_pk[:, toks].astype(I32) * D_MODEL).T

        @pl.when(mt + 1 < n_tiles)
        def _():
            h_copy(mt + 1).start()

    phases = [phase_project]
    for hp in range(PEER_HEADS):
        phases += [functools.partial(phase_half, 2 * hp), functools.partial(phase_half, 2 * hp + 1),
                   functools.partial(phase_head, hp)]
    phases.append(phase_finalize)
    assert len(phases) == _ROUTE_PHASES <= _MACRO_STEPS

    def run_phase(ph, mt):
        lax.switch(ph, [lambda mt, f=f: f(mt) for f in phases], mt)

    def idx_copy(step, slot):
        src = idx_tok.at[(step // _MACRO_STEPS) % 2, pl.ds(pl.multiple_of((step % _MACRO_STEPS) * _PEER_T, _PEER_T), _PEER_T)]
        return pltpu.make_async_copy(src, idx_smem[slot], idx_sem.at[slot])

    def issue_token(slot, t):
        for k in range(PEER_PICKS):
            off = pl.multiple_of(idx_smem[slot][t, k], D_MODEL)
            copy = pltpu.make_async_copy(uv_hbm.at[pl.ds(off, D_MODEL)], bufs[slot].at[t, k], row_sem.at[slot])
            copy.start(priority=k % _DMA_QUEUES)

    def wait_rows(slot):
        pltpu.make_async_copy(bufs[slot], bufs[slot], row_sem.at[slot]).wait()

    @pl.when(i == 0)
    def _():
        cw = pltpu.make_async_copy(wq_hbm, wq_ref, w_sem)
        cw.start()
        cw.wait()
        ck = pltpu.make_async_copy(keys_hbm, keys_ref, w_sem)
        ck.start()
        ck.wait()
        h_copy(0).start()

    def routed(r, carry):
        first = (i == 0) & (r < _ROUTE_PHASES)
        ph = jnp.where(first, r, i % _MACRO_STEPS)
        mt = jnp.where(first, 0, i // _MACRO_STEPS + 1)

        @pl.when(first | ((ph < _ROUTE_PHASES) & (mt < n_tiles)))
        def _():
            run_phase(ph, mt)
        return carry
    lax.fori_loop(0, jnp.where(i == 0, _ROUTE_PHASES + 1, 1), routed, 0)

    @pl.when(i == 0)
    def _():
        for s0 in range(_RING - 1):
            idx_copy(s0, s0).start()
            idx_copy(s0, s0).wait()
            lax.fori_loop(0, _PEER_T, lambda t, c, s0=s0: (issue_token(s0, t), c)[1], 0)
        idx_copy(_RING - 1, _RING - 1).start()

    def step(cur, nxt, new):
        idx_copy(i + _RING - 1, new).wait()
        idx_copy(i + _RING, cur).start()
        wait_rows(cur)
        buf = bufs[cur]
        tile_slot = (i // _MACRO_STEPS) % 2
        tile_row = pl.multiple_of((i % _MACRO_STEPS) * _PEER_T, _PEER_T)

        h = h_ref[...]
        hh = jnp.concatenate([h[:, :half], h[:, half:]], axis=0).astype(BF16)

        @pl.when(i >= 0)
        def _():
            for t in range(_PEER_T // 2):
                issue_token(new, t)
            tok = lax.broadcasted_iota(I32, (_PEER_T, 2 * PEER_PICKS), 0)
            a = jnp.zeros((_PEER_T, 2 * PEER_PICKS), F32)
            for t in range(_PEER_T):
                urows = pltpu.bitcast(buf[t, :, :half], BF16)
                r = lax.dot_general(hh, urows, (((1,), (1,)), ((), ())), preferred_element_type=F32)
                r = r[:_PEER_T] + pltpu.roll(r[_PEER_T:], 2 * PEER_PICKS - 1, axis=1)
                a = jnp.where(tok == t, r, a)
            a_ref[...] = a

        for t in range(_PEER_T // 2, _PEER_T):
            issue_token(new, t)
        w_even = gate_tok[tile_slot, pl.ds(tile_row, _PEER_T), :] * _gelu_exact(a_ref[...])
        w = jnp.concatenate([w_even, pltpu.roll(w_even, 1, axis=1)], axis=0).astype(BF16)
        tok = lax.broadcasted_iota(I32, (_PEER_T, half), 0)
        lo = jnp.zeros((_PEER_T, half), F32)
        hi = jnp.zeros((_PEER_T, half), F32)
        for t in range(_PEER_T):
            vrows = pltpu.bitcast(buf[t, :, half:], BF16)
            o = jnp.dot(w, vrows, preferred_element_type=F32)
            lo = jnp.where(tok == t, o[:_PEER_T], lo)
            hi = jnp.where(tok == t, o[_PEER_T:], hi)
        y = x_ref[...] + jnp.concatenate([lo, hi], axis=1)
        if final_norm:
            ms = jnp.mean(y * y, axis=-1, keepdims=True)
            y = y * lax.rsqrt(ms + EPS) * gf_ref[...]
        o_ref[...] = y

        @pl.when(i == n - 1)
        def _():
            wait_rows(nxt)
            wait_rows(new)
            idx_copy(i + _RING, cur).wait()

    for r in range(_RING):
        pl.when(i % _RING == r)(functools.partial(step, r, (r + 1) % _RING, (r + 2) % _RING))


def _peer_route_experts(h, x, wq, keys, uv, g_final, *, final_norm):
    n, d = x.shape
    assert n % _MACRO == 0 and n // _MACRO >= 2
    steps = n // _PEER_T
    row = lambda width: pl.BlockSpec((_PEER_T, width), lambda i: (i, 0))
    anyspec = pl.BlockSpec(memory_space=pl.ANY)
    return pl.pallas_call(
        functools.partial(_route_experts_kernel, final_norm=final_norm, n_tiles=n // _MACRO),
        grid=(steps,),
        in_specs=[row(d), row(d), pl.BlockSpec((1, d), lambda i: (0, 0)), anyspec, anyspec, anyspec, anyspec],
        out_specs=row(d),
        out_shape=jax.ShapeDtypeStruct((n, d), F32),
        scratch_shapes=[pltpu.VMEM(wq.shape, BF16),
                        pltpu.VMEM(keys.shape, BF16),
                        pltpu.VMEM((_MACRO, d), F32),
                        pltpu.VMEM((_MACRO, wq.shape[1]), BF16),
                        pltpu.VMEM((2, PEER_TOPK, _MACRO), F32),
                        pltpu.VMEM((2, PEER_TOPK, _MACRO), F32),
                        pltpu.VMEM((PEER_PICKS, _MACRO), F32),
                        pltpu.VMEM((PEER_PICKS, _MACRO), F32),
                        pltpu.VMEM((_MACRO // 128, 2 * PEER_PICKS, 128), F32),
                        pltpu.VMEM((2, _MACRO, PEER_PICKS), I32),
                        pltpu.VMEM((2, _MACRO, 2 * PEER_PICKS), F32),
                        *[pltpu.SMEM((_PEER_T, PEER_PICKS), I32)] * _RING,
                        *[pltpu.VMEM((_PEER_T, PEER_PICKS, d), I32)] * _RING,
                        pltpu.VMEM((_PEER_T, 2 * PEER_PICKS), F32),
                        pltpu.SemaphoreType.DMA,
                        pltpu.SemaphoreType.DMA((_RING,)),
                        pltpu.SemaphoreType.DMA((_RING,))],
        compiler_params=_cparams(("arbitrary",)),
        name="peer_route_experts",
    )(h, x, g_final.reshape(1, d), h, wq, keys, uv.reshape(-1))


def _encoder(xs, rel_bias, g_attn, w_in, a_sink, g_grp_a, nat_rpb, g_grp_b, lam_q1, lam_k1, lam_q2, lam_k2,
             g_subln, w_out, g_ffn, peer_wq, peer_keys, peer_u, peer_v, g_final):
    depth = w_in.shape[0]
    d = xs[0].shape[-1]
    splits = [int(c) for c in np.cumsum((0,) + IN_SIZES)]
    bias_a = _bias_a_table(rel_bias[:, :A_Q_HEADS])
    tiles_c, far_c = _bias_c_tables(rel_bias[:, A_Q_HEADS:])
    shapes = [x.shape[:2] for x in xs]
    xs = [x.reshape(-1, d) for x in xs]
    for l in range(depth):
        lam_init = 0.8 - 0.6 * math.exp(-0.3 * l)
        w_l = jnp.concatenate([w_in[l][:, splits[s]:splits[s + 1]] for s in _PROJ_ORDER], axis=1).astype(BF16)
        w_o = w_out[l].astype(BF16)
        w_q = peer_wq[l].astype(BF16)
        bias_b = _bias_b_table(nat_rpb[l])
        keys = peer_keys[l].reshape(2 * PEER_HEADS, PEER_KEYS, PEER_HALF).astype(BF16)
        uv = _pack_expert_tables(peer_u[l], peer_v[l])
        for b, (n_seq, seq) in enumerate(shapes):
            x = xs[b]
            n = x.shape[0]
            kw = dict(row0=0, n_seq=n_seq, seq=seq)
            proj = _norm_proj(x, g_attn[l], w_l)
            ya = _mixer_a(proj, bias_a, a_sink[l], g_grp_a[l], **kw)
            yb = _mixer_b(proj, bias_b, g_grp_b[l], **kw)
            yc = _mixer_c(proj, tiles_c, far_c, lam_q1[l], lam_k1[l], lam_q2[l], lam_k2[l], g_subln[l],
                          lam_init=lam_init, **kw)
            x1, h2 = _out_proj(x, ya, yb, yc, w_o, g_ffn[l])
            xs[b] = _peer_route_experts(h2, x1, w_q, keys, uv, g_final, final_norm=(l == depth - 1))
    return tuple(x.reshape(n_seq, seq, d) for x, (n_seq, seq) in zip(xs, shapes))


def kernel(x_prompt, x_sample, rel_bias, g_attn, w_in, a_sink, g_grp_a, nat_rpb, g_grp_b, lam_q1, lam_k1, lam_q2, lam_k2, g_subln, w_out, g_ffn, peer_wq, peer_keys, peer_u, peer_v, g_final):
    return _encoder([x_prompt, x_sample], rel_bias, g_attn, w_in, a_sink, g_grp_a, nat_rpb, g_grp_b, lam_q1, lam_k1,
                    lam_q2, lam_k2, g_subln, w_out, g_ffn, peer_wq, peer_keys, peer_u, peer_v, g_final)
```
